```python
import math
import jax, jax.numpy as jnp
from jax import lax
import numpy as np

D_MODEL = 1024
BATCH = 1
SEQ = 16384
DEPTH = 4

N_A_LAYERS = DEPTH // 2
N_B_LAYERS = DEPTH - N_A_LAYERS
A_HEADS = 8
A_DK = 128
A_DV = 128
A_QK = A_HEADS * A_DK
A_V = A_HEADS * A_DV
A_CONV = 4
A_CONV_CH = 2 * A_QK + A_V
A_IN = 2 * A_QK + 2 * A_V + 2 * A_HEADS
A_CHUNK = 64
B_HEADS = 8
B_DH = 64
B_QK = B_HEADS * 2 * B_DH
B_V = B_HEADS * 2 * B_DH
B_QBLOCK = 128
REL_BUCKETS = 32
REL_MAX_DIST = 128
N_EXPERTS = 16
N_GROUPS = 4
EXPERTS_PER_GROUP = N_EXPERTS // N_GROUPS
TOP_K = 2
D_EXPERT = 512
DN_ALPHA = (2 * DEPTH) ** 0.25
DN_BETA = (8 * DEPTH) ** -0.25
LN_EPS = 1e-5
RMS_EPS = 1e-6

kernel_name = 'hybrid_gdn_diffattn_yoco_moe'

F32 = jnp.float32


def layer_norm(x, w, b):
    xf = x.astype(F32)
    mu = xf.mean(-1, keepdims=True)
    var = jnp.square(xf - mu).mean(-1, keepdims=True)
    return ((xf - mu) * lax.rsqrt(var + LN_EPS) * w.astype(F32) + b.astype(F32)).astype(x.dtype)


def rms_norm_f32(x, w):
    xf = x.astype(F32)
    return xf * lax.rsqrt(jnp.mean(xf * xf, -1, keepdims=True) + RMS_EPS) * w.astype(F32)


def l2norm(x):
    return x * lax.rsqrt(jnp.sum(x * x, -1, keepdims=True) + RMS_EPS)


def causal_depthwise_conv(x, w):
    c = x.shape[-1]
    return lax.conv_general_dilated(x, w[:, None, :], window_strides=(1,),
                                    padding=[(w.shape[0] - 1, 0)],
                                    dimension_numbers=('NWC', 'WIO', 'NWC'),
                                    feature_group_count=c)


def chunked_gated_delta_rule(q, k, v, g, beta):
    bsz, t_len, h, dk = q.shape
    dv = v.shape[-1]
    nc = t_len // A_CHUNK

    def chunks(t):
        t = t.reshape((bsz, nc, A_CHUNK, h) + t.shape[3:])
        return jnp.moveaxis(t, 3, 1)

    q, k, v, g, beta = map(chunks, (q, k, v, g, beta))
    g = jnp.cumsum(g, axis=-1)
    idx = jnp.arange(A_CHUNK)
    incl = idx[:, None] >= idx[None, :]
    strict = idx[:, None] > idx[None, :]
    decay = jnp.exp(jnp.where(incl, g[..., :, None] - g[..., None, :], -jnp.inf))
    kk = jnp.einsum('bhncd,bhnsd->bhncs', k, k)
    m = jnp.where(strict, beta[..., :, None] * kk * decay, 0.0)
    a_mat = m + jnp.eye(A_CHUNK, dtype=F32)
    u = lax.linalg.triangular_solve(a_mat, beta[..., None] * v, left_side=True,
                                    lower=True, unit_diagonal=True)
    w = lax.linalg.triangular_solve(a_mat, (beta * jnp.exp(g))[..., None] * k, left_side=True,
                                    lower=True, unit_diagonal=True)
    qk = jnp.einsum('bhncd,bhnsd->bhncs', q, k) * decay
    g_last = g[..., -1:]
    q_dec = q * jnp.exp(g)[..., None]
    k_dec = k * jnp.exp(g_last - g)[..., None]
    chunk_decay = jnp.exp(g_last)[..., None]
    xs = tuple(jnp.moveaxis(t, 2, 0) for t in (q_dec, k_dec, u, w, qk, chunk_decay))

    def step(s, inp):
        q_c, k_c, u_c, w_c, qk_c, dec_c = inp
        v_new = u_c - jnp.einsum('bhcd,bhde->bhce', w_c, s)
        o_c = jnp.einsum('bhcd,bhde->bhce', q_c, s) + jnp.einsum('bhcs,bhse->bhce', qk_c, v_new)
        s = s * dec_c + jnp.einsum('bhcd,bhce->bhde', k_c, v_new)
        return s, o_c

    s0 = jnp.zeros((bsz, h, dk, dv), F32)
    _, o = lax.scan(step, s0, xs)
    return o.transpose(1, 0, 3, 2, 4).reshape(bsz, t_len, h, dv)


def gated_deltanet_mixer(x, w_in, conv_w, a_log, dt_bias, norm_w, w_out):
    bsz, t_len, _ = x.shape
    proj = x @ w_in
    qkv, z, a, b = jnp.split(proj, [A_CONV_CH, A_CONV_CH + A_V, A_CONV_CH + A_V + A_HEADS], axis=-1)
    qkv = jax.nn.silu(causal_depthwise_conv(qkv, conv_w))
    q, k, v = jnp.split(qkv, [A_QK, 2 * A_QK], axis=-1)
    q = l2norm(q.reshape(bsz, t_len, A_HEADS, A_DK).astype(F32)) * (A_DK ** -0.5)
    k = l2norm(k.reshape(bsz, t_len, A_HEADS, A_DK).astype(F32))
    v = v.reshape(bsz, t_len, A_HEADS, A_DV).astype(F32)
    beta = jax.nn.sigmoid(b.astype(F32))
    g = -jnp.exp(a_log.astype(F32)) * jax.nn.softplus(a.astype(F32) + dt_bias.astype(F32))
    o = chunked_gated_delta_rule(q, k, v, g, beta)
    o = rms_norm_f32(o, norm_w) * jax.nn.silu(z.reshape(bsz, t_len, A_HEADS, A_DV).astype(F32))
    return o.reshape(bsz, t_len, A_V).astype(x.dtype) @ w_out


def t5_bucket(n):
    max_exact = REL_BUCKETS // 2
    nf = jnp.maximum(n, 1).astype(F32)
    large = max_exact + (jnp.log(nf / max_exact) / math.log(REL_MAX_DIST / max_exact)
                         * (REL_BUCKETS - max_exact)).astype(jnp.int32)
    large = jnp.minimum(large, REL_BUCKETS - 1)
    return jnp.where(n < max_exact, n, large)


def shared_kv(x, w_kv):
    bsz, t_len, _ = x.shape
    kv = x @ w_kv
    k = kv[..., :B_QK].reshape(bsz, t_len, B_HEADS, 2, B_DH).transpose(0, 2, 3, 1, 4)
    v = kv[..., B_QK:].reshape(bsz, t_len, B_HEADS, 2 * B_DH).transpose(0, 2, 1, 3)
    return k, v


def diff_attention_mixer(x, k_sh, v_sh, dist_bias, w_q, lam, subln_w, w_out, lambda_init):
    bsz, t_len, _ = x.shape
    nb = t_len // B_QBLOCK
    q = (x @ w_q).reshape(bsz, nb, B_QBLOCK, B_HEADS, 2, B_DH).transpose(1, 0, 3, 4, 2, 5)
    lam_f = lam.astype(F32)
    lam_val = (jnp.exp(jnp.sum(lam_f[0] * lam_f[1])) - jnp.exp(jnp.sum(lam_f[2] * lam_f[3]))
               + lambda_init)
    k_pos = jnp.arange(t_len)
    scale = B_DH ** -0.5

    def one_block(args):
        q_blk, blk = args
        q_pos = blk * B_QBLOCK + jnp.arange(B_QBLOCK)
        rel = q_pos[:, None] - k_pos[None, :]
        bias = dist_bias[:, jnp.clip(rel, 0, t_len - 1)]
        s = jnp.einsum('bhiqd,bhikd->bhiqk', q_blk, k_sh).astype(F32) * scale + bias[None, :, None]
        s = jnp.where(rel >= 0, s, -jnp.inf)
        p = jax.nn.softmax(s, axis=-1)
        attn = p[:, :, 0] - lam_val * p[:, :, 1]
        return jnp.einsum('bhqk,bhkd->bhqd', attn.astype(v_sh.dtype), v_sh)

    o = lax.map(one_block, (q, jnp.arange(nb)))
    o = o.transpose(1, 0, 3, 2, 4).reshape(bsz, t_len, B_HEADS, 2 * B_DH)
    o = rms_norm_f32(o, subln_w) * (1.0 - lambda_init)
    return o.reshape(bsz, t_len, B_V).astype(x.dtype) @ w_out


def grouped_moe(x, router_w, router_b, w_gate, w_up, w_down):
    bsz, t_len, _ = x.shape
    probs = jax.nn.softmax((x @ router_w).astype(F32) + router_b.astype(F32), axis=-1)
    pg = probs.reshape(bsz, t_len, N_GROUPS, EXPERTS_PER_GROUP)
    top_in_group, _ = lax.top_k(pg, TOP_K)
    best_group = jnp.argmax(top_in_group.sum(-1), axis=-1)
    group_probs = jnp.einsum('btge,btg->bte', pg, jax.nn.one_hot(best_group, N_GROUPS, dtype=F32))
    top_w, top_i = lax.top_k(group_probs, TOP_K)
    top_w = top_w / top_w.sum(-1, keepdims=True)
    expert_idx = best_group[..., None] * EXPERTS_PER_GROUP + top_i
    gate = jnp.sum(jax.nn.one_hot(expert_idx, N_EXPERTS, dtype=F32) * top_w[..., None], axis=-2)
    gate = gate.astype(x.dtype)
    y = jnp.zeros_like(x)
    for e in range(N_EXPERTS):
        h = jax.nn.silu(x @ w_gate[e]) * (x @ w_up[e])
        y = y + gate[..., e:e + 1] * (h @ w_down[e])
    return y


def lambda_init_for(layer):
    return 0.8 - 0.6 * math.exp(-0.3 * layer)


def setup_inputs(seed: int = 0) -> dict:
    key = jax.random.key(seed)
    ks = jax.random.split(key, 20)

    def nrm(k, shape, scale):
        return jax.random.normal(k, shape, F32) * scale

    x = nrm(ks[0], (BATCH, SEQ, D_MODEL), 1.0)
    a_w_in = nrm(ks[1], (N_A_LAYERS, D_MODEL, A_IN), D_MODEL ** -0.5)
    a_conv = nrm(ks[2], (N_A_LAYERS, A_CONV, A_CONV_CH), A_CONV ** -0.5)
    a_A_log = jnp.log(jax.random.uniform(ks[3], (N_A_LAYERS, A_HEADS), F32, 1.0, 16.0))
    dt = jnp.exp(jax.random.uniform(ks[4], (N_A_LAYERS, A_HEADS), F32, math.log(1e-3), math.log(1e-1)))
    a_dt_bias = dt + jnp.log(-jnp.expm1(-dt))
    a_norm_w = 1.0 + nrm(ks[5], (N_A_LAYERS, A_DV), 0.05)
    a_w_out = nrm(ks[6], (N_A_LAYERS, A_V, D_MODEL), DN_BETA * A_V ** -0.5)
    b_w_q = nrm(ks[7], (N_B_LAYERS, D_MODEL, B_QK), D_MODEL ** -0.5)
    b_w_kv = nrm(ks[8], (D_MODEL, B_QK + B_V), D_MODEL ** -0.5)
    b_lambda = nrm(ks[9], (N_B_LAYERS, 4, B_DH), 0.1)
    b_subln_w = 1.0 + nrm(ks[10], (N_B_LAYERS, 2 * B_DH), 0.05)
    b_w_out = nrm(ks[11], (N_B_LAYERS, B_V, D_MODEL), DN_BETA * B_V ** -0.5)
    rel_bias = nrm(ks[12], (REL_BUCKETS, B_HEADS), 0.5)
    router_w = nrm(ks[13], (D_MODEL, N_EXPERTS), D_MODEL ** -0.5)
    router_b = nrm(ks[14], (N_EXPERTS,), 0.01)
    moe_w_gate = nrm(ks[15], (DEPTH, N_EXPERTS, D_MODEL, D_EXPERT), D_MODEL ** -0.5)
    moe_w_up = nrm(ks[16], (DEPTH, N_EXPERTS, D_MODEL, D_EXPERT), D_MODEL ** -0.5)
    moe_w_down = nrm(ks[17], (DEPTH, N_EXPERTS, D_EXPERT, D_MODEL), DN_BETA * D_EXPERT ** -0.5)
    ln_w = 1.0 + nrm(ks[18], (DEPTH, 2, D_MODEL), 0.05)
    ln_b = nrm(ks[19], (DEPTH, 2, D_MODEL), 0.02)
    return {'x': x, 'a_w_in': a_w_in, 'a_conv': a_conv, 'a_A_log': a_A_log,
            'a_dt_bias': a_dt_bias, 'a_norm_w': a_norm_w, 'a_w_out': a_w_out,
            'b_w_q': b_w_q, 'b_w_kv': b_w_kv, 'b_lambda': b_lambda, 'b_subln_w': b_subln_w,
            'b_w_out': b_w_out, 'rel_bias': rel_bias, 'router_w': router_w, 'router_b': router_b,
            'moe_w_gate': moe_w_gate, 'moe_w_up': moe_w_up, 'moe_w_down': moe_w_down,
            'ln_w': ln_w, 'ln_b': ln_b}


def reference(x, a_w_in, a_conv, a_A_log, a_dt_bias, a_norm_w, a_w_out, b_w_q, b_w_kv,
              b_lambda, b_subln_w, b_w_out, rel_bias, router_w, router_b,
              moe_w_gate, moe_w_up, moe_w_down, ln_w, ln_b):
    t_len = x.shape[1]
    dist_bias = rel_bias[t5_bucket(jnp.arange(t_len))].T.astype(F32)
    k_sh = None
    v_sh = None
    for l in range(DEPTH):
        if l < N_A_LAYERS:
            h = gated_deltanet_mixer(x, a_w_in[l], a_conv[l], a_A_log[l], a_dt_bias[l],
                                     a_norm_w[l], a_w_out[l])
        else:
            if l == N_A_LAYERS:
                k_sh, v_sh = shared_kv(x, b_w_kv)
            j = l - N_A_LAYERS
            h = diff_attention_mixer(x, k_sh, v_sh, dist_bias, b_w_q[j], b_lambda[j],
                                     b_subln_w[j], b_w_out[j], lambda_init_for(l))
        x = layer_norm(DN_ALPHA * x + h, ln_w[l, 0], ln_b[l, 0])
        y = grouped_moe(x, router_w, router_b, moe_w_gate[l], moe_w_up[l], moe_w_down[l])
        x = layer_norm(DN_ALPHA * x + y, ln_w[l, 1], ln_b[l, 1])
    return x
```

```python
import functools
import math

import jax
import jax.numpy as jnp
import numpy as np
from jax import lax
from jax.experimental import pallas as pl
from jax.experimental.pallas import tpu as pltpu

F32 = jnp.float32
BF16 = jnp.bfloat16

D_MODEL = 1024
DEPTH = 4
N_A_LAYERS = DEPTH // 2
A_HEADS = 8
A_DK = 128
A_DV = 128
A_QK = A_HEADS * A_DK
A_V = A_HEADS * A_DV
A_CONV = 4
A_CONV_CH = 2 * A_QK + A_V
A_CHUNK = 64
B_HEADS = 8
B_DH = 64
B_QK = B_HEADS * 2 * B_DH
REL_BUCKETS = 32
REL_MAX_DIST = 128
N_EXPERTS = 16
N_GROUPS = 4
EXPERTS_PER_GROUP = N_EXPERTS // N_GROUPS
D_EXPERT = 512
DN_ALPHA = (2 * DEPTH) ** 0.25
LN_EPS = 1e-5
RMS_EPS = 1e-6

LANES = 128
SUBLANES = 8
VMEM_LIMIT = 48 * 1024 * 1024
MASK_VALUE = -1e30

HIGHEST = lax.Precision.HIGHEST
NT_DIMS = (((1,), (1,)), ((), ()))
TN_DIMS = (((0,), (0,)), ((), ()))


def _params(*sem):
    return pltpu.CompilerParams(dimension_semantics=sem, vmem_limit_bytes=VMEM_LIMIT)


def _sigmoid(x):
    return 1.0 / (1.0 + jnp.exp(-x))


def _silu(x):
    return x * _sigmoid(x)


def _layer_norm(v, w, b):
    mu = jnp.mean(v, axis=-1, keepdims=True)
    c = v - mu
    var = jnp.mean(c * c, axis=-1, keepdims=True)
    return c * lax.rsqrt(var + LN_EPS) * w + b


def _mm_kernel(a_ref, w_ref, o_ref, *, scale):
    y = jnp.dot(a_ref[...], w_ref[...], preferred_element_type=F32)
    if scale != 1.0:
        y = y * scale
    o_ref[...] = y.astype(o_ref.dtype)


def matmul_bf16(a, w, *, scale=1.0, tm=512, tn=1024):
    m, k = a.shape
    n = w.shape[1]
    tm = min(tm, m)
    return pl.pallas_call(
        functools.partial(_mm_kernel, scale=scale),
        grid=(n // tn, m // tm),
        in_specs=[pl.BlockSpec((tm, k), lambda j, i: (i, 0)),
                  pl.BlockSpec((k, tn), lambda j, i: (0, j))],
        out_specs=pl.BlockSpec((tm, tn), lambda j, i: (i, j)),
        out_shape=jax.ShapeDtypeStruct((m, n), BF16),
        compiler_params=_params("parallel", "arbitrary"),
        name="matmul_bf16",
    )(a, w)


def _mm_ln_kernel(a_ref, w_ref, x_ref, lw_ref, lb_ref, o_ref, ob_ref):
    h = jnp.dot(a_ref[...], w_ref[...], preferred_element_type=F32)
    y = _layer_norm(DN_ALPHA * x_ref[...] + h, lw_ref[...], lb_ref[...])
    o_ref[...] = y
    ob_ref[...] = y.astype(BF16)


def matmul_residual_ln(a, w, x, lw, lb, *, tm=512):
    m, k = a.shape
    n = w.shape[1]
    tm = min(tm, m)
    return pl.pallas_call(
        _mm_ln_kernel,
        grid=(m // tm,),
        in_specs=[pl.BlockSpec((tm, k), lambda i: (i, 0)),
                  pl.BlockSpec((k, n), lambda i: (0, 0)),
                  pl.BlockSpec((tm, n), lambda i: (i, 0)),
                  pl.BlockSpec((1, n), lambda i: (0, 0)),
                  pl.BlockSpec((1, n), lambda i: (0, 0))],
        out_specs=[pl.BlockSpec((tm, n), lambda i: (i, 0)),
                   pl.BlockSpec((tm, n), lambda i: (i, 0))],
        out_shape=[jax.ShapeDtypeStruct((m, n), F32),
                   jax.ShapeDtypeStruct((m, n), BF16)],
        compiler_params=_params("parallel"),
        name="matmul_residual_ln",
    )(a, w, x, lw.reshape(1, n), lb.reshape(1, n))


def _gdn_inproj_kernel(x_ref, w_ref, cw_ref, o_ref, halo_ref):
    n = pl.program_id(0)
    m = pl.program_id(1)
    tm, tn = o_ref.shape
    y = jnp.dot(x_ref[...], w_ref[...], preferred_element_type=F32)

    @pl.when(n == 3)
    def _():
        o_ref[...] = y.astype(o_ref.dtype)

    @pl.when(n < 3)
    def _():
        @pl.when(m == 0)
        def _():
            halo_ref[...] = jnp.zeros_like(halo_ref)

        halo = halo_ref[...]
        cw = cw_ref[...]
        rows = lax.broadcasted_iota(jnp.int32, (SUBLANES, tn), 0)
        acc = y * cw[A_CONV - 1:A_CONV]
        for shift in range(1, A_CONV):
            ys = pltpu.roll(y, shift, 0)
            top = jnp.where(rows < shift, pltpu.roll(halo, shift, 0), ys[:SUBLANES])
            ys = jnp.concatenate([top, ys[SUBLANES:]], axis=0)
            acc = acc + ys * cw[A_CONV - 1 - shift:A_CONV - shift]
        halo_ref[...] = y[tm - SUBLANES:]
        a = _silu(acc)

        @pl.when(n == 2)
        def _():
            o_ref[...] = a.astype(o_ref.dtype)

        @pl.when(n < 2)
        def _():
            post = jnp.where(n == 0, A_DK ** -0.5, 1.0).astype(F32)
            for h in range(tn // A_DK):
                seg = a[:, h * A_DK:(h + 1) * A_DK]
                ss = jnp.sum(seg * seg, axis=-1, keepdims=True)
                o_ref[:, h * A_DK:(h + 1) * A_DK] = (seg * (lax.rsqrt(ss + RMS_EPS) * post)).astype(o_ref.dtype)


def gdn_inproj(xb, w_main, conv_w_pad, *, tm=512):
    t, k = xb.shape
    n = w_main.shape[1]
    tn = A_QK
    tm = min(tm, t)
    return pl.pallas_call(
        _gdn_inproj_kernel,
        grid=(n // tn, t // tm),
        in_specs=[pl.BlockSpec((tm, k), lambda j, i: (i, 0)),
                  pl.BlockSpec((k, tn), lambda j, i: (0, j)),
                  pl.BlockSpec((A_CONV, tn), lambda j, i: (0, j))],
        out_specs=pl.BlockSpec((tm, tn), lambda j, i: (i, j)),
        out_shape=jax.ShapeDtypeStruct((t, n), BF16),
        scratch_shapes=[pltpu.VMEM((SUBLANES, tn), F32)],
        compiler_params=_params("arbitrary", "arbitrary"),
        name="gdn_inproj",
    )(xb, w_main, conv_w_pad)


def _gdn_gates_kernel(x_ref, w_ref, par_ref, tri_ref, fac_ref, fact_ref):
    ab = jnp.dot(x_ref[...], w_ref[...], preferred_element_type=F32)
    lane = lax.broadcasted_iota(jnp.int32, ab.shape, 1)
    neg_a = -jnp.exp(par_ref[0:1, :])
    z = ab + par_ref[1:2, :]
    softplus = jnp.maximum(z, 0.0) + jnp.log1p(jnp.exp(-jnp.abs(z)))
    g = jnp.where(lane < A_HEADS, neg_a * softplus, 0.0)
    beta = _sigmoid(ab)
    gc = jnp.dot(tri_ref[...], g, preferred_element_type=F32, precision=HIGHEST)
    fac = jnp.where(lane < A_HEADS, gc, jnp.where(lane < 2 * A_HEADS, beta, 0.0))
    fac_ref[...] = fac
    fact_ref[...] = fac.T[:2 * A_HEADS, :]


def gdn_gates(xb, w_ab_pad, par, tri, *, tm=512):
    t, k = xb.shape
    tm = min(tm, t)
    return pl.pallas_call(
        _gdn_gates_kernel,
        grid=(t // tm,),
        in_specs=[pl.BlockSpec((tm, k), lambda i: (i, 0)),
                  pl.BlockSpec((k, LANES), lambda i: (0, 0)),
                  pl.BlockSpec((SUBLANES, LANES), lambda i: (0, 0)),
                  pl.BlockSpec((tm, tm), lambda i: (0, 0))],
        out_specs=[pl.BlockSpec((tm, LANES), lambda i: (i, 0)),
                   pl.BlockSpec((2 * A_HEADS, tm), lambda i: (0, i))],
        out_shape=[jax.ShapeDtypeStruct((t, LANES), F32),
                   jax.ShapeDtypeStruct((2 * A_HEADS, t), F32)],
        compiler_params=_params("parallel"),
        name="gdn_gates",
    )(xb, w_ab_pad, par, tri)


def _gdn_chunk_kernel(q_ref, k_ref, v_ref, z_ref, fac_ref, fact_ref, nw_ref, o_ref, s_ref):
    c = A_CHUNK

    @pl.when(pl.program_id(0) == 0)
    def _():
        s_ref[...] = jnp.zeros_like(s_ref)

    fac = fac_ref[...]
    fact = fact_ref[0]
    nw = nw_ref[...]
    ii = lax.broadcasted_iota(jnp.int32, (c, c), 0)
    jj = lax.broadcasted_iota(jnp.int32, (c, c), 1)
    incl = ii >= jj
    strict = ii > jj

    def mm(a, b):
        return jnp.dot(a.astype(BF16), b.astype(BF16), preferred_element_type=F32)

    for h in range(A_HEADS):
        sl = slice(h * A_DK, (h + 1) * A_DK)
        q = q_ref[:, sl]
        k = k_ref[:, sl]
        v = v_ref[:, sl].astype(F32)
        z = z_ref[:, sl].astype(F32)
        kf = k.astype(F32)
        gc_col = fac[:, h:h + 1]
        beta = fac[:, A_HEADS + h:A_HEADS + h + 1]
        gc_row = fact[h:h + 1, :]
        g_last = gc_col[c - 1:c, :]
        eg = jnp.exp(gc_col)
        decay = jnp.exp(jnp.where(incl, gc_col - gc_row, MASK_VALUE))

        kk = lax.dot_general(k, k, NT_DIMS, preferred_element_type=F32)
        mmat = jnp.where(strict, beta * kk * decay, 0.0)
        x = -mmat
        p = mm(mmat, mmat)
        for level in range(5):
            x = x + p + mm(x, p)
            if level < 4:
                p = mm(p, p)
        rhs = jnp.concatenate([beta * v, (beta * eg) * kf], axis=1)
        uw = rhs + mm(x, rhs)
        u = uw[:, :A_DV]
        w = uw[:, A_DV:]

        s = s_ref[h]
        sb = s.astype(BF16)
        v_new = u - jnp.dot(w.astype(BF16), sb, preferred_element_type=F32)
        qk = lax.dot_general(q, k, NT_DIMS, preferred_element_type=F32) * decay
        o = jnp.dot((q.astype(F32) * eg).astype(BF16), sb, preferred_element_type=F32) + mm(qk, v_new)
        k_dec = kf * jnp.exp(g_last - gc_col)
        s_ref[h] = s * jnp.exp(g_last) + lax.dot_general(
            k_dec.astype(BF16), v_new.astype(BF16), TN_DIMS, preferred_element_type=F32)

        ms = jnp.mean(o * o, axis=-1, keepdims=True)
        o_ref[:, sl] = (o * lax.rsqrt(ms + RMS_EPS) * nw * _silu(z)).astype(o_ref.dtype)


def gdn_chunk(proj, fac, fact3, norm_w):
    t = proj.shape[0]
    c = A_CHUNK
    return pl.pallas_call(
        _gdn_chunk_kernel,
        grid=(t // c,),
        in_specs=[pl.BlockSpec((c, A_QK), lambda i: (i, 0)),
                  pl.BlockSpec((c, A_QK), lambda i: (i, 1)),
                  pl.BlockSpec((c, A_V), lambda i: (i, 2)),
                  pl.BlockSpec((c, A_V), lambda i: (i, 3)),
                  pl.BlockSpec((c, LANES), lambda i: (i, 0)),
                  pl.BlockSpec((1, 2 * A_HEADS, c), lambda i: (i, 0, 0)),
                  pl.BlockSpec((1, A_DV), lambda i: (0, 0))],
        out_specs=pl.BlockSpec((c, A_V), lambda i: (i, 0)),
        out_shape=jax.ShapeDtypeStruct((t, A_V), BF16),
        scratch_shapes=[pltpu.VMEM((A_HEADS, A_DK, A_DV), F32)],
        compiler_params=_params("arbitrary"),
        name="gdn_chunk",
    )(proj, proj, proj, proj, fac, fact3, norm_w.reshape(1, A_DV))


def _attn_kernel(qi_ref, kj_ref, q_ref, k_ref, v_ref, bias_ref, lam_ref, sw_ref, o_ref,
                 m_ref, l_ref, acc_ref, *, lambda_init):
    step = pl.program_id(1)
    qi = qi_ref[step]
    kj = kj_ref[step]

    @pl.when(kj == 0)
    def _():
        m_ref[...] = jnp.full_like(m_ref, MASK_VALUE)
        l_ref[...] = jnp.zeros_like(l_ref)
        acc_ref[...] = jnp.zeros_like(acc_ref)

    def process(bias):
        q = q_ref[...]
        k = k_ref[...]
        v = v_ref[...]
        lane = lax.broadcasted_iota(jnp.int32, q.shape, 1)
        for i in range(2):
            in_map = (lane >= i * B_DH) & (lane < (i + 1) * B_DH)
            qm = jnp.where(in_map, q, jnp.zeros_like(q))
            s = lax.dot_general(qm, k, NT_DIMS, preferred_element_type=F32)
            if bias is not None:
                s = s + bias
            m_prev = m_ref[i]
            m_new = jnp.maximum(m_prev, jnp.max(s, axis=-1, keepdims=True))
            alpha = jnp.exp(m_prev - m_new)
            p = jnp.exp(s - m_new)
            l_ref[i] = alpha * l_ref[i] + jnp.sum(p, axis=-1, keepdims=True)
            acc_ref[i] = alpha * acc_ref[i] + jnp.dot(p.astype(BF16), v, preferred_element_type=F32)
            m_ref[i] = m_new

    @pl.when(kj >= qi - 1)
    def _():
        process(bias_ref[0, 0])

    @pl.when(kj < qi - 1)
    def _():
        process(None)

    @pl.when(kj == qi)
    def _():
        lam = lam_ref[...]
        lam_val = (jnp.exp(jnp.sum(lam[0:1] * lam[1:2], axis=-1, keepdims=True))
                   - jnp.exp(jnp.sum(lam[2:3] * lam[3:4], axis=-1, keepdims=True)) + lambda_init)
        o = acc_ref[0] / l_ref[0] - lam_val * (acc_ref[1] / l_ref[1])
        ms = jnp.mean(o * o, axis=-1, keepdims=True)
        o = o * lax.rsqrt(ms + RMS_EPS) * sw_ref[...] * (1.0 - lambda_init)
        o_ref[...] = o.astype(o_ref.dtype)


def diff_attention(qb, kvb, bias_tiles, lam, subln_w, lambda_init, *, tile=512):
    t = qb.shape[0]
    tile = min(tile, t)
    nb = t // tile
    qi_list, kj_list = [], []
    for i in range(nb):
        for j in range(i + 1):
            qi_list.append(i)
            kj_list.append(j)
    qi_arr = jnp.asarray(np.array(qi_list, np.int32))
    kj_arr = jnp.asarray(np.array(kj_list, np.int32))
    dh2 = 2 * B_DH
    grid_spec = pltpu.PrefetchScalarGridSpec(
        num_scalar_prefetch=2,
        grid=(B_HEADS, len(qi_list)),
        in_specs=[
            pl.BlockSpec((tile, dh2), lambda h, s, qi, kj: (qi[s], h)),
            pl.BlockSpec((tile, dh2), lambda h, s, qi, kj: (kj[s], h)),
            pl.BlockSpec((tile, dh2), lambda h, s, qi, kj: (kj[s], B_HEADS + h)),
            pl.BlockSpec((1, 1, tile, tile),
                         lambda h, s, qi, kj: (h, jnp.where(kj[s] == qi[s], 0, 1), 0, 0)),
            pl.BlockSpec((4, B_DH), lambda h, s, qi, kj: (0, 0)),
            pl.BlockSpec((1, dh2), lambda h, s, qi, kj: (0, 0)),
        ],
        out_specs=pl.BlockSpec((tile, dh2), lambda h, s, qi, kj: (qi[s], h)),
        scratch_shapes=[pltpu.VMEM((2, tile, 1), F32),
                        pltpu.VMEM((2, tile, 1), F32),
                        pltpu.VMEM((2, tile, dh2), F32)],
    )
    return pl.pallas_call(
        functools.partial(_attn_kernel, lambda_init=lambda_init),
        grid_spec=grid_spec,
        out_shape=jax.ShapeDtypeStruct((t, B_HEADS * dh2), BF16),
        compiler_params=_params("parallel", "arbitrary"),
        name="diff_attention",
    )(qi_arr, kj_arr, qb, kvb, kvb, bias_tiles, lam, subln_w.reshape(1, dh2))


def _t5_bucket(n):
    max_exact = REL_BUCKETS // 2
    nf = jnp.maximum(n, 1).astype(F32)
    large = max_exact + (jnp.log(nf / max_exact) / math.log(REL_MAX_DIST / max_exact)
                         * (REL_BUCKETS - max_exact)).astype(jnp.int32)
    large = jnp.minimum(large, REL_BUCKETS - 1)
    return jnp.where(n < max_exact, n, large)


def attention_bias_tiles(rel_bias, tile):
    assert tile >= REL_MAX_DIST
    n = jnp.arange(2 * tile)
    far = rel_bias[REL_BUCKETS - 1]
    db = (rel_bias[_t5_bucket(n)] - far[None, :]).T.astype(F32)
    rel = np.arange(tile)[:, None] - np.arange(tile)[None, :]
    diag = jnp.where(jnp.asarray(rel >= 0)[None], db[:, np.clip(rel, 0, None)], MASK_VALUE)
    near = db[:, rel + tile]
    return jnp.stack([diag, near], axis=1)


def _router_gates_t(x, rwt, rb):
    logits = lax.dot_general(rwt, x, NT_DIMS, preferred_element_type=F32, precision=HIGHEST) + rb
    mx = jnp.max(logits, axis=0, keepdims=True)
    ex = jnp.exp(logits - mx)
    probs = ex / jnp.sum(ex, axis=0, keepdims=True)
    row = [probs[e:e + 1, :] for e in range(N_EXPERTS)]
    epg = EXPERTS_PER_GROUP
    best_sum = None
    best_group = None
    for g in range(N_GROUPS):
        vals = row[g * epg:(g + 1) * epg]
        top2 = None
        for a in range(epg):
            for b in range(a + 1, epg):
                pair = vals[a] + vals[b]
                top2 = pair if top2 is None else jnp.maximum(top2, pair)
        if g == 0:
            best_sum = top2
            best_group = jnp.zeros_like(top2, dtype=jnp.int32)
        else:
            upd = top2 > best_sum
            best_sum = jnp.where(upd, top2, best_sum)
            best_group = jnp.where(upd, g, best_group)
    gp = []
    for j in range(epg):
        sel = row[j]
        for g in range(1, N_GROUPS):
            sel = jnp.where(best_group == g, row[g * epg + j], sel)
        gp.append(sel)
    v1 = gp[0]
    i1 = jnp.zeros_like(best_group)
    for j in range(1, epg):
        upd = gp[j] > v1
        v1 = jnp.where(upd, gp[j], v1)
        i1 = jnp.where(upd, j, i1)
    v2 = jnp.full_like(v1, -1.0)
    i2 = jnp.zeros_like(best_group)
    for j in range(epg):
        upd = (i1 != j) & (gp[j] > v2)
        v2 = jnp.where(upd, gp[j], v2)
        i2 = jnp.where(upd, j, i2)
    denom = v1 + v2
    e1 = best_group * epg + i1
    e2 = best_group * epg + i2
    eidx = lax.broadcasted_iota(jnp.int32, logits.shape, 0)
    return jnp.where(eidx == e1, v1 / denom, 0.0) + jnp.where(eidx == e2, v2 / denom, 0.0)


def _moe_dense_kernel(x_ref, xb_ref, rwt_ref, rb_ref, wg_ref, wu_ref, wd_ref, lw_ref, lb_ref,
                      o_ref, ob_ref, gate_ref, acc_ref):
    e = pl.program_id(1)
    tm = x_ref.shape[0]

    @pl.when(e == 0)
    def _():
        gates_t = _router_gates_t(x_ref[...], rwt_ref[...], rb_ref[...])
        padded = jnp.concatenate([gates_t, jnp.zeros((LANES - N_EXPERTS, tm), F32)], axis=0)
        gate_ref[...] = padded.T
        acc_ref[...] = jnp.zeros_like(acc_ref)

    xb = xb_ref[...]
    lane = lax.broadcasted_iota(jnp.int32, gate_ref.shape, 1)
    g_e = jnp.sum(jnp.where(lane == e, gate_ref[...], 0.0), axis=-1, keepdims=True)
    hg = jnp.dot(xb, wg_ref[0], preferred_element_type=F32)
    hu = jnp.dot(xb, wu_ref[0], preferred_element_type=F32)
    hid = (_silu(hg) * hu).astype(BF16)
    acc_ref[...] += g_e * jnp.dot(hid, wd_ref[0], preferred_element_type=F32)

    @pl.when(e == N_EXPERTS - 1)
    def _():
        y = _layer_norm(DN_ALPHA * x_ref[...] + acc_ref[...], lw_ref[...], lb_ref[...])
        o_ref[...] = y
        ob_ref[...] = y.astype(BF16)


def moe_residual_ln(x, xb, rwt, rb, wg, wu, wd, lw, lb, *, tm=512):
    t, d = x.shape
    tm = min(tm, t)
    return pl.pallas_call(
        _moe_dense_kernel,
        grid=(t // tm, N_EXPERTS),
        in_specs=[pl.BlockSpec((tm, d), lambda i, e: (i, 0)),
                  pl.BlockSpec((tm, d), lambda i, e: (i, 0)),
                  pl.BlockSpec((N_EXPERTS, d), lambda i, e: (0, 0)),
                  pl.BlockSpec((N_EXPERTS, 1), lambda i, e: (0, 0)),
                  pl.BlockSpec((1, d, D_EXPERT), lambda i, e: (e, 0, 0)),
                  pl.BlockSpec((1, d, D_EXPERT), lambda i, e: (e, 0, 0)),
                  pl.BlockSpec((1, D_EXPERT, d), lambda i, e: (e, 0, 0)),
                  pl.BlockSpec((1, d), lambda i, e: (0, 0)),
                  pl.BlockSpec((1, d), lambda i, e: (0, 0))],
        out_specs=[pl.BlockSpec((tm, d), lambda i, e: (i, 0)),
                   pl.BlockSpec((tm, d), lambda i, e: (i, 0))],
        out_shape=[jax.ShapeDtypeStruct((t, d), F32),
                   jax.ShapeDtypeStruct((t, d), BF16)],
        scratch_shapes=[pltpu.VMEM((tm, LANES), F32),
                        pltpu.VMEM((tm, d), F32)],
        compiler_params=_params("parallel", "arbitrary"),
        name="moe_residual_ln",
    )(x, xb, rwt, rb, wg, wu, wd, lw.reshape(1, d), lb.reshape(1, d))


def _lambda_init_for(layer):
    return 0.8 - 0.6 * math.exp(-0.3 * layer)


def kernel(x, a_w_in, a_conv, a_A_log, a_dt_bias, a_norm_w, a_w_out, b_w_q, b_w_kv, b_lambda, b_subln_w, b_w_out, rel_bias, router_w, router_b, moe_w_gate, moe_w_up, moe_w_down, ln_w, ln_b):
    bsz, t, d = x.shape
    assert bsz == 1 and d == D_MODEL and t % A_CHUNK == 0
    xf = x.reshape(t, d).astype(F32)
    xb = xf.astype(BF16)

    gate_tile = min(512, t)
    tri = jnp.asarray(np.kron(np.eye(gate_tile // A_CHUNK, dtype=np.float32),
                              np.tril(np.ones((A_CHUNK, A_CHUNK), np.float32))))
    rwt = router_w.T.astype(F32)
    rb = router_b.reshape(N_EXPERTS, 1).astype(F32)
    attn_tile = min(512, t)
    bias_tiles = attention_bias_tiles(rel_bias.astype(F32), attn_tile)

    kvb = None
    for l in range(DEPTH):
        if l < N_A_LAYERS:
            w_in = a_w_in[l]
            w_main = w_in[:, :A_CONV_CH + A_V].astype(BF16)
            w_ab = jnp.pad(w_in[:, A_CONV_CH + A_V:], ((0, 0), (0, LANES - 2 * A_HEADS))).astype(BF16)
            conv_pad = jnp.pad(a_conv[l].astype(F32), ((0, 0), (0, A_V)))
            par = jnp.zeros((SUBLANES, LANES), F32)
            par = par.at[0, :A_HEADS].set(a_A_log[l].astype(F32))
            par = par.at[1, :A_HEADS].set(a_dt_bias[l].astype(F32))
            proj = gdn_inproj(xb, w_main, conv_pad)
            fac, fact = gdn_gates(xb, w_ab, par, tri)
            fact3 = fact.reshape(2 * A_HEADS, t // A_CHUNK, A_CHUNK).transpose(1, 0, 2)
            mixed = gdn_chunk(proj, fac, fact3, a_norm_w[l].astype(F32))
            w_out = a_w_out[l].astype(BF16)
        else:
            j = l - N_A_LAYERS
            if kvb is None:
                kvb = matmul_bf16(xb, b_w_kv.astype(BF16))
            qb = matmul_bf16(xb, b_w_q[j].astype(BF16), scale=B_DH ** -0.5)
            mixed = diff_attention(qb, kvb, bias_tiles, b_lambda[j].astype(F32),
                                   b_subln_w[j].astype(F32), _lambda_init_for(l), tile=attn_tile)
            w_out = b_w_out[j].astype(BF16)
        xf, xb = matmul_residual_ln(mixed, w_out, xf, ln_w[l, 0].astype(F32), ln_b[l, 0].astype(F32))
        xf, xb = moe_residual_ln(xf, xb, rwt, rb, moe_w_gate[l].astype(BF16), moe_w_up[l].astype(BF16),
                                 moe_w_down[l].astype(BF16), ln_w[l, 1].astype(F32), ln_b[l, 1].astype(F32))
    return xf.reshape(bsz, t, d).astype(x.dtype)
```

```python
import functools
import math

import jax
import jax.numpy as jnp
import numpy as np
from jax import lax
from jax.experimental import pallas as pl
from jax.experimental.pallas import tpu as pltpu

F32 = jnp.float32
BF16 = jnp.bfloat16

D_MODEL = 1024
DEPTH = 4
N_A_LAYERS = DEPTH // 2
A_HEADS = 8
A_DK = 128
A_DV = 128
A_QK = A_HEADS * A_DK
A_V = A_HEADS * A_DV
A_CONV = 4
A_CONV_CH = 2 * A_QK + A_V
A_CHUNK = 64
B_HEADS = 8
B_DH = 64
B_QK = B_HEADS * 2 * B_DH
REL_BUCKETS = 32
REL_MAX_DIST = 128
N_EXPERTS = 16
N_GROUPS = 4
EXPERTS_PER_GROUP = N_EXPERTS // N_GROUPS
D_EXPERT = 512
DN_ALPHA = (2 * DEPTH) ** 0.25
LN_EPS = 1e-5
RMS_EPS = 1e-6

LANES = 128
SUBLANES = 8
VMEM_LIMIT = 48 * 1024 * 1024
MASK_VALUE = -1e30

HIGHEST = lax.Precision.HIGHEST
NT_DIMS = (((1,), (1,)), ((), ()))
TN_DIMS = (((0,), (0,)), ((), ()))


def _params(*sem):
    return pltpu.CompilerParams(dimension_semantics=sem, vmem_limit_bytes=VMEM_LIMIT)


def _sigmoid(x):
    return 1.0 / (1.0 + jnp.exp(-x))


def _silu(x):
    return x * _sigmoid(x)


def _layer_norm(v, w, b):
    mu = jnp.mean(v, axis=-1, keepdims=True)
    c = v - mu
    var = jnp.mean(c * c, axis=-1, keepdims=True)
    return c * lax.rsqrt(var + LN_EPS) * w + b


def _mm_kernel(a_ref, w_ref, o_ref, *, scale):
    y = jnp.dot(a_ref[...], w_ref[...], preferred_element_type=F32)
    if scale != 1.0:
        y = y * scale
    o_ref[...] = y.astype(o_ref.dtype)


def matmul_bf16(a, w, *, scale=1.0, tm=512, tn=1024):
    m, k = a.shape
    n = w.shape[1]
    tm = min(tm, m)
    return pl.pallas_call(
        functools.partial(_mm_kernel, scale=scale),
        grid=(n // tn, m // tm),
        in_specs=[pl.BlockSpec((tm, k), lambda j, i: (i, 0)),
                  pl.BlockSpec((k, tn), lambda j, i: (0, j))],
        out_specs=pl.BlockSpec((tm, tn), lambda j, i: (i, j)),
        out_shape=jax.ShapeDtypeStruct((m, n), BF16),
        compiler_params=_params("parallel", "arbitrary"),
        name="matmul_bf16",
    )(a, w)


def _mm_ln_kernel(a_ref, w_ref, x_ref, lw_ref, lb_ref, o_ref, ob_ref):
    h = jnp.dot(a_ref[...], w_ref[...], preferred_element_type=F32)
    y = _layer_norm(DN_ALPHA * x_ref[...] + h, lw_ref[...], lb_ref[...])
    o_ref[...] = y
    ob_ref[...] = y.astype(BF16)


def matmul_residual_ln(a, w, x, lw, lb, *, tm=512):
    m, k = a.shape
    n = w.shape[1]
    tm = min(tm, m)
    return pl.pallas_call(
        _mm_ln_kernel,
        grid=(m // tm,),
        in_specs=[pl.BlockSpec((tm, k), lambda i: (i, 0)),
                  pl.BlockSpec((k, n), lambda i: (0, 0)),
                  pl.BlockSpec((tm, n), lambda i: (i, 0)),
                  pl.BlockSpec((1, n), lambda i: (0, 0)),
                  pl.BlockSpec((1, n), lambda i: (0, 0))],
        out_specs=[pl.BlockSpec((tm, n), lambda i: (i, 0)),
                   pl.BlockSpec((tm, n), lambda i: (i, 0))],
        out_shape=[jax.ShapeDtypeStruct((m, n), F32),
                   jax.ShapeDtypeStruct((m, n), BF16)],
        compiler_params=_params("parallel"),
        name="matmul_residual_ln",
    )(a, w, x, lw.reshape(1, n), lb.reshape(1, n))


def _gdn_inproj_kernel(x_ref, w_ref, cw_ref, o_ref, halo_ref):
    n = pl.program_id(0)
    m = pl.program_id(1)
    tm, tn = o_ref.shape
    y = jnp.dot(x_ref[...], w_ref[...], preferred_element_type=F32)

    @pl.when(n == 3)
    def _():
        o_ref[...] = y.astype(o_ref.dtype)

    @pl.when(n < 3)
    def _():
        @pl.when(m == 0)
        def _():
            halo_ref[...] = jnp.zeros_like(halo_ref)

        halo = halo_ref[...]
        cw = cw_ref[...]
        rows = lax.broadcasted_iota(jnp.int32, (SUBLANES, tn), 0)
        acc = y * cw[A_CONV - 1:A_CONV]
        for shift in range(1, A_CONV):
            ys = pltpu.roll(y, shift, 0)
            top = jnp.where(rows < shift, pltpu.roll(halo, shift, 0), ys[:SUBLANES])
            ys = jnp.concatenate([top, ys[SUBLANES:]], axis=0)
            acc = acc + ys * cw[A_CONV - 1 - shift:A_CONV - shift]
        halo_ref[...] = y[tm - SUBLANES:]
        a = _silu(acc)

        @pl.when(n == 2)
        def _():
            o_ref[...] = a.astype(o_ref.dtype)

        @pl.when(n < 2)
        def _():
            post = jnp.where(n == 0, A_DK ** -0.5, 1.0).astype(F32)
            for h in range(tn // A_DK):
                seg = a[:, h * A_DK:(h + 1) * A_DK]
                ss = jnp.sum(seg * seg, axis=-1, keepdims=True)
                o_ref[:, h * A_DK:(h + 1) * A_DK] = (seg * (lax.rsqrt(ss + RMS_EPS) * post)).astype(o_ref.dtype)


def gdn_inproj(xb, w_main, conv_w_pad, *, tm=512):
    t, k = xb.shape
    n = w_main.shape[1]
    tn = A_QK
    tm = min(tm, t)
    return pl.pallas_call(
        _gdn_inproj_kernel,
        grid=(n // tn, t // tm),
        in_specs=[pl.BlockSpec((tm, k), lambda j, i: (i, 0)),
                  pl.BlockSpec((k, tn), lambda j, i: (0, j)),
                  pl.BlockSpec((A_CONV, tn), lambda j, i: (0, j))],
        out_specs=pl.BlockSpec((tm, tn), lambda j, i: (i, j)),
        out_shape=jax.ShapeDtypeStruct((t, n), BF16),
        scratch_shapes=[pltpu.VMEM((SUBLANES, tn), F32)],
        compiler_params=_params("arbitrary", "arbitrary"),
        name="gdn_inproj",
    )(xb, w_main, conv_w_pad)


def _gdn_gates_kernel(x_ref, w_ref, par_ref, tri_ref, fac_ref, fact_ref):
    ab = jnp.dot(x_ref[...], w_ref[...], preferred_element_type=F32)
    lane = lax.broadcasted_iota(jnp.int32, ab.shape, 1)
    neg_a = -jnp.exp(par_ref[0:1, :])
    z = ab + par_ref[1:2, :]
    softplus = jnp.maximum(z, 0.0) + jnp.log1p(jnp.exp(-jnp.abs(z)))
    g = jnp.where(lane < A_HEADS, neg_a * softplus, 0.0)
    beta = _sigmoid(ab)
    gc = jnp.dot(tri_ref[...], g, preferred_element_type=F32, precision=HIGHEST)
    fac = jnp.where(lane < A_HEADS, gc, jnp.where(lane < 2 * A_HEADS, beta, 0.0))
    fac_ref[...] = fac
    fact_ref[...] = fac.T[:2 * A_HEADS, :]


def gdn_gates(xb, w_ab_pad, par, tri, *, tm=512):
    t, k = xb.shape
    tm = min(tm, t)
    return pl.pallas_call(
        _gdn_gates_kernel,
        grid=(t // tm,),
        in_specs=[pl.BlockSpec((tm, k), lambda i: (i, 0)),
                  pl.BlockSpec((k, LANES), lambda i: (0, 0)),
                  pl.BlockSpec((SUBLANES, LANES), lambda i: (0, 0)),
                  pl.BlockSpec((tm, tm), lambda i: (0, 0))],
        out_specs=[pl.BlockSpec((tm, LANES), lambda i: (i, 0)),
                   pl.BlockSpec((2 * A_HEADS, tm), lambda i: (0, i))],
        out_shape=[jax.ShapeDtypeStruct((t, LANES), F32),
                   jax.ShapeDtypeStruct((2 * A_HEADS, t), F32)],
        compiler_params=_params("parallel"),
        name="gdn_gates",
    )(xb, w_ab_pad, par, tri)


def _gdn_chunk_kernel(q_ref, k_ref, v_ref, z_ref, fac_ref, fact_ref, nw_ref, o_ref, s_ref):
    c = A_CHUNK

    @pl.when(pl.program_id(0) == 0)
    def _():
        s_ref[...] = jnp.zeros_like(s_ref)

    fac = fac_ref[...]
    fact = fact_ref[0]
    nw = nw_ref[...]
    ii = lax.broadcasted_iota(jnp.int32, (c, c), 0)
    jj = lax.broadcasted_iota(jnp.int32, (c, c), 1)
    incl = ii >= jj
    strict = ii > jj

    def mm(a, b):
        return jnp.dot(a.astype(BF16), b.astype(BF16), preferred_element_type=F32)

    for h in range(A_HEADS):
        sl = slice(h * A_DK, (h + 1) * A_DK)
        q = q_ref[:, sl]
        k = k_ref[:, sl]
        v = v_ref[:, sl].astype(F32)
        z = z_ref[:, sl].astype(F32)
        kf = k.astype(F32)
        gc_col = fac[:, h:h + 1]
        beta = fac[:, A_HEADS + h:A_HEADS + h + 1]
        gc_row = fact[h:h + 1, :]
        g_last = gc_col[c - 1:c, :]
        eg = jnp.exp(gc_col)
        decay = jnp.exp(jnp.where(incl, gc_col - gc_row, MASK_VALUE))

        kk = lax.dot_general(k, k, NT_DIMS, preferred_element_type=F32)
        mmat = jnp.where(strict, beta * kk * decay, 0.0)
        x = -mmat
        p = mm(mmat, mmat)
        for level in range(5):
            x = x + p + mm(x, p)
            if level < 4:
                p = mm(p, p)
        rhs = jnp.concatenate([beta * v, (beta * eg) * kf], axis=1)
        uw = rhs + mm(x, rhs)
        u = uw[:, :A_DV]
        w = uw[:, A_DV:]

        s = s_ref[h]
        sb = s.astype(BF16)
        v_new = u - jnp.dot(w.astype(BF16), sb, preferred_element_type=F32)
        qk = lax.dot_general(q, k, NT_DIMS, preferred_element_type=F32) * decay
        o = jnp.dot((q.astype(F32) * eg).astype(BF16), sb, preferred_element_type=F32) + mm(qk, v_new)
        k_dec = kf * jnp.exp(g_last - gc_col)
        s_ref[h] = s * jnp.exp(g_last) + lax.dot_general(
            k_dec.astype(BF16), v_new.astype(BF16), TN_DIMS, preferred_element_type=F32)

        ms = jnp.mean(o * o, axis=-1, keepdims=True)
        o_ref[:, sl] = (o * lax.rsqrt(ms + RMS_EPS) * nw * _silu(z)).astype(o_ref.dtype)


def gdn_chunk(proj, fac, fact3, norm_w):
    t = proj.shape[0]
    c = A_CHUNK
    return pl.pallas_call(
        _gdn_chunk_kernel,
        grid=(t // c,),
        in_specs=[pl.BlockSpec((c, A_QK), lambda i: (i, 0)),
                  pl.BlockSpec((c, A_QK), lambda i: (i, 1)),
                  pl.BlockSpec((c, A_V), lambda i: (i, 2)),
                  pl.BlockSpec((c, A_V), lambda i: (i, 3)),
                  pl.BlockSpec((c, LANES), lambda i: (i, 0)),
                  pl.BlockSpec((1, 2 * A_HEADS, c), lambda i: (i, 0, 0)),
                  pl.BlockSpec((1, A_DV), lambda i: (0, 0))],
        out_specs=pl.BlockSpec((c, A_V), lambda i: (i, 0)),
        out_shape=jax.ShapeDtypeStruct((t, A_V), BF16),
        scratch_shapes=[pltpu.VMEM((A_HEADS, A_DK, A_DV), F32)],
        compiler_params=_params("arbitrary"),
        name="gdn_chunk",
    )(proj, proj, proj, proj, fac, fact3, norm_w.reshape(1, A_DV))


LOG2E = math.log2(math.e)
VT_ROWS = 2 * B_DH + 16


def _vt_proj_kernel(x_ref, w_ref, o_ref):
    tk = x_ref.shape[0]
    y = lax.dot_general(w_ref[...], x_ref[...], NT_DIMS, preferred_element_type=F32)
    ones = jnp.ones((VT_ROWS - 2 * B_DH, tk), o_ref.dtype)
    for h in range(B_HEADS):
        o_ref[0, h, :2 * B_DH, :] = y[h * 2 * B_DH:(h + 1) * 2 * B_DH].astype(o_ref.dtype)
        o_ref[0, h, 2 * B_DH:, :] = ones


def v_transposed_proj(xb, w_vt, tile):
    t, k = xb.shape
    n = w_vt.shape[0]
    return pl.pallas_call(
        _vt_proj_kernel,
        grid=(t // tile,),
        in_specs=[pl.BlockSpec((tile, k), lambda i: (i, 0)),
                  pl.BlockSpec((n, k), lambda i: (0, 0))],
        out_specs=pl.BlockSpec((1, B_HEADS, VT_ROWS, tile), lambda i: (i, 0, 0, 0)),
        out_shape=jax.ShapeDtypeStruct((t // tile, B_HEADS, VT_ROWS, tile), BF16),
        compiler_params=_params("parallel"),
        name="v_transposed_proj",
    )(xb, w_vt)


def _attn_kernel(q_ref, k_ref, vt_ref, bias_ref, lam_ref, swb_ref, o_ref, m_ref, acc_ref,
                 *, lambda_init, tile):
    qi = pl.program_id(1)
    dv = 2 * B_DH
    q = q_ref[...]
    first_map = lax.broadcasted_iota(jnp.int32, (tile, dv), 1) < B_DH
    m_ref[...] = jnp.full_like(m_ref, MASK_VALUE)
    acc_ref[...] = jnp.zeros_like(acc_ref)

    def process(j, bias):
        k = k_ref[pl.ds(pl.multiple_of(j * tile, tile), tile), :]
        zero = jnp.zeros_like(k)
        k2 = jnp.concatenate([jnp.where(first_map, k, zero), jnp.where(first_map, zero, k)], axis=0)
        st = lax.dot_general(k2, q, NT_DIMS, preferred_element_type=F32)
        vt = vt_ref[j, 0]
        for i in range(2):
            s = st[i * tile:(i + 1) * tile]
            if bias is not None:
                s = s + bias
            m_prev = m_ref[i]
            m_new = jnp.maximum(m_prev, jnp.max(s, axis=0, keepdims=True))
            alpha = jnp.exp2(m_prev - m_new)
            p = jnp.exp2(s - m_new).astype(BF16)
            acc_ref[i] = alpha * acc_ref[i] + jnp.dot(vt, p, preferred_element_type=F32)
            m_ref[i] = m_new

    def far_body(j, carry):
        process(j, None)
        return carry

    lax.fori_loop(0, jnp.maximum(qi - 1, 0), far_body, 0)

    @pl.when(qi >= 1)
    def _():
        process(qi - 1, bias_ref[0, 1])

    process(qi, bias_ref[0, 0])

    lam = lam_ref[...]
    lam_val = (jnp.exp(jnp.sum(lam[0:1] * lam[1:2], axis=-1, keepdims=True))
               - jnp.exp(jnp.sum(lam[2:3] * lam[3:4], axis=-1, keepdims=True)) + lambda_init)
    a0 = acc_ref[0]
    a1 = acc_ref[1]
    ot = a0[:dv] / a0[dv:dv + 1] - lam_val * (a1[:dv] / a1[dv:dv + 1])
    ms = jnp.mean(ot * ot, axis=0, keepdims=True)
    ot = ot * lax.rsqrt(ms + RMS_EPS) * swb_ref[...] * (1.0 - lambda_init)
    o_ref[...] = ot.T.astype(o_ref.dtype)


def diff_attention(qb, kb, vt, bias_tiles, lam, subln_w, lambda_init, *, tile):
    t = qb.shape[0]
    nb = t // tile
    dv = 2 * B_DH
    swb = jnp.broadcast_to(subln_w.reshape(dv, 1), (dv, tile))
    return pl.pallas_call(
        functools.partial(_attn_kernel, lambda_init=lambda_init, tile=tile),
        grid=(B_HEADS, nb),
        in_specs=[pl.BlockSpec((tile, dv), lambda h, i: (i, h)),
                  pl.BlockSpec((t, dv), lambda h, i: (0, h)),
                  pl.BlockSpec((nb, 1, VT_ROWS, tile), lambda h, i: (0, h, 0, 0)),
                  pl.BlockSpec((1, 2, tile, tile), lambda h, i: (h, 0, 0, 0)),
                  pl.BlockSpec((4, B_DH), lambda h, i: (0, 0)),
                  pl.BlockSpec((dv, tile), lambda h, i: (0, 0))],
        out_specs=pl.BlockSpec((tile, dv), lambda h, i: (i, h)),
        out_shape=jax.ShapeDtypeStruct((t, B_HEADS * dv), BF16),
        scratch_shapes=[pltpu.VMEM((2, 1, tile), F32),
                        pltpu.VMEM((2, VT_ROWS, tile), F32)],
        compiler_params=_params("parallel", "arbitrary"),
        name="diff_attention",
    )(qb, kb, vt, bias_tiles, lam, swb)


def _t5_bucket(n):
    max_exact = REL_BUCKETS // 2
    nf = jnp.maximum(n, 1).astype(F32)
    large = max_exact + (jnp.log(nf / max_exact) / math.log(REL_MAX_DIST / max_exact)
                         * (REL_BUCKETS - max_exact)).astype(jnp.int32)
    large = jnp.minimum(large, REL_BUCKETS - 1)
    return jnp.where(n < max_exact, n, large)


def attention_bias_tiles(rel_bias, tile):
    assert tile >= REL_MAX_DIST
    n = jnp.arange(2 * tile)
    far = rel_bias[REL_BUCKETS - 1]
    db = ((rel_bias[_t5_bucket(n)] - far[None, :]) * LOG2E).T.astype(F32)
    nh = db.shape[0]
    full = jnp.concatenate([jnp.full((nh, tile - 1), MASK_VALUE, F32), db], axis=1)

    def toeplitz(vec):
        w = jnp.concatenate([vec, jnp.zeros((nh, 1), F32)], axis=1)
        flat = jnp.tile(w, (1, tile))[:, :tile * (2 * tile - 1)]
        return flat.reshape(nh, tile, 2 * tile - 1)[:, :, tile - 1:]

    diag = toeplitz(full[:, :2 * tile - 1])
    near = toeplitz(full[:, tile:3 * tile - 1])
    return jnp.stack([diag, near], axis=1)


def _router_gates_t(x, rwt, rb):
    logits = lax.dot_general(rwt, x, NT_DIMS, preferred_element_type=F32, precision=HIGHEST) + rb
    mx = jnp.max(logits, axis=0, keepdims=True)
    ex = jnp.exp(logits - mx)
    probs = ex / jnp.sum(ex, axis=0, keepdims=True)
    row = [probs[e:e + 1, :] for e in range(N_EXPERTS)]
    epg = EXPERTS_PER_GROUP
    best_sum = None
    best_group = None
    for g in range(N_GROUPS):
        vals = row[g * epg:(g + 1) * epg]
        top2 = None
        for a in range(epg):
            for b in range(a + 1, epg):
                pair = vals[a] + vals[b]
                top2 = pair if top2 is None else jnp.maximum(top2, pair)
        if g == 0:
            best_sum = top2
            best_group = jnp.zeros_like(top2, dtype=jnp.int32)
        else:
            upd = top2 > best_sum
            best_sum = jnp.where(upd, top2, best_sum)
            best_group = jnp.where(upd, g, best_group)
    gp = []
    for j in range(epg):
        sel = row[j]
        for g in range(1, N_GROUPS):
            sel = jnp.where(best_group == g, row[g * epg + j], sel)
        gp.append(sel)
    v1 = gp[0]
    i1 = jnp.zeros_like(best_group)
    for j in range(1, epg):
        upd = gp[j] > v1
        v1 = jnp.where(upd, gp[j], v1)
        i1 = jnp.where(upd, j, i1)
    v2 = jnp.full_like(v1, -1.0)
    i2 = jnp.zeros_like(best_group)
    for j in range(epg):
        upd = (i1 != j) & (gp[j] > v2)
        v2 = jnp.where(upd, gp[j], v2)
        i2 = jnp.where(upd, j, i2)
    denom = v1 + v2
    e1 = best_group * epg + i1
    e2 = best_group * epg + i2
    eidx = lax.broadcasted_iota(jnp.int32, logits.shape, 0)
    return jnp.where(eidx == e1, v1 / denom, 0.0) + jnp.where(eidx == e2, v2 / denom, 0.0)


def _moe_dense_kernel(x_ref, xb_ref, rwt_ref, rb_ref, wg_ref, wu_ref, wd_ref, lw_ref, lb_ref,
                      o_ref, ob_ref, gate_ref, acc_ref):
    e = pl.program_id(1)
    tm = x_ref.shape[0]

    @pl.when(e == 0)
    def _():
        gates_t = _router_gates_t(x_ref[...], rwt_ref[...], rb_ref[...])
        padded = jnp.concatenate([gates_t, jnp.zeros((LANES - N_EXPERTS, tm), F32)], axis=0)
        gate_ref[...] = padded.T
        acc_ref[...] = jnp.zeros_like(acc_ref)

    xb = xb_ref[...]
    lane = lax.broadcasted_iota(jnp.int32, gate_ref.shape, 1)
    g_e = jnp.sum(jnp.where(lane == e, gate_ref[...], 0.0), axis=-1, keepdims=True)
    hg = jnp.dot(xb, wg_ref[0], preferred_element_type=F32)
    hu = jnp.dot(xb, wu_ref[0], preferred_element_type=F32)
    hid = (_silu(hg) * hu).astype(BF16)
    acc_ref[...] += g_e * jnp.dot(hid, wd_ref[0], preferred_element_type=F32)

    @pl.when(e == N_EXPERTS - 1)
    def _():
        y = _layer_norm(DN_ALPHA * x_ref[...] + acc_ref[...], lw_ref[...], lb_ref[...])
        o_ref[...] = y
        ob_ref[...] = y.astype(BF16)


def moe_residual_ln(x, xb, rwt, rb, wg, wu, wd, lw, lb, *, tm=512):
    t, d = x.shape
    tm = min(tm, t)
    return pl.pallas_call(
        _moe_dense_kernel,
        grid=(t // tm, N_EXPERTS),
        in_specs=[pl.BlockSpec((tm, d), lambda i, e: (i, 0)),
                  pl.BlockSpec((tm, d), lambda i, e: (i, 0)),
                  pl.BlockSpec((N_EXPERTS, d), lambda i, e: (0, 0)),
                  pl.BlockSpec((N_EXPERTS, 1), lambda i, e: (0, 0)),
                  pl.BlockSpec((1, d, D_EXPERT), lambda i, e: (e, 0, 0)),
                  pl.BlockSpec((1, d, D_EXPERT), lambda i, e: (e, 0, 0)),
                  pl.BlockSpec((1, D_EXPERT, d), lambda i, e: (e, 0, 0)),
                  pl.BlockSpec((1, d), lambda i, e: (0, 0)),
                  pl.BlockSpec((1, d), lambda i, e: (0, 0))],
        out_specs=[pl.BlockSpec((tm, d), lambda i, e: (i, 0)),
                   pl.BlockSpec((tm, d), lambda i, e: (i, 0))],
        out_shape=[jax.ShapeDtypeStruct((t, d), F32),
                   jax.ShapeDtypeStruct((t, d), BF16)],
        scratch_shapes=[pltpu.VMEM((tm, LANES), F32),
                        pltpu.VMEM((tm, d), F32)],
        compiler_params=_params("parallel", "arbitrary"),
        name="moe_residual_ln",
    )(x, xb, rwt, rb, wg, wu, wd, lw.reshape(1, d), lb.reshape(1, d))


def _lambda_init_for(layer):
    return 0.8 - 0.6 * math.exp(-0.3 * layer)


def kernel(x, a_w_in, a_conv, a_A_log, a_dt_bias, a_norm_w, a_w_out, b_w_q, b_w_kv, b_lambda, b_subln_w, b_w_out, rel_bias, router_w, router_b, moe_w_gate, moe_w_up, moe_w_down, ln_w, ln_b):
    bsz, t, d = x.shape
    assert bsz == 1 and d == D_MODEL and t % A_CHUNK == 0
    xf = x.reshape(t, d).astype(F32)
    xb = xf.astype(BF16)

    gate_tile = min(512, t)
    tri = jnp.asarray(np.kron(np.eye(gate_tile // A_CHUNK, dtype=np.float32),
                              np.tril(np.ones((A_CHUNK, A_CHUNK), np.float32))))
    rwt = router_w.T.astype(F32)
    rb = router_b.reshape(N_EXPERTS, 1).astype(F32)
    attn_tile = min(512, t)
    bias_tiles = attention_bias_tiles(rel_bias.astype(F32), attn_tile)

    kb = None
    vt = None
    for l in range(DEPTH):
        if l < N_A_LAYERS:
            w_in = a_w_in[l]
            w_main = w_in[:, :A_CONV_CH + A_V].astype(BF16)
            w_ab = jnp.pad(w_in[:, A_CONV_CH + A_V:], ((0, 0), (0, LANES - 2 * A_HEADS))).astype(BF16)
            conv_pad = jnp.pad(a_conv[l].astype(F32), ((0, 0), (0, A_V)))
            par = jnp.zeros((SUBLANES, LANES), F32)
            par = par.at[0, :A_HEADS].set(a_A_log[l].astype(F32))
            par = par.at[1, :A_HEADS].set(a_dt_bias[l].astype(F32))
            proj = gdn_inproj(xb, w_main, conv_pad)
            fac, fact = gdn_gates(xb, w_ab, par, tri)
            fact3 = fact.reshape(2 * A_HEADS, t // A_CHUNK, A_CHUNK).transpose(1, 0, 2)
            mixed = gdn_chunk(proj, fac, fact3, a_norm_w[l].astype(F32))
            w_out = a_w_out[l].astype(BF16)
        else:
            j = l - N_A_LAYERS
            if kb is None:
                kb = matmul_bf16(xb, b_w_kv[:, :B_QK].astype(BF16))
                vt = v_transposed_proj(xb, b_w_kv[:, B_QK:].T.astype(BF16), attn_tile)
            qb = matmul_bf16(xb, b_w_q[j].astype(BF16), scale=B_DH ** -0.5 * LOG2E)
            mixed = diff_attention(qb, kb, vt, bias_tiles, b_lambda[j].astype(F32),
                                   b_subln_w[j].astype(F32), _lambda_init_for(l), tile=attn_tile)
            w_out = b_w_out[j].astype(BF16)
        xf, xb = matmul_residual_ln(mixed, w_out, xf, ln_w[l, 0].astype(F32), ln_b[l, 0].astype(F32))
        xf, xb = moe_residual_ln(xf, xb, rwt, rb, moe_w_gate[l].astype(BF16), moe_w_up[l].astype(BF16),
                                 moe_w_down[l].astype(BF16), ln_w[l, 1].astype(F32), ln_b[l, 1].astype(F32))
    return xf.reshape(bsz, t, d).astype(x.dtype)
```

```python
import functools
import math

import jax
import jax.numpy as jnp
import numpy as np
from jax import lax
from jax.experimental import pallas as pl
from jax.experimental.pallas import tpu as pltpu

F32 = jnp.float32
BF16 = jnp.bfloat16

D_MODEL = 1024
DEPTH = 4
N_A_LAYERS = DEPTH // 2
A_HEADS = 8
A_DK = 128
A_DV = 128
A_QK = A_HEADS * A_DK
A_V = A_HEADS * A_DV
A_CONV = 4
A_CONV_CH = 2 * A_QK + A_V
A_CHUNK = 64
B_HEADS = 8
B_DH = 64
B_QK = B_HEADS * 2 * B_DH
REL_BUCKETS = 32
REL_MAX_DIST = 128
N_EXPERTS = 16
N_GROUPS = 4
EXPERTS_PER_GROUP = N_EXPERTS // N_GROUPS
D_EXPERT = 512
DN_ALPHA = (2 * DEPTH) ** 0.25
LN_EPS = 1e-5
RMS_EPS = 1e-6

LANES = 128
SUBLANES = 8
VMEM_LIMIT = 48 * 1024 * 1024
MASK_VALUE = -1e30

HIGHEST = lax.Precision.HIGHEST
NT_DIMS = (((1,), (1,)), ((), ()))
TN_DIMS = (((0,), (0,)), ((), ()))


def _params(*sem):
    return pltpu.CompilerParams(dimension_semantics=sem, vmem_limit_bytes=VMEM_LIMIT)


def _sigmoid(x):
    return 1.0 / (1.0 + jnp.exp(-x))


def _silu(x):
    return x * _sigmoid(x)


def _layer_norm(v, w, b):
    mu = jnp.mean(v, axis=-1, keepdims=True)
    c = v - mu
    var = jnp.mean(c * c, axis=-1, keepdims=True)
    return c * lax.rsqrt(var + LN_EPS) * w + b


def _mm_kernel(a_ref, w_ref, o_ref, *, scale):
    y = jnp.dot(a_ref[...], w_ref[...], preferred_element_type=F32)
    if scale != 1.0:
        y = y * scale
    o_ref[...] = y.astype(o_ref.dtype)


def matmul_bf16(a, w, *, scale=1.0, tm=512, tn=1024):
    m, k = a.shape
    n = w.shape[1]
    tm = min(tm, m)
    return pl.pallas_call(
        functools.partial(_mm_kernel, scale=scale),
        grid=(n // tn, m // tm),
        in_specs=[pl.BlockSpec((tm, k), lambda j, i: (i, 0)),
                  pl.BlockSpec((k, tn), lambda j, i: (0, j))],
        out_specs=pl.BlockSpec((tm, tn), lambda j, i: (i, j)),
        out_shape=jax.ShapeDtypeStruct((m, n), BF16),
        compiler_params=_params("parallel", "arbitrary"),
        name="matmul_bf16",
    )(a, w)


def _mm_ln_kernel(a_ref, w_ref, x_ref, lw_ref, lb_ref, o_ref, ob_ref):
    h = jnp.dot(a_ref[...], w_ref[...], preferred_element_type=F32)
    y = _layer_norm(DN_ALPHA * x_ref[...] + h, lw_ref[...], lb_ref[...])
    o_ref[...] = y
    ob_ref[...] = y.astype(BF16)


def matmul_residual_ln(a, w, x, lw, lb, *, tm=512):
    m, k = a.shape
    n = w.shape[1]
    tm = min(tm, m)
    return pl.pallas_call(
        _mm_ln_kernel,
        grid=(m // tm,),
        in_specs=[pl.BlockSpec((tm, k), lambda i: (i, 0)),
                  pl.BlockSpec((k, n), lambda i: (0, 0)),
                  pl.BlockSpec((tm, n), lambda i: (i, 0)),
                  pl.BlockSpec((1, n), lambda i: (0, 0)),
                  pl.BlockSpec((1, n), lambda i: (0, 0))],
        out_specs=[pl.BlockSpec((tm, n), lambda i: (i, 0)),
                   pl.BlockSpec((tm, n), lambda i: (i, 0))],
        out_shape=[jax.ShapeDtypeStruct((m, n), F32),
                   jax.ShapeDtypeStruct((m, n), BF16)],
        compiler_params=_params("parallel"),
        name="matmul_residual_ln",
    )(a, w, x, lw.reshape(1, n), lb.reshape(1, n))


def _gdn_inproj_kernel(x_ref, w_ref, cw_ref, o_ref, halo_ref):
    n = pl.program_id(0)
    m = pl.program_id(1)
    tm, tn = o_ref.shape
    y = jnp.dot(x_ref[...], w_ref[...], preferred_element_type=F32)

    @pl.when(n == 3)
    def _():
        o_ref[...] = y.astype(o_ref.dtype)

    @pl.when(n < 3)
    def _():
        @pl.when(m == 0)
        def _():
            halo_ref[...] = jnp.zeros_like(halo_ref)

        halo = halo_ref[...]
        cw = cw_ref[...]
        rows = lax.broadcasted_iota(jnp.int32, (SUBLANES, tn), 0)
        acc = y * cw[A_CONV - 1:A_CONV]
        for shift in range(1, A_CONV):
            ys = pltpu.roll(y, shift, 0)
            top = jnp.where(rows < shift, pltpu.roll(halo, shift, 0), ys[:SUBLANES])
            ys = jnp.concatenate([top, ys[SUBLANES:]], axis=0)
            acc = acc + ys * cw[A_CONV - 1 - shift:A_CONV - shift]
        halo_ref[...] = y[tm - SUBLANES:]
        a = _silu(acc)

        @pl.when(n == 2)
        def _():
            o_ref[...] = a.astype(o_ref.dtype)

        @pl.when(n < 2)
        def _():
            post = jnp.where(n == 0, A_DK ** -0.5, 1.0).astype(F32)
            for h in range(tn // A_DK):
                seg = a[:, h * A_DK:(h + 1) * A_DK]
                ss = jnp.sum(seg * seg, axis=-1, keepdims=True)
                o_ref[:, h * A_DK:(h + 1) * A_DK] = (seg * (lax.rsqrt(ss + RMS_EPS) * post)).astype(o_ref.dtype)


def gdn_inproj(xb, w_main, conv_w_pad, *, tm=512):
    t, k = xb.shape
    n = w_main.shape[1]
    tn = A_QK
    tm = min(tm, t)
    return pl.pallas_call(
        _gdn_inproj_kernel,
        grid=(n // tn, t // tm),
        in_specs=[pl.BlockSpec((tm, k), lambda j, i: (i, 0)),
                  pl.BlockSpec((k, tn), lambda j, i: (0, j)),
                  pl.BlockSpec((A_CONV, tn), lambda j, i: (0, j))],
        out_specs=pl.BlockSpec((tm, tn), lambda j, i: (i, j)),
        out_shape=jax.ShapeDtypeStruct((t, n), BF16),
        scratch_shapes=[pltpu.VMEM((SUBLANES, tn), F32)],
        compiler_params=_params("arbitrary", "arbitrary"),
        name="gdn_inproj",
    )(xb, w_main, conv_w_pad)


def _gdn_gates_kernel(x_ref, w_ref, par_ref, tri_ref, fac_ref, fact_ref):
    ab = jnp.dot(x_ref[...], w_ref[...], preferred_element_type=F32)
    lane = lax.broadcasted_iota(jnp.int32, ab.shape, 1)
    neg_a = -jnp.exp(par_ref[0:1, :])
    z = ab + par_ref[1:2, :]
    softplus = jnp.maximum(z, 0.0) + jnp.log1p(jnp.exp(-jnp.abs(z)))
    g = jnp.where(lane < A_HEADS, neg_a * softplus, 0.0)
    beta = _sigmoid(ab)
    gc = jnp.dot(tri_ref[...], g, preferred_element_type=F32, precision=HIGHEST)
    fac = jnp.where(lane < A_HEADS, gc, jnp.where(lane < 2 * A_HEADS, beta, 0.0))
    fac_ref[...] = fac
    fact_ref[...] = fac.T[:2 * A_HEADS, :]


def gdn_gates(xb, w_ab_pad, par, tri, *, tm=512):
    t, k = xb.shape
    tm = min(tm, t)
    return pl.pallas_call(
        _gdn_gates_kernel,
        grid=(t // tm,),
        in_specs=[pl.BlockSpec((tm, k), lambda i: (i, 0)),
                  pl.BlockSpec((k, LANES), lambda i: (0, 0)),
                  pl.BlockSpec((SUBLANES, LANES), lambda i: (0, 0)),
                  pl.BlockSpec((tm, tm), lambda i: (0, 0))],
        out_specs=[pl.BlockSpec((tm, LANES), lambda i: (i, 0)),
                   pl.BlockSpec((2 * A_HEADS, tm), lambda i: (0, i))],
        out_shape=[jax.ShapeDtypeStruct((t, LANES), F32),
                   jax.ShapeDtypeStruct((2 * A_HEADS, t), F32)],
        compiler_params=_params("parallel"),
        name="gdn_gates",
    )(xb, w_ab_pad, par, tri)


GDN_INTRA_CHUNKS = 2
GDN_SCAN_CHUNKS = 8


def _mm_bf16(a, b):
    return jnp.dot(a.astype(BF16), b.astype(BF16), preferred_element_type=F32)


def _gdn_intra_kernel(q_ref, k_ref, v_ref, fac_ref, fact_ref, u_ref, wq_ref, qk_ref, kd_ref, dec_ref):
    c = A_CHUNK
    ii = lax.broadcasted_iota(jnp.int32, (c, c), 0)
    jj = lax.broadcasted_iota(jnp.int32, (c, c), 1)
    incl = ii >= jj
    strict = ii > jj
    units = [(cb, h) for cb in range(GDN_INTRA_CHUNKS) for h in range(A_HEADS)]
    rows = {cb: slice(cb * c, (cb + 1) * c) for cb in range(GDN_INTRA_CHUNKS)}
    cols = {h: slice(h * A_DK, (h + 1) * A_DK) for h in range(A_HEADS)}

    q, k, kf, beta, eg, decay = {}, {}, {}, {}, {}, {}
    for cb, h in units:
        fac = fac_ref[rows[cb], :]
        gc_col = fac[:, h:h + 1]
        gc_row = fact_ref[cb][h:h + 1, :]
        g_last = gc_col[c - 1:c, :]
        beta[cb, h] = fac[:, A_HEADS + h:A_HEADS + h + 1]
        eg[cb, h] = jnp.exp(gc_col)
        decay[cb, h] = jnp.exp(jnp.where(incl, gc_col - gc_row, MASK_VALUE))
        q[cb, h] = q_ref[rows[cb], cols[h]]
        k[cb, h] = k_ref[rows[cb], cols[h]]
        kf[cb, h] = k[cb, h].astype(F32)
        kd_ref[rows[cb], cols[h]] = (kf[cb, h] * jnp.exp(g_last - gc_col)).astype(kd_ref.dtype)
        dec_ref[cb, h:h + 1, :] = jnp.broadcast_to(jnp.exp(g_last), (1, A_DV))
        wq_ref[cb, c:, cols[h]] = (q[cb, h].astype(F32) * eg[cb, h]).astype(wq_ref.dtype)

    mmat, x, p = {}, {}, {}
    for u_ in units:
        kk = lax.dot_general(k[u_], k[u_], NT_DIMS, preferred_element_type=F32)
        mmat[u_] = jnp.where(strict, beta[u_] * kk * decay[u_], 0.0)
    for u_ in units:
        cb, h = u_
        qk = lax.dot_general(q[u_], k[u_], NT_DIMS, preferred_element_type=F32) * decay[u_]
        qk_ref[cb, h] = qk.astype(qk_ref.dtype)
    for u_ in units:
        x[u_] = -mmat[u_]
        p[u_] = _mm_bf16(mmat[u_], mmat[u_])
    for level in range(5):
        for u_ in units:
            x[u_] = x[u_] + p[u_] + _mm_bf16(x[u_], p[u_])
        if level < 4:
            for u_ in units:
                p[u_] = _mm_bf16(p[u_], p[u_])
    for u_ in units:
        cb, h = u_
        v = v_ref[rows[cb], cols[h]].astype(F32)
        rhs = jnp.concatenate([beta[u_] * v, (beta[u_] * eg[u_]) * kf[u_]], axis=1)
        uw = rhs + _mm_bf16(x[u_], rhs)
        u_ref[rows[cb], cols[h]] = uw[:, :A_DV].astype(u_ref.dtype)
        wq_ref[cb, :c, cols[h]] = uw[:, A_DV:].astype(wq_ref.dtype)


def gdn_intra(proj, fac, fact3):
    t = proj.shape[0]
    c = A_CHUNK
    cb = GDN_INTRA_CHUNKS
    nc = t // c
    return pl.pallas_call(
        _gdn_intra_kernel,
        grid=(nc // cb,),
        in_specs=[pl.BlockSpec((cb * c, A_QK), lambda i: (i, 0)),
                  pl.BlockSpec((cb * c, A_QK), lambda i: (i, 1)),
                  pl.BlockSpec((cb * c, A_V), lambda i: (i, 2)),
                  pl.BlockSpec((cb * c, LANES), lambda i: (i, 0)),
                  pl.BlockSpec((cb, 2 * A_HEADS, c), lambda i: (i, 0, 0))],
        out_specs=[pl.BlockSpec((cb * c, A_V), lambda i: (i, 0)),
                   pl.BlockSpec((cb, 2 * c, A_QK), lambda i: (i, 0, 0)),
                   pl.BlockSpec((cb, A_HEADS, c, c), lambda i: (i, 0, 0, 0)),
                   pl.BlockSpec((cb * c, A_QK), lambda i: (i, 0)),
                   pl.BlockSpec((cb, A_HEADS, A_DV), lambda i: (i, 0, 0))],
        out_shape=[jax.ShapeDtypeStruct((t, A_V), BF16),
                   jax.ShapeDtypeStruct((nc, 2 * c, A_QK), BF16),
                   jax.ShapeDtypeStruct((nc, A_HEADS, c, c), BF16),
                   jax.ShapeDtypeStruct((t, A_QK), BF16),
                   jax.ShapeDtypeStruct((nc, A_HEADS, A_DV), F32)],
        compiler_params=_params("parallel"),
        name="gdn_intra",
    )(proj, proj, proj, fac, fact3)


def _gdn_scan_kernel(u_ref, wq_ref, qk_ref, kd_ref, dec_ref, z_ref, nw_ref, o_ref, s_ref):
    c = A_CHUNK

    @pl.when(pl.program_id(0) == 0)
    def _():
        s_ref[...] = jnp.zeros_like(s_ref)

    nw = nw_ref[...]
    heads = range(A_HEADS)
    cols = [slice(h * A_DK, (h + 1) * A_DK) for h in heads]
    for cb in range(GDN_SCAN_CHUNKS):
        rows = slice(cb * c, (cb + 1) * c)
        s = [s_ref[h] for h in heads]
        sb = [s[h].astype(BF16) for h in heads]
        ws_qs = [jnp.dot(wq_ref[cb, :, cols[h]], sb[h], preferred_element_type=F32) for h in heads]
        v_new = [(u_ref[rows, cols[h]].astype(F32) - ws_qs[h][:c]).astype(BF16) for h in heads]
        o = [ws_qs[h][c:] + jnp.dot(qk_ref[cb, h], v_new[h], preferred_element_type=F32) for h in heads]
        for h in heads:
            s_ref[h] = s[h] * dec_ref[cb, h:h + 1, :] + lax.dot_general(
                kd_ref[rows, cols[h]], v_new[h], TN_DIMS, preferred_element_type=F32)
        for h in heads:
            z = z_ref[rows, cols[h]].astype(F32)
            ms = jnp.mean(o[h] * o[h], axis=-1, keepdims=True)
            o_ref[rows, cols[h]] = (o[h] * lax.rsqrt(ms + RMS_EPS) * nw * _silu(z)).astype(o_ref.dtype)


def gdn_scan(u, wq, qk, kd, dec, proj, norm_w):
    t = u.shape[0]
    c = A_CHUNK
    cb = GDN_SCAN_CHUNKS
    return pl.pallas_call(
        _gdn_scan_kernel,
        grid=(t // (cb * c),),
        in_specs=[pl.BlockSpec((cb * c, A_V), lambda i: (i, 0)),
                  pl.BlockSpec((cb, 2 * c, A_QK), lambda i: (i, 0, 0)),
                  pl.BlockSpec((cb, A_HEADS, c, c), lambda i: (i, 0, 0, 0)),
                  pl.BlockSpec((cb * c, A_QK), lambda i: (i, 0)),
                  pl.BlockSpec((cb, A_HEADS, A_DV), lambda i: (i, 0, 0)),
                  pl.BlockSpec((cb * c, A_V), lambda i: (i, 3)),
                  pl.BlockSpec((1, A_DV), lambda i: (0, 0))],
        out_specs=pl.BlockSpec((cb * c, A_V), lambda i: (i, 0)),
        out_shape=jax.ShapeDtypeStruct((t, A_V), BF16),
        scratch_shapes=[pltpu.VMEM((A_HEADS, A_DK, A_DV), F32)],
        compiler_params=_params("arbitrary"),
        name="gdn_scan",
    )(u, wq, qk, kd, dec, proj, norm_w.reshape(1, A_DV))


LOG2E = math.log2(math.e)
VT_ROWS = 2 * B_DH + 16


def _vt_proj_kernel(x_ref, w_ref, o_ref):
    tk = x_ref.shape[0]
    y = lax.dot_general(w_ref[...], x_ref[...], NT_DIMS, preferred_element_type=F32)
    ones = jnp.ones((VT_ROWS - 2 * B_DH, tk), o_ref.dtype)
    for h in range(B_HEADS):
        o_ref[0, h, :2 * B_DH, :] = y[h * 2 * B_DH:(h + 1) * 2 * B_DH].astype(o_ref.dtype)
        o_ref[0, h, 2 * B_DH:, :] = ones


def v_transposed_proj(xb, w_vt, tile):
    t, k = xb.shape
    n = w_vt.shape[0]
    return pl.pallas_call(
        _vt_proj_kernel,
        grid=(t // tile,),
        in_specs=[pl.BlockSpec((tile, k), lambda i: (i, 0)),
                  pl.BlockSpec((n, k), lambda i: (0, 0))],
        out_specs=pl.BlockSpec((1, B_HEADS, VT_ROWS, tile), lambda i: (i, 0, 0, 0)),
        out_shape=jax.ShapeDtypeStruct((t // tile, B_HEADS, VT_ROWS, tile), BF16),
        compiler_params=_params("parallel"),
        name="v_transposed_proj",
    )(xb, w_vt)


def _attn_kernel(q_ref, k_ref, vt_ref, bias_ref, lam_ref, swb_ref, o_ref, m_ref, acc_ref,
                 sta_ref, mta_ref, stb_ref, mtb_ref, *, lambda_init, tile):
    qi = pl.program_id(1)
    dv = 2 * B_DH
    q = q_ref[...]
    first_map = lax.broadcasted_iota(jnp.int32, (tile, dv), 1) < B_DH
    m_ref[...] = jnp.full_like(m_ref, MASK_VALUE)
    acc_ref[...] = jnp.zeros_like(acc_ref)

    def produce(j, st_ref, mt_ref):
        k = k_ref[pl.ds(pl.multiple_of(j * tile, tile), tile), :]
        zero = jnp.zeros_like(k)
        k2 = jnp.concatenate([jnp.where(first_map, k, zero), jnp.where(first_map, zero, k)], axis=0)
        st = lax.dot_general(k2, q, NT_DIMS, preferred_element_type=F32)
        st_ref[...] = st
        for i in range(2):
            mt_ref[i] = jnp.max(st[i * tile:(i + 1) * tile], axis=0, keepdims=True)

    def consume(j, st_ref, mt_ref, bias):
        vt = vt_ref[j, 0]
        for i in range(2):
            s = st_ref[i * tile:(i + 1) * tile, :]
            m_prev = m_ref[i]
            if bias is None:
                m_new = jnp.maximum(m_prev, mt_ref[i])
            else:
                s = s + bias
                m_new = jnp.maximum(m_prev, jnp.max(s, axis=0, keepdims=True))
            alpha = jnp.exp2(m_prev - m_new)
            p = jnp.exp2(s - m_new).astype(BF16)
            acc_ref[i] = alpha * acc_ref[i] + jnp.dot(vt, p, preferred_element_type=F32)
            m_ref[i] = m_new

    n_far = jnp.maximum(qi - 1, 0)
    n_pairs = n_far // 2
    produce(0, sta_ref, mta_ref)

    def pair_body(jj, carry):
        j = 2 * jj
        produce(j + 1, stb_ref, mtb_ref)
        consume(j, sta_ref, mta_ref, None)
        produce(j + 2, sta_ref, mta_ref)
        consume(j + 1, stb_ref, mtb_ref, None)
        return carry

    lax.fori_loop(0, n_pairs, pair_body, 0)

    @pl.when(n_far > 2 * n_pairs)
    def _():
        consume(n_far - 1, sta_ref, mta_ref, None)
        produce(n_far, sta_ref, mta_ref)

    @pl.when(qi >= 1)
    def _():
        produce(qi, stb_ref, mtb_ref)
        consume(qi - 1, sta_ref, mta_ref, bias_ref[0, 1])
        consume(qi, stb_ref, mtb_ref, bias_ref[0, 0])

    @pl.when(qi == 0)
    def _():
        consume(0, sta_ref, mta_ref, bias_ref[0, 0])

    lam = lam_ref[...]
    lam_val = (jnp.exp(jnp.sum(lam[0:1] * lam[1:2], axis=-1, keepdims=True))
               - jnp.exp(jnp.sum(lam[2:3] * lam[3:4], axis=-1, keepdims=True)) + lambda_init)
    a0 = acc_ref[0]
    a1 = acc_ref[1]
    ot = a0[:dv] / a0[dv:dv + 1] - lam_val * (a1[:dv] / a1[dv:dv + 1])
    ms = jnp.mean(ot * ot, axis=0, keepdims=True)
    ot = ot * lax.rsqrt(ms + RMS_EPS) * swb_ref[...] * (1.0 - lambda_init)
    o_ref[...] = ot.T.astype(o_ref.dtype)


def diff_attention(qb, kb, vt, bias_tiles, lam, subln_w, lambda_init, *, tile):
    t = qb.shape[0]
    nb = t // tile
    dv = 2 * B_DH
    swb = jnp.broadcast_to(subln_w.reshape(dv, 1), (dv, tile))
    return pl.pallas_call(
        functools.partial(_attn_kernel, lambda_init=lambda_init, tile=tile),
        grid=(B_HEADS, nb),
        in_specs=[pl.BlockSpec((tile, dv), lambda h, i: (i, h)),
                  pl.BlockSpec((t, dv), lambda h, i: (0, h)),
                  pl.BlockSpec((nb, 1, VT_ROWS, tile), lambda h, i: (0, h, 0, 0)),
                  pl.BlockSpec((1, 2, tile, tile), lambda h, i: (h, 0, 0, 0)),
                  pl.BlockSpec((4, B_DH), lambda h, i: (0, 0)),
                  pl.BlockSpec((dv, tile), lambda h, i: (0, 0))],
        out_specs=pl.BlockSpec((tile, dv), lambda h, i: (i, h)),
        out_shape=jax.ShapeDtypeStruct((t, B_HEADS * dv), BF16),
        scratch_shapes=[pltpu.VMEM((2, 1, tile), F32),
                        pltpu.VMEM((2, VT_ROWS, tile), F32),
                        pltpu.VMEM((2 * tile, tile), F32),
                        pltpu.VMEM((2, 1, tile), F32),
                        pltpu.VMEM((2 * tile, tile), F32),
                        pltpu.VMEM((2, 1, tile), F32)],
        compiler_params=_params("parallel", "arbitrary"),
        name="diff_attention",
    )(qb, kb, vt, bias_tiles, lam, swb)


def _t5_bucket(n):
    max_exact = REL_BUCKETS // 2
    nf = jnp.maximum(n, 1).astype(F32)
    large = max_exact + (jnp.log(nf / max_exact) / math.log(REL_MAX_DIST / max_exact)
                         * (REL_BUCKETS - max_exact)).astype(jnp.int32)
    large = jnp.minimum(large, REL_BUCKETS - 1)
    return jnp.where(n < max_exact, n, large)


def attention_bias_tiles(rel_bias, tile):
    assert tile >= REL_MAX_DIST
    n = jnp.arange(2 * tile)
    far = rel_bias[REL_BUCKETS - 1]
    db = ((rel_bias[_t5_bucket(n)] - far[None, :]) * LOG2E).T.astype(F32)
    nh = db.shape[0]
    full = jnp.concatenate([jnp.full((nh, tile - 1), MASK_VALUE, F32), db], axis=1)

    def toeplitz(vec):
        w = jnp.concatenate([vec, jnp.zeros((nh, 1), F32)], axis=1)
        flat = jnp.tile(w, (1, tile))[:, :tile * (2 * tile - 1)]
        return flat.reshape(nh, tile, 2 * tile - 1)[:, :, tile - 1:]

    diag = toeplitz(full[:, :2 * tile - 1])
    near = toeplitz(full[:, tile:3 * tile - 1])
    return jnp.stack([diag, near], axis=1)


def _router_gates_t(x, rwt, rb):
    logits = lax.dot_general(rwt, x, NT_DIMS, preferred_element_type=F32, precision=HIGHEST) + rb
    mx = jnp.max(logits, axis=0, keepdims=True)
    ex = jnp.exp(logits - mx)
    probs = ex / jnp.sum(ex, axis=0, keepdims=True)
    row = [probs[e:e + 1, :] for e in range(N_EXPERTS)]
    epg = EXPERTS_PER_GROUP
    best_sum = None
    best_group = None
    for g in range(N_GROUPS):
        vals = row[g * epg:(g + 1) * epg]
        top2 = None
        for a in range(epg):
            for b in range(a + 1, epg):
                pair = vals[a] + vals[b]
                top2 = pair if top2 is None else jnp.maximum(top2, pair)
        if g == 0:
            best_sum = top2
            best_group = jnp.zeros_like(top2, dtype=jnp.int32)
        else:
            upd = top2 > best_sum
            best_sum = jnp.where(upd, top2, best_sum)
            best_group = jnp.where(upd, g, best_group)
    gp = []
    for j in range(epg):
        sel = row[j]
        for g in range(1, N_GROUPS):
            sel = jnp.where(best_group == g, row[g * epg + j], sel)
        gp.append(sel)
    v1 = gp[0]
    i1 = jnp.zeros_like(best_group)
    for j in range(1, epg):
        upd = gp[j] > v1
        v1 = jnp.where(upd, gp[j], v1)
        i1 = jnp.where(upd, j, i1)
    v2 = jnp.full_like(v1, -1.0)
    i2 = jnp.zeros_like(best_group)
    for j in range(epg):
        upd = (i1 != j) & (gp[j] > v2)
        v2 = jnp.where(upd, gp[j], v2)
        i2 = jnp.where(upd, j, i2)
    denom = v1 + v2
    e1 = best_group * epg + i1
    e2 = best_group * epg + i2
    eidx = lax.broadcasted_iota(jnp.int32, logits.shape, 0)
    return jnp.where(eidx == e1, v1 / denom, 0.0) + jnp.where(eidx == e2, v2 / denom, 0.0)


def _moe_dense_kernel(x_ref, xb_ref, rwt_ref, rb_ref, wg_ref, wu_ref, wd_ref, lw_ref, lb_ref,
                      o_ref, ob_ref, gate_ref, acc_ref):
    e = pl.program_id(1)
    tm = x_ref.shape[0]

    @pl.when(e == 0)
    def _():
        gates_t = _router_gates_t(x_ref[...], rwt_ref[...], rb_ref[...])
        padded = jnp.concatenate([gates_t, jnp.zeros((LANES - N_EXPERTS, tm), F32)], axis=0)
        gate_ref[...] = padded.T
        acc_ref[...] = jnp.zeros_like(acc_ref)

    xb = xb_ref[...]
    lane = lax.broadcasted_iota(jnp.int32, gate_ref.shape, 1)
    g_e = jnp.sum(jnp.where(lane == e, gate_ref[...], 0.0), axis=-1, keepdims=True)
    hg = jnp.dot(xb, wg_ref[0], preferred_element_type=F32)
    hu = jnp.dot(xb, wu_ref[0], preferred_element_type=F32)
    hid = (_silu(hg) * hu).astype(BF16)
    acc_ref[...] += g_e * jnp.dot(hid, wd_ref[0], preferred_element_type=F32)

    @pl.when(e == N_EXPERTS - 1)
    def _():
        y = _layer_norm(DN_ALPHA * x_ref[...] + acc_ref[...], lw_ref[...], lb_ref[...])
        o_ref[...] = y
        ob_ref[...] = y.astype(BF16)


def moe_residual_ln(x, xb, rwt, rb, wg, wu, wd, lw, lb, *, tm=512):
    t, d = x.shape
    tm = min(tm, t)
    return pl.pallas_call(
        _moe_dense_kernel,
        grid=(t // tm, N_EXPERTS),
        in_specs=[pl.BlockSpec((tm, d), lambda i, e: (i, 0)),
                  pl.BlockSpec((tm, d), lambda i, e: (i, 0)),
                  pl.BlockSpec((N_EXPERTS, d), lambda i, e: (0, 0)),
                  pl.BlockSpec((N_EXPERTS, 1), lambda i, e: (0, 0)),
                  pl.BlockSpec((1, d, D_EXPERT), lambda i, e: (e, 0, 0)),
                  pl.BlockSpec((1, d, D_EXPERT), lambda i, e: (e, 0, 0)),
                  pl.BlockSpec((1, D_EXPERT, d), lambda i, e: (e, 0, 0)),
                  pl.BlockSpec((1, d), lambda i, e: (0, 0)),
                  pl.BlockSpec((1, d), lambda i, e: (0, 0))],
        out_specs=[pl.BlockSpec((tm, d), lambda i, e: (i, 0)),
                   pl.BlockSpec((tm, d), lambda i, e: (i, 0))],
        out_shape=[jax.ShapeDtypeStruct((t, d), F32),
                   jax.ShapeDtypeStruct((t, d), BF16)],
        scratch_shapes=[pltpu.VMEM((tm, LANES), F32),
                        pltpu.VMEM((tm, d), F32)],
        compiler_params=_params("parallel", "arbitrary"),
        name="moe_residual_ln",
    )(x, xb, rwt, rb, wg, wu, wd, lw.reshape(1, d), lb.reshape(1, d))


def _lambda_init_for(layer):
    return 0.8 - 0.6 * math.exp(-0.3 * layer)


def kernel(x, a_w_in, a_conv, a_A_log, a_dt_bias, a_norm_w, a_w_out, b_w_q, b_w_kv, b_lambda, b_subln_w, b_w_out, rel_bias, router_w, router_b, moe_w_gate, moe_w_up, moe_w_down, ln_w, ln_b):
    bsz, t, d = x.shape
    assert bsz == 1 and d == D_MODEL and t % A_CHUNK == 0
    xf = x.reshape(t, d).astype(F32)
    xb = xf.astype(BF16)

    gate_tile = min(512, t)
    tri = jnp.asarray(np.kron(np.eye(gate_tile // A_CHUNK, dtype=np.float32),
                              np.tril(np.ones((A_CHUNK, A_CHUNK), np.float32))))
    rwt = router_w.T.astype(F32)
    rb = router_b.reshape(N_EXPERTS, 1).astype(F32)
    attn_tile = min(512, t)
    bias_tiles = attention_bias_tiles(rel_bias.astype(F32), attn_tile)

    kb = None
    vt = None
    for l in range(DEPTH):
        if l < N_A_LAYERS:
            w_in = a_w_in[l]
            w_main = w_in[:, :A_CONV_CH + A_V].astype(BF16)
            w_ab = jnp.pad(w_in[:, A_CONV_CH + A_V:], ((0, 0), (0, LANES - 2 * A_HEADS))).astype(BF16)
            conv_pad = jnp.pad(a_conv[l].astype(F32), ((0, 0), (0, A_V)))
            par = jnp.zeros((SUBLANES, LANES), F32)
            par = par.at[0, :A_HEADS].set(a_A_log[l].astype(F32))
            par = par.at[1, :A_HEADS].set(a_dt_bias[l].astype(F32))
            proj = gdn_inproj(xb, w_main, conv_pad)
            fac, fact = gdn_gates(xb, w_ab, par, tri)
            fact3 = fact.reshape(2 * A_HEADS, t // A_CHUNK, A_CHUNK).transpose(1, 0, 2)
            u, wq, qk, kd, dec = gdn_intra(proj, fac, fact3)
            mixed = gdn_scan(u, wq, qk, kd, dec, proj, a_norm_w[l].astype(F32))
            w_out = a_w_out[l].astype(BF16)
        else:
            j = l - N_A_LAYERS
            if kb is None:
                kb = matmul_bf16(xb, b_w_kv[:, :B_QK].astype(BF16))
                vt = v_transposed_proj(xb, b_w_kv[:, B_QK:].T.astype(BF16), attn_tile)
            qb = matmul_bf16(xb, b_w_q[j].astype(BF16), scale=B_DH ** -0.5 * LOG2E)
            mixed = diff_attention(qb, kb, vt, bias_tiles, b_lambda[j].astype(F32),
                                   b_subln_w[j].astype(F32), _lambda_init_for(l), tile=attn_tile)
            w_out = b_w_out[j].astype(BF16)
        xf, xb = matmul_residual_ln(mixed, w_out, xf, ln_w[l, 0].astype(F32), ln_b[l, 0].astype(F32))
        xf, xb = moe_residual_ln(xf, xb, rwt, rb, moe_w_gate[l].astype(BF16), moe_w_up[l].astype(BF16),
                                 moe_w_down[l].astype(BF16), ln_w[l, 1].astype(F32), ln_b[l, 1].astype(F32))
    return xf.reshape(bsz, t, d).astype(x.dtype)
```

```python
import functools
import math

import jax
import jax.numpy as jnp
import numpy as np
from jax import lax
from jax.experimental import pallas as pl
from jax.experimental.pallas import tpu as pltpu

F32 = jnp.float32
BF16 = jnp.bfloat16

D_MODEL = 1024
DEPTH = 4
N_A_LAYERS = DEPTH // 2
A_HEADS = 8
A_DK = 128
A_DV = 128
A_QK = A_HEADS * A_DK
A_V = A_HEADS * A_DV
A_CONV = 4
A_CONV_CH = 2 * A_QK + A_V
A_CHUNK = 64
B_HEADS = 8
B_DH = 64
B_QK = B_HEADS * 2 * B_DH
REL_BUCKETS = 32
REL_MAX_DIST = 128
N_EXPERTS = 16
N_GROUPS = 4
EXPERTS_PER_GROUP = N_EXPERTS // N_GROUPS
D_EXPERT = 512
DN_ALPHA = (2 * DEPTH) ** 0.25
LN_EPS = 1e-5
RMS_EPS = 1e-6

LANES = 128
SUBLANES = 8
VMEM_LIMIT = 48 * 1024 * 1024
MASK_VALUE = -1e30

HIGHEST = lax.Precision.HIGHEST
NT_DIMS = (((1,), (1,)), ((), ()))
TN_DIMS = (((0,), (0,)), ((), ()))


def _params(*sem):
    return pltpu.CompilerParams(dimension_semantics=sem, vmem_limit_bytes=VMEM_LIMIT)


def _sigmoid(x):
    return 1.0 / (1.0 + jnp.exp(-x))


def _silu(x):
    return x * _sigmoid(x)


def _layer_norm(v, w, b):
    mu = jnp.mean(v, axis=-1, keepdims=True)
    c = v - mu
    var = jnp.mean(c * c, axis=-1, keepdims=True)
    return c * lax.rsqrt(var + LN_EPS) * w + b


def _mm_kernel(a_ref, w_ref, o_ref, *, scale):
    y = jnp.dot(a_ref[...], w_ref[...], preferred_element_type=F32)
    if scale != 1.0:
        y = y * scale
    o_ref[...] = y.astype(o_ref.dtype)


def matmul_bf16(a, w, *, scale=1.0, tm=512, tn=1024):
    m, k = a.shape
    n = w.shape[1]
    tm = min(tm, m)
    return pl.pallas_call(
        functools.partial(_mm_kernel, scale=scale),
        grid=(n // tn, m // tm),
        in_specs=[pl.BlockSpec((tm, k), lambda j, i: (i, 0)),
                  pl.BlockSpec((k, tn), lambda j, i: (0, j))],
        out_specs=pl.BlockSpec((tm, tn), lambda j, i: (i, j)),
        out_shape=jax.ShapeDtypeStruct((m, n), BF16),
        compiler_params=_params("parallel", "arbitrary"),
        name="matmul_bf16",
    )(a, w)


def _mm_ln_kernel(a_ref, w_ref, x_ref, lw_ref, lb_ref, ot_ref):
    h = jnp.dot(a_ref[...], w_ref[...], preferred_element_type=F32)
    y = _layer_norm(DN_ALPHA * x_ref[...] + h, lw_ref[...], lb_ref[...])
    ot_ref[...] = y.T


def matmul_residual_ln_t(a, w, x, lw, lb, *, tm=512):
    m, k = a.shape
    n = w.shape[1]
    tm = min(tm, m)
    return pl.pallas_call(
        _mm_ln_kernel,
        grid=(m // tm,),
        in_specs=[pl.BlockSpec((tm, k), lambda i: (i, 0)),
                  pl.BlockSpec((k, n), lambda i: (0, 0)),
                  pl.BlockSpec((tm, n), lambda i: (i, 0)),
                  pl.BlockSpec((1, n), lambda i: (0, 0)),
                  pl.BlockSpec((1, n), lambda i: (0, 0))],
        out_specs=pl.BlockSpec((n, tm), lambda i: (0, i)),
        out_shape=jax.ShapeDtypeStruct((n, m), F32),
        compiler_params=_params("parallel"),
        name="matmul_residual_ln",
    )(a, w, x, lw.reshape(1, n), lb.reshape(1, n))


def _gdn_inproj_kernel(x_ref, w_ref, cw_ref, o_ref, halo_ref):
    n = pl.program_id(0)
    m = pl.program_id(1)
    tm, tn = o_ref.shape
    y = jnp.dot(x_ref[...], w_ref[...], preferred_element_type=F32)

    @pl.when(n == 3)
    def _():
        o_ref[...] = y.astype(o_ref.dtype)

    @pl.when(n < 3)
    def _():
        @pl.when(m == 0)
        def _():
            halo_ref[...] = jnp.zeros_like(halo_ref)

        halo = halo_ref[...]
        cw = cw_ref[...]
        rows = lax.broadcasted_iota(jnp.int32, (SUBLANES, tn), 0)
        acc = y * cw[A_CONV - 1:A_CONV]
        for shift in range(1, A_CONV):
            ys = pltpu.roll(y, shift, 0)
            top = jnp.where(rows < shift, pltpu.roll(halo, shift, 0), ys[:SUBLANES])
            ys = jnp.concatenate([top, ys[SUBLANES:]], axis=0)
            acc = acc + ys * cw[A_CONV - 1 - shift:A_CONV - shift]
        halo_ref[...] = y[tm - SUBLANES:]
        a = _silu(acc)

        @pl.when(n == 2)
        def _():
            o_ref[...] = a.astype(o_ref.dtype)

        @pl.when(n < 2)
        def _():
            post = jnp.where(n == 0, A_DK ** -0.5, 1.0).astype(F32)
            for h in range(tn // A_DK):
                seg = a[:, h * A_DK:(h + 1) * A_DK]
                ss = jnp.sum(seg * seg, axis=-1, keepdims=True)
                o_ref[:, h * A_DK:(h + 1) * A_DK] = (seg * (lax.rsqrt(ss + RMS_EPS) * post)).astype(o_ref.dtype)


def gdn_inproj(xb, w_main, conv_w_pad, *, tm=512):
    t, k = xb.shape
    n = w_main.shape[1]
    tn = A_QK
    tm = min(tm, t)
    return pl.pallas_call(
        _gdn_inproj_kernel,
        grid=(n // tn, t // tm),
        in_specs=[pl.BlockSpec((tm, k), lambda j, i: (i, 0)),
                  pl.BlockSpec((k, tn), lambda j, i: (0, j)),
                  pl.BlockSpec((A_CONV, tn), lambda j, i: (0, j))],
        out_specs=pl.BlockSpec((tm, tn), lambda j, i: (i, j)),
        out_shape=jax.ShapeDtypeStruct((t, n), BF16),
        scratch_shapes=[pltpu.VMEM((SUBLANES, tn), F32)],
        compiler_params=_params("arbitrary", "arbitrary"),
        name="gdn_inproj",
    )(xb, w_main, conv_w_pad)


def _gdn_gates_kernel(x_ref, w_ref, par_ref, tri_ref, fac_ref, fact_ref):
    ab = jnp.dot(x_ref[...], w_ref[...], preferred_element_type=F32)
    lane = lax.broadcasted_iota(jnp.int32, ab.shape, 1)
    neg_a = -jnp.exp(par_ref[0:1, :])
    z = ab + par_ref[1:2, :]
    softplus = jnp.maximum(z, 0.0) + jnp.log1p(jnp.exp(-jnp.abs(z)))
    g = jnp.where(lane < A_HEADS, neg_a * softplus, 0.0)
    beta = _sigmoid(ab)
    gc = jnp.dot(tri_ref[...], g, preferred_element_type=F32, precision=HIGHEST)
    fac = jnp.where(lane < A_HEADS, gc, jnp.where(lane < 2 * A_HEADS, beta, 0.0))
    fac_ref[...] = fac
    fact_ref[...] = fac.T[:2 * A_HEADS, :]


def gdn_gates(xb, w_ab_pad, par, tri, *, tm=512):
    t, k = xb.shape
    tm = min(tm, t)
    return pl.pallas_call(
        _gdn_gates_kernel,
        grid=(t // tm,),
        in_specs=[pl.BlockSpec((tm, k), lambda i: (i, 0)),
                  pl.BlockSpec((k, LANES), lambda i: (0, 0)),
                  pl.BlockSpec((SUBLANES, LANES), lambda i: (0, 0)),
                  pl.BlockSpec((tm, tm), lambda i: (0, 0))],
        out_specs=[pl.BlockSpec((tm, LANES), lambda i: (i, 0)),
                   pl.BlockSpec((2 * A_HEADS, tm), lambda i: (0, i))],
        out_shape=[jax.ShapeDtypeStruct((t, LANES), F32),
                   jax.ShapeDtypeStruct((2 * A_HEADS, t), F32)],
        compiler_params=_params("parallel"),
        name="gdn_gates",
    )(xb, w_ab_pad, par, tri)


GDN_INTRA_CHUNKS = 2
GDN_SCAN_CHUNKS = 8


def _mm_bf16(a, b):
    return jnp.dot(a.astype(BF16), b.astype(BF16), preferred_element_type=F32)


def _gdn_intra_kernel(q_ref, k_ref, v_ref, fac_ref, fact_ref, u_ref, wq_ref, qk_ref, kd_ref, dec_ref):
    c = A_CHUNK
    ii = lax.broadcasted_iota(jnp.int32, (c, c), 0)
    jj = lax.broadcasted_iota(jnp.int32, (c, c), 1)
    incl = ii >= jj
    strict = ii > jj
    units = [(cb, h) for cb in range(GDN_INTRA_CHUNKS) for h in range(A_HEADS)]
    rows = {cb: slice(cb * c, (cb + 1) * c) for cb in range(GDN_INTRA_CHUNKS)}
    cols = {h: slice(h * A_DK, (h + 1) * A_DK) for h in range(A_HEADS)}

    q, k, kf, beta, eg, decay = {}, {}, {}, {}, {}, {}
    for cb, h in units:
        fac = fac_ref[rows[cb], :]
        gc_col = fac[:, h:h + 1]
        gc_row = fact_ref[cb][h:h + 1, :]
        g_last = gc_col[c - 1:c, :]
        beta[cb, h] = fac[:, A_HEADS + h:A_HEADS + h + 1]
        eg[cb, h] = jnp.exp(gc_col)
        decay[cb, h] = jnp.exp(jnp.where(incl, gc_col - gc_row, MASK_VALUE))
        q[cb, h] = q_ref[rows[cb], cols[h]]
        k[cb, h] = k_ref[rows[cb], cols[h]]
        kf[cb, h] = k[cb, h].astype(F32)
        kd_ref[rows[cb], cols[h]] = (kf[cb, h] * jnp.exp(g_last - gc_col)).astype(kd_ref.dtype)
        dec_ref[cb, h:h + 1, :] = jnp.broadcast_to(jnp.exp(g_last), (1, A_DV))
        wq_ref[cb, c:, cols[h]] = (q[cb, h].astype(F32) * eg[cb, h]).astype(wq_ref.dtype)

    mmat, x, p = {}, {}, {}
    for u_ in units:
        kk = lax.dot_general(k[u_], k[u_], NT_DIMS, preferred_element_type=F32)
        mmat[u_] = jnp.where(strict, beta[u_] * kk * decay[u_], 0.0)
    for u_ in units:
        cb, h = u_
        qk = lax.dot_general(q[u_], k[u_], NT_DIMS, preferred_element_type=F32) * decay[u_]
        qk_ref[cb, h] = qk.astype(qk_ref.dtype)
    for u_ in units:
        x[u_] = -mmat[u_]
        p[u_] = _mm_bf16(mmat[u_], mmat[u_])
    for level in range(5):
        for u_ in units:
            x[u_] = x[u_] + p[u_] + _mm_bf16(x[u_], p[u_])
        if level < 4:
            for u_ in units:
                p[u_] = _mm_bf16(p[u_], p[u_])
    for u_ in units:
        cb, h = u_
        v = v_ref[rows[cb], cols[h]].astype(F32)
        rhs = jnp.concatenate([beta[u_] * v, (beta[u_] * eg[u_]) * kf[u_]], axis=1)
        uw = rhs + _mm_bf16(x[u_], rhs)
        u_ref[rows[cb], cols[h]] = uw[:, :A_DV].astype(u_ref.dtype)
        wq_ref[cb, :c, cols[h]] = uw[:, A_DV:].astype(wq_ref.dtype)


def gdn_intra(proj, fac, fact3):
    t = proj.shape[0]
    c = A_CHUNK
    cb = GDN_INTRA_CHUNKS
    nc = t // c
    return pl.pallas_call(
        _gdn_intra_kernel,
        grid=(nc // cb,),
        in_specs=[pl.BlockSpec((cb * c, A_QK), lambda i: (i, 0)),
                  pl.BlockSpec((cb * c, A_QK), lambda i: (i, 1)),
                  pl.BlockSpec((cb * c, A_V), lambda i: (i, 2)),
                  pl.BlockSpec((cb * c, LANES), lambda i: (i, 0)),
                  pl.BlockSpec((cb, 2 * A_HEADS, c), lambda i: (i, 0, 0))],
        out_specs=[pl.BlockSpec((cb * c, A_V), lambda i: (i, 0)),
                   pl.BlockSpec((cb, 2 * c, A_QK), lambda i: (i, 0, 0)),
                   pl.BlockSpec((cb, A_HEADS, c, c), lambda i: (i, 0, 0, 0)),
                   pl.BlockSpec((cb * c, A_QK), lambda i: (i, 0)),
                   pl.BlockSpec((cb, A_HEADS, A_DV), lambda i: (i, 0, 0))],
        out_shape=[jax.ShapeDtypeStruct((t, A_V), BF16),
                   jax.ShapeDtypeStruct((nc, 2 * c, A_QK), BF16),
                   jax.ShapeDtypeStruct((nc, A_HEADS, c, c), BF16),
                   jax.ShapeDtypeStruct((t, A_QK), BF16),
                   jax.ShapeDtypeStruct((nc, A_HEADS, A_DV), F32)],
        compiler_params=_params("parallel"),
        name="gdn_intra",
    )(proj, proj, proj, fac, fact3)


def _gdn_scan_kernel(u_ref, wq_ref, qk_ref, kd_ref, dec_ref, z_ref, nw_ref, o_ref, s_ref):
    c = A_CHUNK

    @pl.when(pl.program_id(0) == 0)
    def _():
        s_ref[...] = jnp.zeros_like(s_ref)

    nw = nw_ref[...]
    heads = range(A_HEADS)
    cols = [slice(h * A_DK, (h + 1) * A_DK) for h in heads]
    for cb in range(GDN_SCAN_CHUNKS):
        rows = slice(cb * c, (cb + 1) * c)
        s = [s_ref[h] for h in heads]
        sb = [s[h].astype(BF16) for h in heads]
        ws_qs = [jnp.dot(wq_ref[cb, :, cols[h]], sb[h], preferred_element_type=F32) for h in heads]
        v_new = [(u_ref[rows, cols[h]].astype(F32) - ws_qs[h][:c]).astype(BF16) for h in heads]
        o = [ws_qs[h][c:] + jnp.dot(qk_ref[cb, h], v_new[h], preferred_element_type=F32) for h in heads]
        for h in heads:
            s_ref[h] = s[h] * dec_ref[cb, h:h + 1, :] + lax.dot_general(
                kd_ref[rows, cols[h]], v_new[h], TN_DIMS, preferred_element_type=F32)
        for h in heads:
            z = z_ref[rows, cols[h]].astype(F32)
            ms = jnp.mean(o[h] * o[h], axis=-1, keepdims=True)
            o_ref[rows, cols[h]] = (o[h] * lax.rsqrt(ms + RMS_EPS) * nw * _silu(z)).astype(o_ref.dtype)


def gdn_scan(u, wq, qk, kd, dec, proj, norm_w):
    t = u.shape[0]
    c = A_CHUNK
    cb = GDN_SCAN_CHUNKS
    return pl.pallas_call(
        _gdn_scan_kernel,
        grid=(t // (cb * c),),
        in_specs=[pl.BlockSpec((cb * c, A_V), lambda i: (i, 0)),
                  pl.BlockSpec((cb, 2 * c, A_QK), lambda i: (i, 0, 0)),
                  pl.BlockSpec((cb, A_HEADS, c, c), lambda i: (i, 0, 0, 0)),
                  pl.BlockSpec((cb * c, A_QK), lambda i: (i, 0)),
                  pl.BlockSpec((cb, A_HEADS, A_DV), lambda i: (i, 0, 0)),
                  pl.BlockSpec((cb * c, A_V), lambda i: (i, 3)),
                  pl.BlockSpec((1, A_DV), lambda i: (0, 0))],
        out_specs=pl.BlockSpec((cb * c, A_V), lambda i: (i, 0)),
        out_shape=jax.ShapeDtypeStruct((t, A_V), BF16),
        scratch_shapes=[pltpu.VMEM((A_HEADS, A_DK, A_DV), F32)],
        compiler_params=_params("arbitrary"),
        name="gdn_scan",
    )(u, wq, qk, kd, dec, proj, norm_w.reshape(1, A_DV))


LOG2E = math.log2(math.e)
VT_ROWS = 2 * B_DH + 16


def _vt_proj_kernel(x_ref, w_ref, o_ref):
    tk = x_ref.shape[0]
    y = lax.dot_general(w_ref[...], x_ref[...], NT_DIMS, preferred_element_type=F32)
    ones = jnp.ones((VT_ROWS - 2 * B_DH, tk), o_ref.dtype)
    for h in range(B_HEADS):
        o_ref[0, h, :2 * B_DH, :] = y[h * 2 * B_DH:(h + 1) * 2 * B_DH].astype(o_ref.dtype)
        o_ref[0, h, 2 * B_DH:, :] = ones


def v_transposed_proj(xb, w_vt, tile):
    t, k = xb.shape
    n = w_vt.shape[0]
    return pl.pallas_call(
        _vt_proj_kernel,
        grid=(t // tile,),
        in_specs=[pl.BlockSpec((tile, k), lambda i: (i, 0)),
                  pl.BlockSpec((n, k), lambda i: (0, 0))],
        out_specs=pl.BlockSpec((1, B_HEADS, VT_ROWS, tile), lambda i: (i, 0, 0, 0)),
        out_shape=jax.ShapeDtypeStruct((t // tile, B_HEADS, VT_ROWS, tile), BF16),
        compiler_params=_params("parallel"),
        name="v_transposed_proj",
    )(xb, w_vt)


def _attn_kernel(q_ref, k_ref, vt_ref, bias_ref, lam_ref, swb_ref, o_ref, m_ref, acc_ref,
                 sta_ref, mta_ref, stb_ref, mtb_ref, *, lambda_init, tile):
    qi = pl.program_id(1)
    dv = 2 * B_DH
    q = q_ref[...]
    first_map = lax.broadcasted_iota(jnp.int32, (tile, dv), 1) < B_DH
    m_ref[...] = jnp.full_like(m_ref, MASK_VALUE)
    acc_ref[...] = jnp.zeros_like(acc_ref)

    def produce(j, st_ref, mt_ref):
        k = k_ref[pl.ds(pl.multiple_of(j * tile, tile), tile), :]
        zero = jnp.zeros_like(k)
        k2 = jnp.concatenate([jnp.where(first_map, k, zero), jnp.where(first_map, zero, k)], axis=0)
        st = lax.dot_general(k2, q, NT_DIMS, preferred_element_type=F32)
        st_ref[...] = st
        for i in range(2):
            mt_ref[i] = jnp.max(st[i * tile:(i + 1) * tile], axis=0, keepdims=True)

    def consume(j, st_ref, mt_ref, bias):
        vt = vt_ref[j, 0]
        for i in range(2):
            s = st_ref[i * tile:(i + 1) * tile, :]
            m_prev = m_ref[i]
            if bias is None:
                m_new = jnp.maximum(m_prev, mt_ref[i])
            else:
                s = s + bias
                m_new = jnp.maximum(m_prev, jnp.max(s, axis=0, keepdims=True))
            alpha = jnp.exp2(m_prev - m_new)
            p = jnp.exp2(s - m_new).astype(BF16)
            acc_ref[i] = alpha * acc_ref[i] + jnp.dot(vt, p, preferred_element_type=F32)
            m_ref[i] = m_new

    n_far = jnp.maximum(qi - 1, 0)
    n_pairs = n_far // 2
    produce(0, sta_ref, mta_ref)

    def pair_body(jj, carry):
        j = 2 * jj
        produce(j + 1, stb_ref, mtb_ref)
        consume(j, sta_ref, mta_ref, None)
        produce(j + 2, sta_ref, mta_ref)
        consume(j + 1, stb_ref, mtb_ref, None)
        return carry

    lax.fori_loop(0, n_pairs, pair_body, 0)

    @pl.when(n_far > 2 * n_pairs)
    def _():
        consume(n_far - 1, sta_ref, mta_ref, None)
        produce(n_far, sta_ref, mta_ref)

    @pl.when(qi >= 1)
    def _():
        produce(qi, stb_ref, mtb_ref)
        consume(qi - 1, sta_ref, mta_ref, bias_ref[0, 1])
        consume(qi, stb_ref, mtb_ref, bias_ref[0, 0])

    @pl.when(qi == 0)
    def _():
        consume(0, sta_ref, mta_ref, bias_ref[0, 0])

    lam = lam_ref[...]
    lam_val = (jnp.exp(jnp.sum(lam[0:1] * lam[1:2], axis=-1, keepdims=True))
               - jnp.exp(jnp.sum(lam[2:3] * lam[3:4], axis=-1, keepdims=True)) + lambda_init)
    a0 = acc_ref[0]
    a1 = acc_ref[1]
    ot = a0[:dv] / a0[dv:dv + 1] - lam_val * (a1[:dv] / a1[dv:dv + 1])
    ms = jnp.mean(ot * ot, axis=0, keepdims=True)
    ot = ot * lax.rsqrt(ms + RMS_EPS) * swb_ref[...] * (1.0 - lambda_init)
    o_ref[...] = ot.T.astype(o_ref.dtype)


def diff_attention(qb, kb, vt, bias_tiles, lam, subln_w, lambda_init, *, tile):
    t = qb.shape[0]
    nb = t // tile
    dv = 2 * B_DH
    swb = jnp.broadcast_to(subln_w.reshape(dv, 1), (dv, tile))
    return pl.pallas_call(
        functools.partial(_attn_kernel, lambda_init=lambda_init, tile=tile),
        grid=(B_HEADS, nb),
        in_specs=[pl.BlockSpec((tile, dv), lambda h, i: (i, h)),
                  pl.BlockSpec((t, dv), lambda h, i: (0, h)),
                  pl.BlockSpec((nb, 1, VT_ROWS, tile), lambda h, i: (0, h, 0, 0)),
                  pl.BlockSpec((1, 2, tile, tile), lambda h, i: (h, 0, 0, 0)),
                  pl.BlockSpec((4, B_DH), lambda h, i: (0, 0)),
                  pl.BlockSpec((dv, tile), lambda h, i: (0, 0))],
        out_specs=pl.BlockSpec((tile, dv), lambda h, i: (i, h)),
        out_shape=jax.ShapeDtypeStruct((t, B_HEADS * dv), BF16),
        scratch_shapes=[pltpu.VMEM((2, 1, tile), F32),
                        pltpu.VMEM((2, VT_ROWS, tile), F32),
                        pltpu.VMEM((2 * tile, tile), F32),
                        pltpu.VMEM((2, 1, tile), F32),
                        pltpu.VMEM((2 * tile, tile), F32),
                        pltpu.VMEM((2, 1, tile), F32)],
        compiler_params=_params("parallel", "arbitrary"),
        name="diff_attention",
    )(qb, kb, vt, bias_tiles, lam, swb)


def _t5_bucket(n):
    max_exact = REL_BUCKETS // 2
    nf = jnp.maximum(n, 1).astype(F32)
    large = max_exact + (jnp.log(nf / max_exact) / math.log(REL_MAX_DIST / max_exact)
                         * (REL_BUCKETS - max_exact)).astype(jnp.int32)
    large = jnp.minimum(large, REL_BUCKETS - 1)
    return jnp.where(n < max_exact, n, large)


def attention_bias_tiles(rel_bias, tile):
    assert tile >= REL_MAX_DIST
    n = jnp.arange(2 * tile)
    far = rel_bias[REL_BUCKETS - 1]
    db = ((rel_bias[_t5_bucket(n)] - far[None, :]) * LOG2E).T.astype(F32)
    nh = db.shape[0]
    full = jnp.concatenate([jnp.full((nh, tile - 1), MASK_VALUE, F32), db], axis=1)

    def toeplitz(vec):
        w = jnp.concatenate([vec, jnp.zeros((nh, 1), F32)], axis=1)
        flat = jnp.tile(w, (1, tile))[:, :tile * (2 * tile - 1)]
        return flat.reshape(nh, tile, 2 * tile - 1)[:, :, tile - 1:]

    diag = toeplitz(full[:, :2 * tile - 1])
    near = toeplitz(full[:, tile:3 * tile - 1])
    return jnp.stack([diag, near], axis=1)


def _route(logits):
    mx = jnp.max(logits, axis=0, keepdims=True)
    ex = jnp.exp(logits - mx)
    probs = ex / jnp.sum(ex, axis=0, keepdims=True)
    row = [probs[e:e + 1, :] for e in range(N_EXPERTS)]
    epg = EXPERTS_PER_GROUP
    best_sum = None
    best_group = None
    for g in range(N_GROUPS):
        vals = row[g * epg:(g + 1) * epg]
        top2 = None
        for a in range(epg):
            for b in range(a + 1, epg):
                pair = vals[a] + vals[b]
                top2 = pair if top2 is None else jnp.maximum(top2, pair)
        if g == 0:
            best_sum = top2
            best_group = jnp.zeros_like(top2, dtype=jnp.int32)
        else:
            upd = top2 > best_sum
            best_sum = jnp.where(upd, top2, best_sum)
            best_group = jnp.where(upd, g, best_group)
    gp = []
    for j in range(epg):
        sel = row[j]
        for g in range(1, N_GROUPS):
            sel = jnp.where(best_group == g, row[g * epg + j], sel)
        gp.append(sel)
    v1 = gp[0]
    i1 = jnp.zeros_like(best_group)
    for j in range(1, epg):
        upd = gp[j] > v1
        v1 = jnp.where(upd, gp[j], v1)
        i1 = jnp.where(upd, j, i1)
    v2 = jnp.full_like(v1, -1.0)
    i2 = jnp.zeros_like(best_group)
    for j in range(epg):
        upd = (i1 != j) & (gp[j] > v2)
        v2 = jnp.where(upd, gp[j], v2)
        i2 = jnp.where(upd, j, i2)
    denom = v1 + v2
    e1 = best_group * epg + i1
    e2 = best_group * epg + i2
    eidx = lax.broadcasted_iota(jnp.int32, logits.shape, 0)
    gates = jnp.where(eidx == e1, v1 / denom, 0.0) + jnp.where(eidx == e2, v2 / denom, 0.0)
    return gates, best_group


MOE_BLOCK = 1024
MOE_CHUNK = 256
MOE_MAX_CHUNKS = MOE_BLOCK // MOE_CHUNK
MOE_GATE_ROWS = 16
MOE_VMEM_LIMIT = 56 * 1024 * 1024


def _moe_kernel(xt_ref, rwt_ref, rb_ref, tri_ref, wgu_ref, wd_ref, lw_ref, lb_ref, o_ref, ob_ref,
                xg_ref, gates_ref, grpc_ref, keyc_ref, grpr_ref, keyr_ref, cnt_ref,
                xs_ref, gw_ref, ych_ref, yacc_ref):
    e = pl.program_id(1)
    g = e // EXPERTS_PER_GROUP
    j = e % EXPERTS_PER_GROUP
    d, bt = xt_ref.shape
    n = MOE_CHUNK
    gr = MOE_GATE_ROWS

    @pl.when(e == 0)
    def _route_block():
        xt = xt_ref[...]
        xg_ref[:d, :] = xt.astype(BF16)
        logits = jnp.dot(rwt_ref[...], xt, preferred_element_type=F32, precision=HIGHEST) + rb_ref[...]
        gates, best_group = _route(logits)
        gidx = lax.broadcasted_iota(jnp.int32, (SUBLANES, bt), 0)
        onehot = jnp.where(gidx == best_group, 1.0, 0.0)
        rank = jnp.dot(onehot.astype(BF16), tri_ref[...], preferred_element_type=F32)
        key = jnp.sum(onehot * rank, axis=0, keepdims=True)
        grp_f = best_group.astype(F32)
        grpr_ref[...] = grp_f
        keyr_ref[...] = key
        stack = jnp.concatenate([grp_f, key, jnp.zeros((LANES - 2, bt), F32)], axis=0)
        col = stack.T
        grpc_ref[...] = col[:, 0:1]
        keyc_ref[...] = col[:, 1:2]
        for gg in range(N_GROUPS):
            cnt_ref[gg] = jnp.sum(onehot[gg:gg + 1, :]).astype(jnp.int32)
            gates_ref[gg] = jnp.concatenate(
                [gates[gg * EXPERTS_PER_GROUP:(gg + 1) * EXPERTS_PER_GROUP],
                 jnp.zeros((gr - EXPERTS_PER_GROUP, bt), F32)], axis=0)
        yacc_ref[...] = jnp.zeros_like(yacc_ref)

    count = cnt_ref[g]
    g_f = g.astype(F32)

    @pl.when(j == 0)
    def _gather_group():
        gt = gates_ref[g]
        hi = gt.astype(BF16)
        xg_ref[d:d + gr, :] = hi
        xg_ref[d + gr:d + 2 * gr, :] = (gt - hi.astype(F32)).astype(BF16)
        key_mine = jnp.where(grpc_ref[...] == g_f, keyc_ref[...], -1.0)
        lane = lax.broadcasted_iota(jnp.int32, (bt, n), 1).astype(F32)
        for c in range(MOE_MAX_CHUNKS):
            @pl.when(c * n < count)
            def _():
                selt = jnp.where(key_mine - float(c * n) == lane, 1.0, 0.0).astype(BF16)
                gx = jnp.dot(xg_ref[...], selt, preferred_element_type=F32)
                xs_ref[c] = gx[:d].astype(BF16)
                gw = gx[d:d + gr] + gx[d + gr:d + 2 * gr]
                for jj in range(EXPERTS_PER_GROUP):
                    gw_ref[c, jj] = jnp.broadcast_to(gw[jj:jj + 1, :], (SUBLANES, n))
                ych_ref[c] = jnp.zeros((d, n), F32)

    for c in range(MOE_MAX_CHUNKS):
        @pl.when(c * n < count)
        def _():
            xs = xs_ref[c]
            gw = gw_ref[c, j][0:1, :]
            half = D_EXPERT // 2
            gu = [(jnp.dot(wgu_ref[0, s * half:(s + 1) * half, :], xs, preferred_element_type=F32),
                   jnp.dot(wgu_ref[0, D_EXPERT + s * half:D_EXPERT + (s + 1) * half, :], xs,
                           preferred_element_type=F32)) for s in range(2)]
            y = ych_ref[c]
            for s in range(2):
                hid = (_silu(gu[s][0]) * gu[s][1] * gw).astype(BF16)
                y = y + jnp.dot(wd_ref[0, :, s * half:(s + 1) * half], hid, preferred_element_type=F32)
            ych_ref[c] = y

    @pl.when(j == EXPERTS_PER_GROUP - 1)
    def _scatter_group():
        key_mine = jnp.where(grpr_ref[...] == g_f, keyr_ref[...], -1.0)
        sub = lax.broadcasted_iota(jnp.int32, (n, bt), 0).astype(F32)
        for c in range(MOE_MAX_CHUNKS):
            @pl.when(c * n < count)
            def _():
                sel = jnp.where(key_mine - float(c * n) == sub, 1.0, 0.0).astype(BF16)
                yacc_ref[...] += jnp.dot(ych_ref[c].astype(BF16), sel, preferred_element_type=F32)

    @pl.when(e == N_EXPERTS - 1)
    def _finish():
        for c in range(bt // n):
            cols = slice(c * n, (c + 1) * n)
            z = DN_ALPHA * xt_ref[:, cols] + yacc_ref[:, cols]
            y = _layer_norm(z.T, lw_ref[...], lb_ref[...])
            o_ref[cols, :] = y
            ob_ref[cols, :] = y.astype(BF16)


def moe_residual_ln(xt, rwt, rb, tri, wgu_t, wd_t, lw, lb):
    d, t = xt.shape
    bt = MOE_BLOCK
    n = MOE_CHUNK
    assert t % bt == 0
    return pl.pallas_call(
        _moe_kernel,
        grid=(t // bt, N_EXPERTS),
        in_specs=[pl.BlockSpec((d, bt), lambda i, e: (0, i)),
                  pl.BlockSpec((N_EXPERTS, d), lambda i, e: (0, 0)),
                  pl.BlockSpec((N_EXPERTS, 1), lambda i, e: (0, 0)),
                  pl.BlockSpec((bt, bt), lambda i, e: (0, 0)),
                  pl.BlockSpec((1, 2 * D_EXPERT, d), lambda i, e: (e, 0, 0)),
                  pl.BlockSpec((1, d, D_EXPERT), lambda i, e: (e, 0, 0)),
                  pl.BlockSpec((1, d), lambda i, e: (0, 0)),
                  pl.BlockSpec((1, d), lambda i, e: (0, 0))],
        out_specs=[pl.BlockSpec((bt, d), lambda i, e: (i, 0)),
                   pl.BlockSpec((bt, d), lambda i, e: (i, 0))],
        out_shape=[jax.ShapeDtypeStruct((t, d), F32),
                   jax.ShapeDtypeStruct((t, d), BF16)],
        scratch_shapes=[pltpu.VMEM((d + 2 * MOE_GATE_ROWS, bt), BF16),
                        pltpu.VMEM((N_GROUPS, MOE_GATE_ROWS, bt), F32),
                        pltpu.VMEM((bt, 1), F32), pltpu.VMEM((bt, 1), F32),
                        pltpu.VMEM((1, bt), F32), pltpu.VMEM((1, bt), F32),
                        pltpu.SMEM((N_GROUPS,), jnp.int32),
                        pltpu.VMEM((MOE_MAX_CHUNKS, d, n), BF16),
                        pltpu.VMEM((MOE_MAX_CHUNKS, EXPERTS_PER_GROUP, SUBLANES, n), F32),
                        pltpu.VMEM((MOE_MAX_CHUNKS, d, n), F32),
                        pltpu.VMEM((d, bt), F32)],
        compiler_params=pltpu.CompilerParams(dimension_semantics=("parallel", "arbitrary"),
                                             vmem_limit_bytes=MOE_VMEM_LIMIT),
        name="moe_residual_ln",
    )(xt, rwt, rb, tri, wgu_t, wd_t, lw.reshape(1, d), lb.reshape(1, d))


def _lambda_init_for(layer):
    return 0.8 - 0.6 * math.exp(-0.3 * layer)


def kernel(x, a_w_in, a_conv, a_A_log, a_dt_bias, a_norm_w, a_w_out, b_w_q, b_w_kv, b_lambda, b_subln_w, b_w_out, rel_bias, router_w, router_b, moe_w_gate, moe_w_up, moe_w_down, ln_w, ln_b):
    bsz, t, d = x.shape
    assert bsz == 1 and d == D_MODEL and t % A_CHUNK == 0
    xf = x.reshape(t, d).astype(F32)
    xb = xf.astype(BF16)

    gate_tile = min(512, t)
    tri = jnp.asarray(np.kron(np.eye(gate_tile // A_CHUNK, dtype=np.float32),
                              np.tril(np.ones((A_CHUNK, A_CHUNK), np.float32))))
    moe_tri = jnp.asarray(np.triu(np.ones((MOE_BLOCK, MOE_BLOCK), np.float32), k=1), dtype=BF16)
    rwt = router_w.T.astype(F32)
    rb = router_b.reshape(N_EXPERTS, 1).astype(F32)
    attn_tile = min(512, t)
    bias_tiles = attention_bias_tiles(rel_bias.astype(F32), attn_tile)

    kb = None
    vt = None
    for l in range(DEPTH):
        if l < N_A_LAYERS:
            w_in = a_w_in[l]
            w_main = w_in[:, :A_CONV_CH + A_V].astype(BF16)
            w_ab = jnp.pad(w_in[:, A_CONV_CH + A_V:], ((0, 0), (0, LANES - 2 * A_HEADS))).astype(BF16)
            conv_pad = jnp.pad(a_conv[l].astype(F32), ((0, 0), (0, A_V)))
            par = jnp.zeros((SUBLANES, LANES), F32)
            par = par.at[0, :A_HEADS].set(a_A_log[l].astype(F32))
            par = par.at[1, :A_HEADS].set(a_dt_bias[l].astype(F32))
            proj = gdn_inproj(xb, w_main, conv_pad)
            fac, fact = gdn_gates(xb, w_ab, par, tri)
            fact3 = fact.reshape(2 * A_HEADS, t // A_CHUNK, A_CHUNK).transpose(1, 0, 2)
            u, wq, qk, kd, dec = gdn_intra(proj, fac, fact3)
            mixed = gdn_scan(u, wq, qk, kd, dec, proj, a_norm_w[l].astype(F32))
            w_out = a_w_out[l].astype(BF16)
        else:
            j = l - N_A_LAYERS
            if kb is None:
                kb = matmul_bf16(xb, b_w_kv[:, :B_QK].astype(BF16))
                vt = v_transposed_proj(xb, b_w_kv[:, B_QK:].T.astype(BF16), attn_tile)
            qb = matmul_bf16(xb, b_w_q[j].astype(BF16), scale=B_DH ** -0.5 * LOG2E)
            mixed = diff_attention(qb, kb, vt, bias_tiles, b_lambda[j].astype(F32),
                                   b_subln_w[j].astype(F32), _lambda_init_for(l), tile=attn_tile)
            w_out = b_w_out[j].astype(BF16)
        xt = matmul_residual_ln_t(mixed, w_out, xf, ln_w[l, 0].astype(F32), ln_b[l, 0].astype(F32))
        wgu_t = jnp.concatenate([moe_w_gate[l].transpose(0, 2, 1), moe_w_up[l].transpose(0, 2, 1)],
                                axis=1).astype(BF16)
        wd_t = moe_w_down[l].transpose(0, 2, 1).astype(BF16)
        xf, xb = moe_residual_ln(xt, rwt, rb, moe_tri, wgu_t, wd_t,
                                 ln_w[l, 1].astype(F32), ln_b[l, 1].astype(F32))
    return xf.reshape(bsz, t, d).astype(x.dtype)
```

```python
import functools
import math

import jax
import jax.numpy as jnp
import numpy as np
from jax import lax
from jax.experimental import pallas as pl
from jax.experimental.pallas import tpu as pltpu

F32 = jnp.float32
BF16 = jnp.bfloat16

D_MODEL = 1024
DEPTH = 4
N_A_LAYERS = DEPTH // 2
A_HEADS = 8
A_DK = 128
A_DV = 128
A_QK = A_HEADS * A_DK
A_V = A_HEADS * A_DV
A_CONV = 4
A_CONV_CH = 2 * A_QK + A_V
A_CHUNK = 64
B_HEADS = 8
B_DH = 64
B_QK = B_HEADS * 2 * B_DH
REL_BUCKETS = 32
REL_MAX_DIST = 128
N_EXPERTS = 16
N_GROUPS = 4
EXPERTS_PER_GROUP = N_EXPERTS // N_GROUPS
D_EXPERT = 512
DN_ALPHA = (2 * DEPTH) ** 0.25
LN_EPS = 1e-5
RMS_EPS = 1e-6

LANES = 128
SUBLANES = 8
VMEM_LIMIT = 48 * 1024 * 1024
MASK_VALUE = -1e30

HIGHEST = lax.Precision.HIGHEST
NT_DIMS = (((1,), (1,)), ((), ()))
TN_DIMS = (((0,), (0,)), ((), ()))


def _params(*sem):
    return pltpu.CompilerParams(dimension_semantics=sem, vmem_limit_bytes=VMEM_LIMIT)


def _sigmoid(x):
    return 1.0 / (1.0 + jnp.exp(-x))


def _silu(x):
    return x * _sigmoid(x)


def _layer_norm(v, w, b):
    mu = jnp.mean(v, axis=-1, keepdims=True)
    c = v - mu
    var = jnp.mean(c * c, axis=-1, keepdims=True)
    return c * lax.rsqrt(var + LN_EPS) * w + b


def _mm_kernel(a_ref, w_ref, o_ref, *, scale):
    y = jnp.dot(a_ref[...], w_ref[...], preferred_element_type=F32)
    if scale != 1.0:
        y = y * scale
    o_ref[...] = y.astype(o_ref.dtype)


def matmul_bf16(a, w, *, scale=1.0, tm=512, tn=1024):
    m, k = a.shape
    n = w.shape[1]
    tm = min(tm, m)
    return pl.pallas_call(
        functools.partial(_mm_kernel, scale=scale),
        grid=(n // tn, m // tm),
        in_specs=[pl.BlockSpec((tm, k), lambda j, i: (i, 0)),
                  pl.BlockSpec((k, tn), lambda j, i: (0, j))],
        out_specs=pl.BlockSpec((tm, tn), lambda j, i: (i, j)),
        out_shape=jax.ShapeDtypeStruct((m, n), BF16),
        compiler_params=_params("parallel", "arbitrary"),
        name="matmul_bf16",
    )(a, w)


def _mm_ln_kernel(a_ref, w_ref, x_ref, lw_ref, lb_ref, ot_ref):
    h = jnp.dot(a_ref[...], w_ref[...], preferred_element_type=F32)
    y = _layer_norm(DN_ALPHA * x_ref[...] + h, lw_ref[...], lb_ref[...])
    ot_ref[...] = y.T


def matmul_residual_ln_t(a, w, x, lw, lb, *, tm=512):
    m, k = a.shape
    n = w.shape[1]
    tm = min(tm, m)
    return pl.pallas_call(
        _mm_ln_kernel,
        grid=(m // tm,),
        in_specs=[pl.BlockSpec((tm, k), lambda i: (i, 0)),
                  pl.BlockSpec((k, n), lambda i: (0, 0)),
                  pl.BlockSpec((tm, n), lambda i: (i, 0)),
                  pl.BlockSpec((1, n), lambda i: (0, 0)),
                  pl.BlockSpec((1, n), lambda i: (0, 0))],
        out_specs=pl.BlockSpec((n, tm), lambda i: (0, i)),
        out_shape=jax.ShapeDtypeStruct((n, m), F32),
        compiler_params=_params("parallel"),
        name="matmul_residual_ln",
    )(a, w, x, lw.reshape(1, n), lb.reshape(1, n))


def _gdn_inproj_kernel(x_ref, w_ref, cw_ref, o_ref, halo_ref):
    n = pl.program_id(0)
    m = pl.program_id(1)
    tm, tn = o_ref.shape
    parts = 2
    pr = tm // parts

    @pl.when(m == 0)
    def _():
        halo_ref[...] = jnp.zeros_like(halo_ref)

    def project():
        return [jnp.dot(x_ref[p * pr:(p + 1) * pr, :], w_ref[...], preferred_element_type=F32)
                for p in range(parts)]

    def conv_silu(ys):
        cw = cw_ref[...]
        rows = lax.broadcasted_iota(jnp.int32, (SUBLANES, tn), 0)
        out = []
        halo = halo_ref[...]
        for y in ys:
            acc = y * cw[A_CONV - 1:A_CONV]
            for shift in range(1, A_CONV):
                sh = pltpu.roll(y, shift, 0)
                top = jnp.where(rows < shift, pltpu.roll(halo, shift, 0), sh[:SUBLANES])
                sh = jnp.concatenate([top, sh[SUBLANES:]], axis=0)
                acc = acc + sh * cw[A_CONV - 1 - shift:A_CONV - shift]
            halo = y[pr - SUBLANES:]
            out.append(_silu(acc))
        halo_ref[...] = halo
        return out

    @pl.when(n == 3)
    def _():
        for p, y in enumerate(project()):
            o_ref[p * pr:(p + 1) * pr, :] = y.astype(o_ref.dtype)

    @pl.when(n == 2)
    def _():
        for p, a in enumerate(conv_silu(project())):
            o_ref[p * pr:(p + 1) * pr, :] = a.astype(o_ref.dtype)

    @pl.when(n < 2)
    def _():
        post = jnp.where(n == 0, A_DK ** -0.5, 1.0).astype(F32)
        for p, a in enumerate(conv_silu(project())):
            for h in range(tn // A_DK):
                seg = a[:, h * A_DK:(h + 1) * A_DK]
                ss = jnp.sum(seg * seg, axis=-1, keepdims=True)
                o_ref[p * pr:(p + 1) * pr, h * A_DK:(h + 1) * A_DK] = (
                    seg * (lax.rsqrt(ss + RMS_EPS) * post)).astype(o_ref.dtype)


def gdn_inproj(xb, w_main, conv_w_pad, *, tm=512):
    t, k = xb.shape
    n = w_main.shape[1]
    tn = A_QK
    tm = min(tm, t)
    return pl.pallas_call(
        _gdn_inproj_kernel,
        grid=(n // tn, t // tm),
        in_specs=[pl.BlockSpec((tm, k), lambda j, i: (i, 0)),
                  pl.BlockSpec((k, tn), lambda j, i: (0, j)),
                  pl.BlockSpec((A_CONV, tn), lambda j, i: (0, j))],
        out_specs=pl.BlockSpec((tm, tn), lambda j, i: (i, j)),
        out_shape=jax.ShapeDtypeStruct((t, n), BF16),
        scratch_shapes=[pltpu.VMEM((SUBLANES, tn), F32)],
        compiler_params=_params("arbitrary", "arbitrary"),
        name="gdn_inproj",
    )(xb, w_main, conv_w_pad)


def _gdn_gates_kernel(x_ref, w_ref, par_ref, tri_ref, fac_ref, fact_ref):
    ab = jnp.dot(x_ref[...], w_ref[...], preferred_element_type=F32)
    lane = lax.broadcasted_iota(jnp.int32, ab.shape, 1)
    neg_a = -jnp.exp(par_ref[0:1, :])
    z = ab + par_ref[1:2, :]
    softplus = jnp.maximum(z, 0.0) + jnp.log1p(jnp.exp(-jnp.abs(z)))
    g = jnp.where(lane < A_HEADS, neg_a * softplus, 0.0)
    beta = _sigmoid(ab)
    gc = jnp.dot(tri_ref[...], g, preferred_element_type=F32, precision=HIGHEST)
    fac = jnp.where(lane < A_HEADS, gc, jnp.where(lane < 2 * A_HEADS, beta, 0.0))
    fac_ref[...] = fac
    fact_ref[...] = fac.T[:2 * A_HEADS, :]


def gdn_gates(xb, w_ab_pad, par, tri, *, tm=512):
    t, k = xb.shape
    tm = min(tm, t)
    return pl.pallas_call(
        _gdn_gates_kernel,
        grid=(t // tm,),
        in_specs=[pl.BlockSpec((tm, k), lambda i: (i, 0)),
                  pl.BlockSpec((k, LANES), lambda i: (0, 0)),
                  pl.BlockSpec((SUBLANES, LANES), lambda i: (0, 0)),
                  pl.BlockSpec((tm, tm), lambda i: (0, 0))],
        out_specs=[pl.BlockSpec((tm, LANES), lambda i: (i, 0)),
                   pl.BlockSpec((2 * A_HEADS, tm), lambda i: (0, i))],
        out_shape=[jax.ShapeDtypeStruct((t, LANES), F32),
                   jax.ShapeDtypeStruct((2 * A_HEADS, t), F32)],
        compiler_params=_params("parallel"),
        name="gdn_gates",
    )(xb, w_ab_pad, par, tri)


GDN_INTRA_CHUNKS = 2
GDN_SCAN_CHUNKS = 8


def _mm_bf16(a, b):
    return jnp.dot(a.astype(BF16), b.astype(BF16), preferred_element_type=F32)


def _gdn_intra_kernel(q_ref, k_ref, v_ref, fac_ref, fact_ref, u_ref, wq_ref, qk_ref, kd_ref, dec_ref):
    c = A_CHUNK
    ii = lax.broadcasted_iota(jnp.int32, (c, c), 0)
    jj = lax.broadcasted_iota(jnp.int32, (c, c), 1)
    incl = ii >= jj
    strict = ii > jj
    units = [(cb, h) for cb in range(GDN_INTRA_CHUNKS) for h in range(A_HEADS)]
    rows = {cb: slice(cb * c, (cb + 1) * c) for cb in range(GDN_INTRA_CHUNKS)}
    cols = {h: slice(h * A_DK, (h + 1) * A_DK) for h in range(A_HEADS)}

    q, k, kf, beta, eg, decay = {}, {}, {}, {}, {}, {}
    for cb, h in units:
        fac = fac_ref[rows[cb], :]
        gc_col = fac[:, h:h + 1]
        gc_row = fact_ref[cb][h:h + 1, :]
        g_last = gc_col[c - 1:c, :]
        beta[cb, h] = fac[:, A_HEADS + h:A_HEADS + h + 1]
        eg[cb, h] = jnp.exp(gc_col)
        decay[cb, h] = jnp.exp(jnp.where(incl, gc_col - gc_row, MASK_VALUE))
        q[cb, h] = q_ref[rows[cb], cols[h]]
        k[cb, h] = k_ref[rows[cb], cols[h]]
        kf[cb, h] = k[cb, h].astype(F32)
        kd_ref[rows[cb], cols[h]] = (kf[cb, h] * jnp.exp(g_last - gc_col)).astype(kd_ref.dtype)
        dec_ref[cb, h:h + 1, :] = jnp.broadcast_to(jnp.exp(g_last), (1, A_DV))
        wq_ref[cb, c:, cols[h]] = (q[cb, h].astype(F32) * eg[cb, h]).astype(wq_ref.dtype)

    mmat, x, p = {}, {}, {}
    for u_ in units:
        kk = lax.dot_general(k[u_], k[u_], NT_DIMS, preferred_element_type=F32)
        mmat[u_] = jnp.where(strict, beta[u_] * kk * decay[u_], 0.0)
    for u_ in units:
        cb, h = u_
        qk = lax.dot_general(q[u_], k[u_], NT_DIMS, preferred_element_type=F32) * decay[u_]
        qk_ref[cb, h] = qk.astype(qk_ref.dtype)
    for u_ in units:
        x[u_] = -mmat[u_]
        p[u_] = _mm_bf16(mmat[u_], mmat[u_])
    for level in range(5):
        for u_ in units:
            x[u_] = x[u_] + p[u_] + _mm_bf16(x[u_], p[u_])
        if level < 4:
            for u_ in units:
                p[u_] = _mm_bf16(p[u_], p[u_])
    for u_ in units:
        cb, h = u_
        v = v_ref[rows[cb], cols[h]].astype(F32)
        rhs = jnp.concatenate([beta[u_] * v, (beta[u_] * eg[u_]) * kf[u_]], axis=1)
        uw = rhs + _mm_bf16(x[u_], rhs)
        u_ref[rows[cb], cols[h]] = uw[:, :A_DV].astype(u_ref.dtype)
        wq_ref[cb, :c, cols[h]] = uw[:, A_DV:].astype(wq_ref.dtype)


def gdn_intra(proj, fac, fact3):
    t = proj.shape[0]
    c = A_CHUNK
    cb = GDN_INTRA_CHUNKS
    nc = t // c
    return pl.pallas_call(
        _gdn_intra_kernel,
        grid=(nc // cb,),
        in_specs=[pl.BlockSpec((cb * c, A_QK), lambda i: (i, 0)),
                  pl.BlockSpec((cb * c, A_QK), lambda i: (i, 1)),
                  pl.BlockSpec((cb * c, A_V), lambda i: (i, 2)),
                  pl.BlockSpec((cb * c, LANES), lambda i: (i, 0)),
                  pl.BlockSpec((cb, 2 * A_HEADS, c), lambda i: (i, 0, 0))],
        out_specs=[pl.BlockSpec((cb * c, A_V), lambda i: (i, 0)),
                   pl.BlockSpec((cb, 2 * c, A_QK), lambda i: (i, 0, 0)),
                   pl.BlockSpec((cb, A_HEADS, c, c), lambda i: (i, 0, 0, 0)),
                   pl.BlockSpec((cb * c, A_QK), lambda i: (i, 0)),
                   pl.BlockSpec((cb, A_HEADS, A_DV), lambda i: (i, 0, 0))],
        out_shape=[jax.ShapeDtypeStruct((t, A_V), BF16),
                   jax.ShapeDtypeStruct((nc, 2 * c, A_QK), BF16),
                   jax.ShapeDtypeStruct((nc, A_HEADS, c, c), BF16),
                   jax.ShapeDtypeStruct((t, A_QK), BF16),
                   jax.ShapeDtypeStruct((nc, A_HEADS, A_DV), F32)],
        compiler_params=_params("parallel"),
        name="gdn_intra",
    )(proj, proj, proj, fac, fact3)


def _gdn_scan_kernel(u_ref, wq_ref, qk_ref, kd_ref, dec_ref, z_ref, nw_ref, o_ref, s_ref):
    c = A_CHUNK

    @pl.when(pl.program_id(0) == 0)
    def _():
        s_ref[...] = jnp.zeros_like(s_ref)

    nw = nw_ref[...]
    heads = range(A_HEADS)
    cols = [slice(h * A_DK, (h + 1) * A_DK) for h in heads]
    for cb in range(GDN_SCAN_CHUNKS):
        rows = slice(cb * c, (cb + 1) * c)
        s = [s_ref[h] for h in heads]
        sb = [s[h].astype(BF16) for h in heads]
        ws_qs = [jnp.dot(wq_ref[cb, :, cols[h]], sb[h], preferred_element_type=F32) for h in heads]
        v_new = [(u_ref[rows, cols[h]].astype(F32) - ws_qs[h][:c]).astype(BF16) for h in heads]
        o = [ws_qs[h][c:] + jnp.dot(qk_ref[cb, h], v_new[h], preferred_element_type=F32) for h in heads]
        for h in heads:
            s_ref[h] = s[h] * dec_ref[cb, h:h + 1, :] + lax.dot_general(
                kd_ref[rows, cols[h]], v_new[h], TN_DIMS, preferred_element_type=F32)
        for h in heads:
            z = z_ref[rows, cols[h]].astype(F32)
            ms = jnp.mean(o[h] * o[h], axis=-1, keepdims=True)
            o_ref[rows, cols[h]] = (o[h] * lax.rsqrt(ms + RMS_EPS) * nw * _silu(z)).astype(o_ref.dtype)


def gdn_scan(u, wq, qk, kd, dec, proj, norm_w):
    t = u.shape[0]
    c = A_CHUNK
    cb = GDN_SCAN_CHUNKS
    return pl.pallas_call(
        _gdn_scan_kernel,
        grid=(t // (cb * c),),
        in_specs=[pl.BlockSpec((cb * c, A_V), lambda i: (i, 0)),
                  pl.BlockSpec((cb, 2 * c, A_QK), lambda i: (i, 0, 0)),
                  pl.BlockSpec((cb, A_HEADS, c, c), lambda i: (i, 0, 0, 0)),
                  pl.BlockSpec((cb * c, A_QK), lambda i: (i, 0)),
                  pl.BlockSpec((cb, A_HEADS, A_DV), lambda i: (i, 0, 0)),
                  pl.BlockSpec((cb * c, A_V), lambda i: (i, 3)),
                  pl.BlockSpec((1, A_DV), lambda i: (0, 0))],
        out_specs=pl.BlockSpec((cb * c, A_V), lambda i: (i, 0)),
        out_shape=jax.ShapeDtypeStruct((t, A_V), BF16),
        scratch_shapes=[pltpu.VMEM((A_HEADS, A_DK, A_DV), F32)],
        compiler_params=_params("arbitrary"),
        name="gdn_scan",
    )(u, wq, qk, kd, dec, proj, norm_w.reshape(1, A_DV))


LOG2E = math.log2(math.e)
VT_ROWS = 2 * B_DH + 16


def _vt_proj_kernel(x_ref, w_ref, o_ref):
    tk = x_ref.shape[0]
    y = lax.dot_general(w_ref[...], x_ref[...], NT_DIMS, preferred_element_type=F32)
    ones = jnp.ones((VT_ROWS - 2 * B_DH, tk), o_ref.dtype)
    for h in range(B_HEADS):
        o_ref[0, h, :2 * B_DH, :] = y[h * 2 * B_DH:(h + 1) * 2 * B_DH].astype(o_ref.dtype)
        o_ref[0, h, 2 * B_DH:, :] = ones


def v_transposed_proj(xb, w_vt, tile):
    t, k = xb.shape
    n = w_vt.shape[0]
    return pl.pallas_call(
        _vt_proj_kernel,
        grid=(t // tile,),
        in_specs=[pl.BlockSpec((tile, k), lambda i: (i, 0)),
                  pl.BlockSpec((n, k), lambda i: (0, 0))],
        out_specs=pl.BlockSpec((1, B_HEADS, VT_ROWS, tile), lambda i: (i, 0, 0, 0)),
        out_shape=jax.ShapeDtypeStruct((t // tile, B_HEADS, VT_ROWS, tile), BF16),
        compiler_params=_params("parallel"),
        name="v_transposed_proj",
    )(xb, w_vt)


ATTN_HEADS_PER_STEP = 2
BIAS_BLOCK = REL_MAX_DIST


def _attn_kernel(q_ref, k_ref, vt_ref, tab_ref, lam_ref, swb_ref, o_ref, m_ref, acc_ref,
                 st_ref, mt_ref, bias_ref, *, lambda_init, tile):
    qi = pl.program_id(1)
    dv = 2 * B_DH
    nh = ATTN_HEADS_PER_STEP
    first_map = lax.broadcasted_iota(jnp.int32, (tile, dv), 1) < B_DH

    @pl.when(qi == 0)
    def _():
        nblk = tile // BIAS_BLOCK
        zeros = jnp.zeros((BIAS_BLOCK, BIAS_BLOCK), F32)
        masked = jnp.full((BIAS_BLOCK, BIAS_BLOCK), MASK_VALUE, F32)
        for h in range(nh):
            for ja in range(nblk):
                for ib in range(nblk):
                    rows = slice(ja * BIAS_BLOCK, (ja + 1) * BIAS_BLOCK)
                    cols = slice(ib * BIAS_BLOCK, (ib + 1) * BIAS_BLOCK)
                    if ib == ja:
                        blk = tab_ref[h, 0]
                    elif ib == ja + 1:
                        blk = tab_ref[h, 1]
                    else:
                        blk = zeros if ib > ja else masked
                    bias_ref[h, 0, rows, cols] = blk
                    bias_ref[h, 1, rows, cols] = tab_ref[h, 1] if (ja == nblk - 1 and ib == 0) else zeros

    m_ref[...] = jnp.full_like(m_ref, MASK_VALUE)
    acc_ref[...] = jnp.zeros_like(acc_ref)

    def produce(h, j):
        hc = slice(h * dv, (h + 1) * dv)
        q = q_ref[:, hc]
        k = k_ref[pl.ds(pl.multiple_of(j * tile, tile), tile), hc]
        zero = jnp.zeros_like(k)
        k2 = jnp.concatenate([jnp.where(first_map, k, zero), jnp.where(first_map, zero, k)], axis=0)
        st = lax.dot_general(k2, q, NT_DIMS, preferred_element_type=F32)
        st_ref[h] = st
        for i in range(2):
            mt_ref[h, i] = jnp.max(st[i * tile:(i + 1) * tile], axis=0, keepdims=True)

    def consume(h, j, slot):
        vt = vt_ref[j, h]
        for i in range(2):
            s = st_ref[h, i * tile:(i + 1) * tile, :]
            m_prev = m_ref[h, i]
            if slot is None:
                m_new = jnp.maximum(m_prev, mt_ref[h, i])
            else:
                s = s + bias_ref[h, slot]
                m_new = jnp.maximum(m_prev, jnp.max(s, axis=0, keepdims=True))
            alpha = jnp.exp2(m_prev - m_new)
            p = jnp.exp2(s - m_new).astype(BF16)
            acc_ref[h, i] = alpha * acc_ref[h, i] + jnp.dot(vt, p, preferred_element_type=F32)
            m_ref[h, i] = m_new

    n_far = jnp.maximum(qi - 1, 0)
    produce(0, 0)

    def far_body(j, carry):
        produce(1, j)
        consume(0, j, None)
        produce(0, j + 1)
        consume(1, j, None)
        return carry

    lax.fori_loop(0, n_far, far_body, 0)

    @pl.when(qi >= 1)
    def _():
        produce(1, qi - 1)
        consume(0, qi - 1, 1)
        produce(0, qi)
        consume(1, qi - 1, 1)
        produce(1, qi)
        consume(0, qi, 0)
        consume(1, qi, 0)

    @pl.when(qi == 0)
    def _():
        produce(1, 0)
        consume(0, 0, 0)
        consume(1, 0, 0)

    lam = lam_ref[...]
    lam_val = (jnp.exp(jnp.sum(lam[0:1] * lam[1:2], axis=-1, keepdims=True))
               - jnp.exp(jnp.sum(lam[2:3] * lam[3:4], axis=-1, keepdims=True)) + lambda_init)
    for h in range(nh):
        a0 = acc_ref[h, 0]
        a1 = acc_ref[h, 1]
        ot = a0[:dv] / a0[dv:dv + 1] - lam_val * (a1[:dv] / a1[dv:dv + 1])
        ms = jnp.mean(ot * ot, axis=0, keepdims=True)
        ot = ot * lax.rsqrt(ms + RMS_EPS) * swb_ref[...] * (1.0 - lambda_init)
        o_ref[:, h * dv:(h + 1) * dv] = ot.T.astype(o_ref.dtype)


def diff_attention(qb, kb, vt, bias_tables, lam, subln_w, lambda_init, *, tile):
    t = qb.shape[0]
    nb = t // tile
    dv = 2 * B_DH
    nh = ATTN_HEADS_PER_STEP
    resident = pl.Buffered(1)
    swb = jnp.broadcast_to(subln_w.reshape(dv, 1), (dv, tile))
    return pl.pallas_call(
        functools.partial(_attn_kernel, lambda_init=lambda_init, tile=tile),
        grid=(B_HEADS // nh, nb),
        in_specs=[pl.BlockSpec((tile, nh * dv), lambda h, i: (i, h)),
                  pl.BlockSpec((t, nh * dv), lambda h, i: (0, h), pipeline_mode=resident),
                  pl.BlockSpec((nb, nh, VT_ROWS, tile), lambda h, i: (0, h, 0, 0), pipeline_mode=resident),
                  pl.BlockSpec((nh, 2, BIAS_BLOCK, BIAS_BLOCK), lambda h, i: (h, 0, 0, 0)),
                  pl.BlockSpec((4, B_DH), lambda h, i: (0, 0)),
                  pl.BlockSpec((dv, tile), lambda h, i: (0, 0))],
        out_specs=pl.BlockSpec((tile, nh * dv), lambda h, i: (i, h)),
        out_shape=jax.ShapeDtypeStruct((t, B_HEADS * dv), BF16),
        scratch_shapes=[pltpu.VMEM((nh, 2, 1, tile), F32),
                        pltpu.VMEM((nh, 2, VT_ROWS, tile), F32),
                        pltpu.VMEM((nh, 2 * tile, tile), F32),
                        pltpu.VMEM((nh, 2, 1, tile), F32),
                        pltpu.VMEM((nh, 2, tile, tile), F32)],
        compiler_params=_params("arbitrary", "arbitrary"),
        name="diff_attention",
    )(qb, kb, vt, bias_tables, lam, swb)


def _t5_bucket(n):
    max_exact = REL_BUCKETS // 2
    nf = jnp.maximum(n, 1).astype(F32)
    large = max_exact + (jnp.log(nf / max_exact) / math.log(REL_MAX_DIST / max_exact)
                         * (REL_BUCKETS - max_exact)).astype(jnp.int32)
    large = jnp.minimum(large, REL_BUCKETS - 1)
    return jnp.where(n < max_exact, n, large)


def attention_bias_tables(rel_bias):
    tile = BIAS_BLOCK
    n = jnp.arange(2 * tile)
    far = rel_bias[REL_BUCKETS - 1]
    db = ((rel_bias[_t5_bucket(n)] - far[None, :]) * LOG2E).T.astype(F32)
    nh = db.shape[0]
    full = jnp.concatenate([jnp.full((nh, tile - 1), MASK_VALUE, F32), db], axis=1)

    def toeplitz(vec):
        w = jnp.concatenate([vec, jnp.zeros((nh, 1), F32)], axis=1)
        flat = jnp.tile(w, (1, tile))[:, :tile * (2 * tile - 1)]
        return flat.reshape(nh, tile, 2 * tile - 1)[:, :, tile - 1:]

    diag = toeplitz(full[:, :2 * tile - 1])
    near = toeplitz(full[:, tile:3 * tile - 1])
    return jnp.stack([diag, near], axis=1)


def _route(logits):
    mx = jnp.max(logits, axis=0, keepdims=True)
    ex = jnp.exp(logits - mx)
    probs = ex / jnp.sum(ex, axis=0, keepdims=True)
    row = [probs[e:e + 1, :] for e in range(N_EXPERTS)]
    epg = EXPERTS_PER_GROUP
    best_sum = None
    best_group = None
    for g in range(N_GROUPS):
        vals = row[g * epg:(g + 1) * epg]
        top2 = None
        for a in range(epg):
            for b in range(a + 1, epg):
                pair = vals[a] + vals[b]
                top2 = pair if top2 is None else jnp.maximum(top2, pair)
        if g == 0:
            best_sum = top2
            best_group = jnp.zeros_like(top2, dtype=jnp.int32)
        else:
            upd = top2 > best_sum
            best_sum = jnp.where(upd, top2, best_sum)
            best_group = jnp.where(upd, g, best_group)
    gp = []
    for j in range(epg):
        sel = row[j]
        for g in range(1, N_GROUPS):
            sel = jnp.where(best_group == g, row[g * epg + j], sel)
        gp.append(sel)
    v1 = gp[0]
    i1 = jnp.zeros_like(best_group)
    for j in range(1, epg):
        upd = gp[j] > v1
        v1 = jnp.where(upd, gp[j], v1)
        i1 = jnp.where(upd, j, i1)
    v2 = jnp.full_like(v1, -1.0)
    i2 = jnp.zeros_like(best_group)
    for j in range(epg):
        upd = (i1 != j) & (gp[j] > v2)
        v2 = jnp.where(upd, gp[j], v2)
        i2 = jnp.where(upd, j, i2)
    denom = v1 + v2
    e1 = best_group * epg + i1
    e2 = best_group * epg + i2
    eidx = lax.broadcasted_iota(jnp.int32, logits.shape, 0)
    gates = jnp.where(eidx == e1, v1 / denom, 0.0) + jnp.where(eidx == e2, v2 / denom, 0.0)
    return gates, best_group


MOE_BLOCK = 1024
MOE_CHUNK = 256
MOE_MAX_CHUNKS = MOE_BLOCK // MOE_CHUNK
MOE_GATE_ROWS = 16
MOE_VMEM_LIMIT = 56 * 1024 * 1024


def _moe_kernel(xt_ref, rwt_ref, rb_ref, tri_ref, wgu_ref, wd_ref, lw_ref, lb_ref, o_ref, ob_ref,
                xg_ref, gates_ref, grpc_ref, keyc_ref, grpr_ref, keyr_ref, cnt_ref,
                xs_ref, gw_ref, ych_ref, yacc_ref):
    e = pl.program_id(1)
    g = e // EXPERTS_PER_GROUP
    j = e % EXPERTS_PER_GROUP
    d, bt = xt_ref.shape
    n = MOE_CHUNK
    gr = MOE_GATE_ROWS

    @pl.when(e == 0)
    def _route_block():
        xt = xt_ref[...]
        xg_ref[:d, :] = xt.astype(BF16)
        logits = jnp.dot(rwt_ref[...], xt, preferred_element_type=F32, precision=HIGHEST) + rb_ref[...]
        gates, best_group = _route(logits)
        gidx = lax.broadcasted_iota(jnp.int32, (SUBLANES, bt), 0)
        onehot = jnp.where(gidx == best_group, 1.0, 0.0)
        rank = jnp.dot(onehot.astype(BF16), tri_ref[...], preferred_element_type=F32)
        key = jnp.sum(onehot * rank, axis=0, keepdims=True)
        grp_f = best_group.astype(F32)
        grpr_ref[...] = grp_f
        keyr_ref[...] = key
        stack = jnp.concatenate([grp_f, key, jnp.zeros((LANES - 2, bt), F32)], axis=0)
        col = stack.T
        grpc_ref[...] = col[:, 0:1]
        keyc_ref[...] = col[:, 1:2]
        for gg in range(N_GROUPS):
            cnt_ref[gg] = jnp.sum(onehot[gg:gg + 1, :]).astype(jnp.int32)
            gates_ref[gg] = jnp.concatenate(
                [gates[gg * EXPERTS_PER_GROUP:(gg + 1) * EXPERTS_PER_GROUP],
                 jnp.zeros((gr - EXPERTS_PER_GROUP, bt), F32)], axis=0)
        yacc_ref[...] = jnp.zeros_like(yacc_ref)

    count = cnt_ref[g]
    g_f = g.astype(F32)

    @pl.when(j == 0)
    def _gather_group():
        gt = gates_ref[g]
        hi = gt.astype(BF16)
        xg_ref[d:d + gr, :] = hi
        xg_ref[d + gr:d + 2 * gr, :] = (gt - hi.astype(F32)).astype(BF16)
        key_mine = jnp.where(grpc_ref[...] == g_f, keyc_ref[...], -1.0)
        lane = lax.broadcasted_iota(jnp.int32, (bt, n), 1).astype(F32)
        for c in range(MOE_MAX_CHUNKS):
            @pl.when(c * n < count)
            def _():
                selt = jnp.where(key_mine - float(c * n) == lane, 1.0, 0.0).astype(BF16)
                gx = jnp.dot(xg_ref[...], selt, preferred_element_type=F32)
                xs_ref[c] = gx[:d].astype(BF16)
                gw = gx[d:d + gr] + gx[d + gr:d + 2 * gr]
                for jj in range(EXPERTS_PER_GROUP):
                    gw_ref[c, jj] = jnp.broadcast_to(gw[jj:jj + 1, :], (SUBLANES, n))
                ych_ref[c] = jnp.zeros((d, n), F32)

    for c in range(MOE_MAX_CHUNKS):
        @pl.when(c * n < count)
        def _():
            xs = xs_ref[c]
            gw = gw_ref[c, j][0:1, :]
            half = D_EXPERT // 2
            gu = [(jnp.dot(wgu_ref[0, s * half:(s + 1) * half, :], xs, preferred_element_type=F32),
                   jnp.dot(wgu_ref[0, D_EXPERT + s * half:D_EXPERT + (s + 1) * half, :], xs,
                           preferred_element_type=F32)) for s in range(2)]
            y = ych_ref[c]
            for s in range(2):
                hid = (_silu(gu[s][0]) * gu[s][1] * gw).astype(BF16)
                y = y + jnp.dot(wd_ref[0, :, s * half:(s + 1) * half], hid, preferred_element_type=F32)
            ych_ref[c] = y

    @pl.when(j == EXPERTS_PER_GROUP - 1)
    def _scatter_group():
        key_mine = jnp.where(grpr_ref[...] == g_f, keyr_ref[...], -1.0)
        sub = lax.broadcasted_iota(jnp.int32, (n, bt), 0).astype(F32)
        for c in range(MOE_MAX_CHUNKS):
            @pl.when(c * n < count)
            def _():
                sel = jnp.where(key_mine - float(c * n) == sub, 1.0, 0.0).astype(BF16)
                yacc_ref[...] += jnp.dot(ych_ref[c].astype(BF16), sel, preferred_element_type=F32)

    @pl.when(e == N_EXPERTS - 1)
    def _finish():
        for c in range(bt // n):
            cols = slice(c * n, (c + 1) * n)
            z = DN_ALPHA * xt_ref[:, cols] + yacc_ref[:, cols]
            y = _layer_norm(z.T, lw_ref[...], lb_ref[...])
            o_ref[cols, :] = y
            ob_ref[cols, :] = y.astype(BF16)


def moe_residual_ln(xt, rwt, rb, tri, wgu_t, wd_t, lw, lb):
    d, t = xt.shape
    bt = MOE_BLOCK
    n = MOE_CHUNK
    assert t % bt == 0
    return pl.pallas_call(
        _moe_kernel,
        grid=(t // bt, N_EXPERTS),
        in_specs=[pl.BlockSpec((d, bt), lambda i, e: (0, i)),
                  pl.BlockSpec((N_EXPERTS, d), lambda i, e: (0, 0)),
                  pl.BlockSpec((N_EXPERTS, 1), lambda i, e: (0, 0)),
                  pl.BlockSpec((bt, bt), lambda i, e: (0, 0)),
                  pl.BlockSpec((1, 2 * D_EXPERT, d), lambda i, e: (e, 0, 0)),
                  pl.BlockSpec((1, d, D_EXPERT), lambda i, e: (e, 0, 0)),
                  pl.BlockSpec((1, d), lambda i, e: (0, 0)),
                  pl.BlockSpec((1, d), lambda i, e: (0, 0))],
        out_specs=[pl.BlockSpec((bt, d), lambda i, e: (i, 0)),
                   pl.BlockSpec((bt, d), lambda i, e: (i, 0))],
        out_shape=[jax.ShapeDtypeStruct((t, d), F32),
                   jax.ShapeDtypeStruct((t, d), BF16)],
        scratch_shapes=[pltpu.VMEM((d + 2 * MOE_GATE_ROWS, bt), BF16),
                        pltpu.VMEM((N_GROUPS, MOE_GATE_ROWS, bt), F32),
                        pltpu.VMEM((bt, 1), F32), pltpu.VMEM((bt, 1), F32),
                        pltpu.VMEM((1, bt), F32), pltpu.VMEM((1, bt), F32),
                        pltpu.SMEM((N_GROUPS,), jnp.int32),
                        pltpu.VMEM((MOE_MAX_CHUNKS, d, n), BF16),
                        pltpu.VMEM((MOE_MAX_CHUNKS, EXPERTS_PER_GROUP, SUBLANES, n), F32),
                        pltpu.VMEM((MOE_MAX_CHUNKS, d, n), F32),
                        pltpu.VMEM((d, bt), F32)],
        compiler_params=pltpu.CompilerParams(dimension_semantics=("parallel", "arbitrary"),
                                             vmem_limit_bytes=MOE_VMEM_LIMIT),
        name="moe_residual_ln",
    )(xt, rwt, rb, tri, wgu_t, wd_t, lw.reshape(1, d), lb.reshape(1, d))


def _lambda_init_for(layer):
    return 0.8 - 0.6 * math.exp(-0.3 * layer)


def kernel(x, a_w_in, a_conv, a_A_log, a_dt_bias, a_norm_w, a_w_out, b_w_q, b_w_kv, b_lambda, b_subln_w, b_w_out, rel_bias, router_w, router_b, moe_w_gate, moe_w_up, moe_w_down, ln_w, ln_b):
    bsz, t, d = x.shape
    assert bsz == 1 and d == D_MODEL and t % A_CHUNK == 0
    xf = x.reshape(t, d).astype(F32)
    xb = xf.astype(BF16)

    gate_tile = min(512, t)
    tri = jnp.asarray(np.kron(np.eye(gate_tile // A_CHUNK, dtype=np.float32),
                              np.tril(np.ones((A_CHUNK, A_CHUNK), np.float32))))
    moe_tri = jnp.asarray(np.triu(np.ones((MOE_BLOCK, MOE_BLOCK), np.float32), k=1), dtype=BF16)
    rwt = router_w.T.astype(F32)
    rb = router_b.reshape(N_EXPERTS, 1).astype(F32)
    attn_tile = min(512, t)
    assert attn_tile % BIAS_BLOCK == 0
    bias_tables = attention_bias_tables(rel_bias.astype(F32))

    kb = None
    vt = None
    for l in range(DEPTH):
        if l < N_A_LAYERS:
            w_in = a_w_in[l]
            w_main = w_in[:, :A_CONV_CH + A_V].astype(BF16)
            w_ab = jnp.pad(w_in[:, A_CONV_CH + A_V:], ((0, 0), (0, LANES - 2 * A_HEADS))).astype(BF16)
            conv_pad = jnp.pad(a_conv[l].astype(F32), ((0, 0), (0, A_V)))
            par = jnp.zeros((SUBLANES, LANES), F32)
            par = par.at[0, :A_HEADS].set(a_A_log[l].astype(F32))
            par = par.at[1, :A_HEADS].set(a_dt_bias[l].astype(F32))
            proj = gdn_inproj(xb, w_main, conv_pad)
            fac, fact = gdn_gates(xb, w_ab, par, tri)
            fact3 = fact.reshape(2 * A_HEADS, t // A_CHUNK, A_CHUNK).transpose(1, 0, 2)
            u, wq, qk, kd, dec = gdn_intra(proj, fac, fact3)
            mixed = gdn_scan(u, wq, qk, kd, dec, proj, a_norm_w[l].astype(F32))
            w_out = a_w_out[l].astype(BF16)
        else:
            j = l - N_A_LAYERS
            if kb is None:
                kb = matmul_bf16(xb, b_w_kv[:, :B_QK].astype(BF16))
                vt = v_transposed_proj(xb, b_w_kv[:, B_QK:].T.astype(BF16), attn_tile)
            qb = matmul_bf16(xb, b_w_q[j].astype(BF16), scale=B_DH ** -0.5 * LOG2E)
            mixed = diff_attention(qb, kb, vt, bias_tables, b_lambda[j].astype(F32),
                                   b_subln_w[j].astype(F32), _lambda_init_for(l), tile=attn_tile)
            w_out = b_w_out[j].astype(BF16)
        xt = matmul_residual_ln_t(mixed, w_out, xf, ln_w[l, 0].astype(F32), ln_b[l, 0].astype(F32))
        wgu_t = jnp.concatenate([moe_w_gate[l].transpose(0, 2, 1), moe_w_up[l].transpose(0, 2, 1)],
                                axis=1).astype(BF16)
        wd_t = moe_w_down[l].transpose(0, 2, 1).astype(BF16)
        xf, xb = moe_residual_ln(xt, rwt, rb, moe_tri, wgu_t, wd_t,
                                 ln_w[l, 1].astype(F32), ln_b[l, 1].astype(F32))
    return xf.reshape(bsz, t, d).astype(x.dtype)
```

```python
import functools
import math

import jax
import jax.numpy as jnp
import numpy as np
from jax import lax
from jax.experimental import pallas as pl
from jax.experimental.pallas import tpu as pltpu

F32 = jnp.float32
BF16 = jnp.bfloat16

D_MODEL = 1024
DEPTH = 4
N_A_LAYERS = DEPTH // 2
A_HEADS = 8
A_DK = 128
A_DV = 128
A_QK = A_HEADS * A_DK
A_V = A_HEADS * A_DV
A_CONV = 4
A_CONV_CH = 2 * A_QK + A_V
A_CHUNK = 64
B_HEADS = 8
B_DH = 64
B_QK = B_HEADS * 2 * B_DH
REL_BUCKETS = 32
REL_MAX_DIST = 128
N_EXPERTS = 16
N_GROUPS = 4
EXPERTS_PER_GROUP = N_EXPERTS // N_GROUPS
D_EXPERT = 512
DN_ALPHA = (2 * DEPTH) ** 0.25
LN_EPS = 1e-5
RMS_EPS = 1e-6

LANES = 128
SUBLANES = 8
VMEM_LIMIT = 48 * 1024 * 1024
MASK_VALUE = -1e30

HIGHEST = lax.Precision.HIGHEST
NT_DIMS = (((1,), (1,)), ((), ()))
TN_DIMS = (((0,), (0,)), ((), ()))


def _params(*sem):
    return pltpu.CompilerParams(dimension_semantics=sem, vmem_limit_bytes=VMEM_LIMIT)


def _sigmoid(x):
    return 1.0 / (1.0 + jnp.exp(-x))


def _silu(x):
    return x * _sigmoid(x)


def _layer_norm(v, w, b):
    mu = jnp.mean(v, axis=-1, keepdims=True)
    c = v - mu
    var = jnp.mean(c * c, axis=-1, keepdims=True)
    return c * lax.rsqrt(var + LN_EPS) * w + b


def _mm_kernel(a_ref, w_ref, o_ref, *, scale):
    y = jnp.dot(a_ref[...], w_ref[...], preferred_element_type=F32)
    if scale != 1.0:
        y = y * scale
    o_ref[...] = y.astype(o_ref.dtype)


def matmul_bf16(a, w, *, scale=1.0, tm=512, tn=1024):
    m, k = a.shape
    n = w.shape[1]
    tm = min(tm, m)
    return pl.pallas_call(
        functools.partial(_mm_kernel, scale=scale),
        grid=(n // tn, m // tm),
        in_specs=[pl.BlockSpec((tm, k), lambda j, i: (i, 0)),
                  pl.BlockSpec((k, tn), lambda j, i: (0, j))],
        out_specs=pl.BlockSpec((tm, tn), lambda j, i: (i, j)),
        out_shape=jax.ShapeDtypeStruct((m, n), BF16),
        compiler_params=_params("parallel", "arbitrary"),
        name="matmul_bf16",
    )(a, w)


def _mm_ln_kernel(a_ref, w_ref, x_ref, lw_ref, lb_ref, ot_ref):
    h = jnp.dot(a_ref[...], w_ref[...], preferred_element_type=F32)
    y = _layer_norm(DN_ALPHA * x_ref[...] + h, lw_ref[...], lb_ref[...])
    ot_ref[...] = y.T


def matmul_residual_ln_t(a, w, x, lw, lb, *, tm=512):
    m, k = a.shape
    n = w.shape[1]
    tm = min(tm, m)
    return pl.pallas_call(
        _mm_ln_kernel,
        grid=(m // tm,),
        in_specs=[pl.BlockSpec((tm, k), lambda i: (i, 0)),
                  pl.BlockSpec((k, n), lambda i: (0, 0)),
                  pl.BlockSpec((tm, n), lambda i: (i, 0)),
                  pl.BlockSpec((1, n), lambda i: (0, 0)),
                  pl.BlockSpec((1, n), lambda i: (0, 0))],
        out_specs=pl.BlockSpec((n, tm), lambda i: (0, i)),
        out_shape=jax.ShapeDtypeStruct((n, m), F32),
        compiler_params=_params("parallel"),
        name="matmul_residual_ln",
    )(a, w, x, lw.reshape(1, n), lb.reshape(1, n))


def _gdn_inproj_kernel(x_ref, w_ref, cw_ref, o_ref, halo_ref):
    n = pl.program_id(0)
    m = pl.program_id(1)
    tm, tn = o_ref.shape
    parts = 2
    pr = tm // parts

    @pl.when(m == 0)
    def _():
        halo_ref[...] = jnp.zeros_like(halo_ref)

    def project():
        return [jnp.dot(x_ref[p * pr:(p + 1) * pr, :], w_ref[...], preferred_element_type=F32)
                for p in range(parts)]

    def conv_silu(ys):
        cw = cw_ref[...]
        rows = lax.broadcasted_iota(jnp.int32, (SUBLANES, tn), 0)
        out = []
        halo = halo_ref[...]
        for y in ys:
            acc = y * cw[A_CONV - 1:A_CONV]
            for shift in range(1, A_CONV):
                sh = pltpu.roll(y, shift, 0)
                top = jnp.where(rows < shift, pltpu.roll(halo, shift, 0), sh[:SUBLANES])
                sh = jnp.concatenate([top, sh[SUBLANES:]], axis=0)
                acc = acc + sh * cw[A_CONV - 1 - shift:A_CONV - shift]
            halo = y[pr - SUBLANES:]
            out.append(_silu(acc))
        halo_ref[...] = halo
        return out

    @pl.when(n == 3)
    def _():
        for p, y in enumerate(project()):
            o_ref[p * pr:(p + 1) * pr, :] = y.astype(o_ref.dtype)

    @pl.when(n == 2)
    def _():
        for p, a in enumerate(conv_silu(project())):
            o_ref[p * pr:(p + 1) * pr, :] = a.astype(o_ref.dtype)

    @pl.when(n < 2)
    def _():
        post = jnp.where(n == 0, A_DK ** -0.5, 1.0).astype(F32)
        for p, a in enumerate(conv_silu(project())):
            for h in range(tn // A_DK):
                seg = a[:, h * A_DK:(h + 1) * A_DK]
                ss = jnp.sum(seg * seg, axis=-1, keepdims=True)
                o_ref[p * pr:(p + 1) * pr, h * A_DK:(h + 1) * A_DK] = (
                    seg * (lax.rsqrt(ss + RMS_EPS) * post)).astype(o_ref.dtype)


def gdn_inproj(xb, w_main, conv_w_pad, *, tm=512):
    t, k = xb.shape
    n = w_main.shape[1]
    tn = A_QK
    tm = min(tm, t)
    return pl.pallas_call(
        _gdn_inproj_kernel,
        grid=(n // tn, t // tm),
        in_specs=[pl.BlockSpec((tm, k), lambda j, i: (i, 0)),
                  pl.BlockSpec((k, tn), lambda j, i: (0, j)),
                  pl.BlockSpec((A_CONV, tn), lambda j, i: (0, j))],
        out_specs=pl.BlockSpec((tm, tn), lambda j, i: (i, j)),
        out_shape=jax.ShapeDtypeStruct((t, n), BF16),
        scratch_shapes=[pltpu.VMEM((SUBLANES, tn), F32)],
        compiler_params=_params("arbitrary", "arbitrary"),
        name="gdn_inproj",
    )(xb, w_main, conv_w_pad)


def _gdn_gates_kernel(x_ref, w_ref, par_ref, tri_ref, fac_ref, fact_ref):
    ab = jnp.dot(x_ref[...], w_ref[...], preferred_element_type=F32)
    lane = lax.broadcasted_iota(jnp.int32, ab.shape, 1)
    neg_a = -jnp.exp(par_ref[0:1, :])
    z = ab + par_ref[1:2, :]
    softplus = jnp.maximum(z, 0.0) + jnp.log1p(jnp.exp(-jnp.abs(z)))
    g = jnp.where(lane < A_HEADS, neg_a * softplus, 0.0)
    beta = _sigmoid(ab)
    gc = jnp.dot(tri_ref[...], g, preferred_element_type=F32, precision=HIGHEST)
    fac = jnp.where(lane < A_HEADS, gc, jnp.where(lane < 2 * A_HEADS, beta, 0.0))
    fac_ref[...] = fac
    fact_ref[...] = fac.T[:2 * A_HEADS, :]


def gdn_gates(xb, w_ab_pad, par, tri, *, tm=512):
    t, k = xb.shape
    tm = min(tm, t)
    return pl.pallas_call(
        _gdn_gates_kernel,
        grid=(t // tm,),
        in_specs=[pl.BlockSpec((tm, k), lambda i: (i, 0)),
                  pl.BlockSpec((k, LANES), lambda i: (0, 0)),
                  pl.BlockSpec((SUBLANES, LANES), lambda i: (0, 0)),
                  pl.BlockSpec((tm, tm), lambda i: (0, 0))],
        out_specs=[pl.BlockSpec((tm, LANES), lambda i: (i, 0)),
                   pl.BlockSpec((2 * A_HEADS, tm), lambda i: (0, i))],
        out_shape=[jax.ShapeDtypeStruct((t, LANES), F32),
                   jax.ShapeDtypeStruct((2 * A_HEADS, t), F32)],
        compiler_params=_params("parallel"),
        name="gdn_gates",
    )(xb, w_ab_pad, par, tri)


GDN_INTRA_CHUNKS = 2
GDN_SCAN_CHUNKS = 8


def _mm_bf16(a, b):
    return jnp.dot(a.astype(BF16), b.astype(BF16), preferred_element_type=F32)


def _gdn_intra_kernel(q_ref, k_ref, v_ref, fac_ref, fact_ref, u_ref, wq_ref, qk_ref, kd_ref, dec_ref):
    c = A_CHUNK
    ii = lax.broadcasted_iota(jnp.int32, (c, c), 0)
    jj = lax.broadcasted_iota(jnp.int32, (c, c), 1)
    incl = ii >= jj
    strict = ii > jj
    units = [(cb, h) for cb in range(GDN_INTRA_CHUNKS) for h in range(A_HEADS)]
    rows = {cb: slice(cb * c, (cb + 1) * c) for cb in range(GDN_INTRA_CHUNKS)}
    cols = {h: slice(h * A_DK, (h + 1) * A_DK) for h in range(A_HEADS)}

    q, k, kf, beta, eg, decay = {}, {}, {}, {}, {}, {}
    for cb, h in units:
        fac = fac_ref[rows[cb], :]
        gc_col = fac[:, h:h + 1]
        gc_row = fact_ref[cb][h:h + 1, :]
        g_last = gc_col[c - 1:c, :]
        beta[cb, h] = fac[:, A_HEADS + h:A_HEADS + h + 1]
        eg[cb, h] = jnp.exp(gc_col)
        decay[cb, h] = jnp.exp(jnp.where(incl, gc_col - gc_row, MASK_VALUE))
        q[cb, h] = q_ref[rows[cb], cols[h]]
        k[cb, h] = k_ref[rows[cb], cols[h]]
        kf[cb, h] = k[cb, h].astype(F32)
        kd_ref[rows[cb], cols[h]] = (kf[cb, h] * jnp.exp(g_last - gc_col)).astype(kd_ref.dtype)
        dec_ref[cb, h:h + 1, :] = jnp.broadcast_to(jnp.exp(g_last), (1, A_DV))
        wq_ref[cb, c:, cols[h]] = (q[cb, h].astype(F32) * eg[cb, h]).astype(wq_ref.dtype)

    mmat, x, p = {}, {}, {}
    for u_ in units:
        kk = lax.dot_general(k[u_], k[u_], NT_DIMS, preferred_element_type=F32)
        mmat[u_] = jnp.where(strict, beta[u_] * kk * decay[u_], 0.0)
    for u_ in units:
        cb, h = u_
        qk = lax.dot_general(q[u_], k[u_], NT_DIMS, preferred_element_type=F32) * decay[u_]
        qk_ref[cb, h] = qk.astype(qk_ref.dtype)
    for u_ in units:
        x[u_] = -mmat[u_]
        p[u_] = _mm_bf16(mmat[u_], mmat[u_])
    for level in range(5):
        for u_ in units:
            x[u_] = x[u_] + p[u_] + _mm_bf16(x[u_], p[u_])
        if level < 4:
            for u_ in units:
                p[u_] = _mm_bf16(p[u_], p[u_])
    for u_ in units:
        cb, h = u_
        v = v_ref[rows[cb], cols[h]].astype(F32)
        rhs = jnp.concatenate([beta[u_] * v, (beta[u_] * eg[u_]) * kf[u_]], axis=1)
        uw = rhs + _mm_bf16(x[u_], rhs)
        u_ref[rows[cb], cols[h]] = uw[:, :A_DV].astype(u_ref.dtype)
        wq_ref[cb, :c, cols[h]] = uw[:, A_DV:].astype(wq_ref.dtype)


def gdn_intra(proj, fac, fact3):
    t = proj.shape[0]
    c = A_CHUNK
    cb = GDN_INTRA_CHUNKS
    nc = t // c
    return pl.pallas_call(
        _gdn_intra_kernel,
        grid=(nc // cb,),
        in_specs=[pl.BlockSpec((cb * c, A_QK), lambda i: (i, 0)),
                  pl.BlockSpec((cb * c, A_QK), lambda i: (i, 1)),
                  pl.BlockSpec((cb * c, A_V), lambda i: (i, 2)),
                  pl.BlockSpec((cb * c, LANES), lambda i: (i, 0)),
                  pl.BlockSpec((cb, 2 * A_HEADS, c), lambda i: (i, 0, 0))],
        out_specs=[pl.BlockSpec((cb * c, A_V), lambda i: (i, 0)),
                   pl.BlockSpec((cb, 2 * c, A_QK), lambda i: (i, 0, 0)),
                   pl.BlockSpec((cb, A_HEADS, c, c), lambda i: (i, 0, 0, 0)),
                   pl.BlockSpec((cb * c, A_QK), lambda i: (i, 0)),
                   pl.BlockSpec((cb, A_HEADS, A_DV), lambda i: (i, 0, 0))],
        out_shape=[jax.ShapeDtypeStruct((t, A_V), BF16),
                   jax.ShapeDtypeStruct((nc, 2 * c, A_QK), BF16),
                   jax.ShapeDtypeStruct((nc, A_HEADS, c, c), BF16),
                   jax.ShapeDtypeStruct((t, A_QK), BF16),
                   jax.ShapeDtypeStruct((nc, A_HEADS, A_DV), F32)],
        compiler_params=_params("parallel"),
        name="gdn_intra",
    )(proj, proj, proj, fac, fact3)


def _gdn_scan_kernel(u_ref, wq_ref, qk_ref, kd_ref, dec_ref, z_ref, nw_ref, o_ref, s_ref):
    c = A_CHUNK

    @pl.when(pl.program_id(0) == 0)
    def _():
        s_ref[...] = jnp.zeros_like(s_ref)

    nw = nw_ref[...]
    heads = range(A_HEADS)
    cols = [slice(h * A_DK, (h + 1) * A_DK) for h in heads]
    for cb in range(GDN_SCAN_CHUNKS):
        rows = slice(cb * c, (cb + 1) * c)
        s = [s_ref[h] for h in heads]
        sb = [s[h].astype(BF16) for h in heads]
        ws_qs = [jnp.dot(wq_ref[cb, :, cols[h]], sb[h], preferred_element_type=F32) for h in heads]
        v_new = [(u_ref[rows, cols[h]].astype(F32) - ws_qs[h][:c]).astype(BF16) for h in heads]
        o = [ws_qs[h][c:] + jnp.dot(qk_ref[cb, h], v_new[h], preferred_element_type=F32) for h in heads]
        for h in heads:
            s_ref[h] = s[h] * dec_ref[cb, h:h + 1, :] + lax.dot_general(
                kd_ref[rows, cols[h]], v_new[h], TN_DIMS, preferred_element_type=F32)
        for h in heads:
            z = z_ref[rows, cols[h]].astype(F32)
            ms = jnp.mean(o[h] * o[h], axis=-1, keepdims=True)
            o_ref[rows, cols[h]] = (o[h] * lax.rsqrt(ms + RMS_EPS) * nw * _silu(z)).astype(o_ref.dtype)


def gdn_scan(u, wq, qk, kd, dec, proj, norm_w):
    t = u.shape[0]
    c = A_CHUNK
    cb = GDN_SCAN_CHUNKS
    return pl.pallas_call(
        _gdn_scan_kernel,
        grid=(t // (cb * c),),
        in_specs=[pl.BlockSpec((cb * c, A_V), lambda i: (i, 0)),
                  pl.BlockSpec((cb, 2 * c, A_QK), lambda i: (i, 0, 0)),
                  pl.BlockSpec((cb, A_HEADS, c, c), lambda i: (i, 0, 0, 0)),
                  pl.BlockSpec((cb * c, A_QK), lambda i: (i, 0)),
                  pl.BlockSpec((cb, A_HEADS, A_DV), lambda i: (i, 0, 0)),
                  pl.BlockSpec((cb * c, A_V), lambda i: (i, 3)),
                  pl.BlockSpec((1, A_DV), lambda i: (0, 0))],
        out_specs=pl.BlockSpec((cb * c, A_V), lambda i: (i, 0)),
        out_shape=jax.ShapeDtypeStruct((t, A_V), BF16),
        scratch_shapes=[pltpu.VMEM((A_HEADS, A_DK, A_DV), F32)],
        compiler_params=_params("arbitrary"),
        name="gdn_scan",
    )(u, wq, qk, kd, dec, proj, norm_w.reshape(1, A_DV))


LOG2E = math.log2(math.e)
VT_ROWS = 2 * B_DH + 16


def _vt_proj_kernel(x_ref, w_ref, o_ref):
    tk = x_ref.shape[0]
    y = lax.dot_general(w_ref[...], x_ref[...], NT_DIMS, preferred_element_type=F32)
    ones = jnp.ones((VT_ROWS - 2 * B_DH, tk), o_ref.dtype)
    for h in range(B_HEADS):
        o_ref[0, h, :2 * B_DH, :] = y[h * 2 * B_DH:(h + 1) * 2 * B_DH].astype(o_ref.dtype)
        o_ref[0, h, 2 * B_DH:, :] = ones


def v_transposed_proj(xb, w_vt, tile):
    t, k = xb.shape
    n = w_vt.shape[0]
    return pl.pallas_call(
        _vt_proj_kernel,
        grid=(t // tile,),
        in_specs=[pl.BlockSpec((tile, k), lambda i: (i, 0)),
                  pl.BlockSpec((n, k), lambda i: (0, 0))],
        out_specs=pl.BlockSpec((1, B_HEADS, VT_ROWS, tile), lambda i: (i, 0, 0, 0)),
        out_shape=jax.ShapeDtypeStruct((t // tile, B_HEADS, VT_ROWS, tile), BF16),
        compiler_params=_params("parallel"),
        name="v_transposed_proj",
    )(xb, w_vt)


ATTN_HEADS_PER_STEP = 2
BIAS_BLOCK = REL_MAX_DIST


def _attn_kernel(q_ref, k_ref, vt_ref, tab_ref, lam_ref, swb_ref, o_ref, m_ref, acc_ref,
                 st_ref, mt_ref, bias_ref, *, lambda_init, tile):
    qi = pl.program_id(1)
    dv = 2 * B_DH
    nh = ATTN_HEADS_PER_STEP
    first_map = lax.broadcasted_iota(jnp.int32, (tile, dv), 1) < B_DH

    @pl.when(qi == 0)
    def _():
        nblk = tile // BIAS_BLOCK
        zeros = jnp.zeros((BIAS_BLOCK, BIAS_BLOCK), F32)
        masked = jnp.full((BIAS_BLOCK, BIAS_BLOCK), MASK_VALUE, F32)
        for h in range(nh):
            for ja in range(nblk):
                for ib in range(nblk):
                    rows = slice(ja * BIAS_BLOCK, (ja + 1) * BIAS_BLOCK)
                    cols = slice(ib * BIAS_BLOCK, (ib + 1) * BIAS_BLOCK)
                    if ib == ja:
                        blk = tab_ref[h, 0]
                    elif ib == ja + 1:
                        blk = tab_ref[h, 1]
                    else:
                        blk = zeros if ib > ja else masked
                    bias_ref[h, 0, rows, cols] = blk
                    bias_ref[h, 1, rows, cols] = tab_ref[h, 1] if (ja == nblk - 1 and ib == 0) else zeros

    m_ref[...] = jnp.full_like(m_ref, MASK_VALUE)
    acc_ref[...] = jnp.zeros_like(acc_ref)

    def produce(h, j):
        hc = slice(h * dv, (h + 1) * dv)
        q = q_ref[:, hc]
        k = k_ref[pl.ds(pl.multiple_of(j * tile, tile), tile), hc]
        zero = jnp.zeros_like(k)
        k2 = jnp.concatenate([jnp.where(first_map, k, zero), jnp.where(first_map, zero, k)], axis=0)
        st = lax.dot_general(k2, q, NT_DIMS, preferred_element_type=F32)
        st_ref[h] = st
        for i in range(2):
            mt_ref[h, i] = jnp.max(st[i * tile:(i + 1) * tile], axis=0, keepdims=True)

    def consume(h, j, slot):
        vt = vt_ref[j, h]
        for i in range(2):
            s = st_ref[h, i * tile:(i + 1) * tile, :]
            m_prev = m_ref[h, i]
            if slot is None:
                m_new = jnp.maximum(m_prev, mt_ref[h, i])
            else:
                s = s + bias_ref[h, slot]
                m_new = jnp.maximum(m_prev, jnp.max(s, axis=0, keepdims=True))
            alpha = jnp.exp2(m_prev - m_new)
            p = jnp.exp2(s - m_new).astype(BF16)
            acc_ref[h, i] = alpha * acc_ref[h, i] + jnp.dot(vt, p, preferred_element_type=F32)
            m_ref[h, i] = m_new

    n_far = jnp.maximum(qi - 1, 0)
    produce(0, 0)

    def far_units(j):
        produce(1, j)
        consume(0, j, None)
        produce(0, j + 1)
        consume(1, j, None)

    def far_pair_body(jj, carry):
        far_units(2 * jj)
        far_units(2 * jj + 1)
        return carry

    lax.fori_loop(0, n_far // 2, far_pair_body, 0)

    @pl.when(n_far % 2 == 1)
    def _():
        far_units(n_far - 1)

    @pl.when(qi >= 1)
    def _():
        produce(1, qi - 1)
        consume(0, qi - 1, 1)
        produce(0, qi)
        consume(1, qi - 1, 1)
        produce(1, qi)
        consume(0, qi, 0)
        consume(1, qi, 0)

    @pl.when(qi == 0)
    def _():
        produce(1, 0)
        consume(0, 0, 0)
        consume(1, 0, 0)

    lam = lam_ref[...]
    lam_val = (jnp.exp(jnp.sum(lam[0:1] * lam[1:2], axis=-1, keepdims=True))
               - jnp.exp(jnp.sum(lam[2:3] * lam[3:4], axis=-1, keepdims=True)) + lambda_init)
    for h in range(nh):
        a0 = acc_ref[h, 0]
        a1 = acc_ref[h, 1]
        ot = a0[:dv] / a0[dv:dv + 1] - lam_val * (a1[:dv] / a1[dv:dv + 1])
        ms = jnp.mean(ot * ot, axis=0, keepdims=True)
        ot = ot * lax.rsqrt(ms + RMS_EPS) * swb_ref[...] * (1.0 - lambda_init)
        o_ref[:, h * dv:(h + 1) * dv] = ot.T.astype(o_ref.dtype)


def diff_attention(qb, kb, vt, bias_tables, lam, subln_w, lambda_init, *, tile):
    t = qb.shape[0]
    nb = t // tile
    dv = 2 * B_DH
    nh = ATTN_HEADS_PER_STEP
    resident = pl.Buffered(1)
    swb = jnp.broadcast_to(subln_w.reshape(dv, 1), (dv, tile))
    return pl.pallas_call(
        functools.partial(_attn_kernel, lambda_init=lambda_init, tile=tile),
        grid=(B_HEADS // nh, nb),
        in_specs=[pl.BlockSpec((tile, nh * dv), lambda h, i: (i, h)),
                  pl.BlockSpec((t, nh * dv), lambda h, i: (0, h), pipeline_mode=resident),
                  pl.BlockSpec((nb, nh, VT_ROWS, tile), lambda h, i: (0, h, 0, 0), pipeline_mode=resident),
                  pl.BlockSpec((nh, 2, BIAS_BLOCK, BIAS_BLOCK), lambda h, i: (h, 0, 0, 0)),
                  pl.BlockSpec((4, B_DH), lambda h, i: (0, 0)),
                  pl.BlockSpec((dv, tile), lambda h, i: (0, 0))],
        out_specs=pl.BlockSpec((tile, nh * dv), lambda h, i: (i, h)),
        out_shape=jax.ShapeDtypeStruct((t, B_HEADS * dv), BF16),
        scratch_shapes=[pltpu.VMEM((nh, 2, 1, tile), F32),
                        pltpu.VMEM((nh, 2, VT_ROWS, tile), F32),
                        pltpu.VMEM((nh, 2 * tile, tile), F32),
                        pltpu.VMEM((nh, 2, 1, tile), F32),
                        pltpu.VMEM((nh, 2, tile, tile), F32)],
        compiler_params=_params("arbitrary", "arbitrary"),
        name="diff_attention",
    )(qb, kb, vt, bias_tables, lam, swb)


def _t5_bucket(n):
    max_exact = REL_BUCKETS // 2
    nf = jnp.maximum(n, 1).astype(F32)
    large = max_exact + (jnp.log(nf / max_exact) / math.log(REL_MAX_DIST / max_exact)
                         * (REL_BUCKETS - max_exact)).astype(jnp.int32)
    large = jnp.minimum(large, REL_BUCKETS - 1)
    return jnp.where(n < max_exact, n, large)


def attention_bias_tables(rel_bias):
    tile = BIAS_BLOCK
    n = jnp.arange(2 * tile)
    far = rel_bias[REL_BUCKETS - 1]
    db = ((rel_bias[_t5_bucket(n)] - far[None, :]) * LOG2E).T.astype(F32)
    nh = db.shape[0]
    full = jnp.concatenate([jnp.full((nh, tile - 1), MASK_VALUE, F32), db], axis=1)

    def toeplitz(vec):
        w = jnp.concatenate([vec, jnp.zeros((nh, 1), F32)], axis=1)
        flat = jnp.tile(w, (1, tile))[:, :tile * (2 * tile - 1)]
        return flat.reshape(nh, tile, 2 * tile - 1)[:, :, tile - 1:]

    diag = toeplitz(full[:, :2 * tile - 1])
    near = toeplitz(full[:, tile:3 * tile - 1])
    return jnp.stack([diag, near], axis=1)


def _route(logits):
    mx = jnp.max(logits, axis=0, keepdims=True)
    ex = jnp.exp(logits - mx)
    probs = ex / jnp.sum(ex, axis=0, keepdims=True)
    row = [probs[e:e + 1, :] for e in range(N_EXPERTS)]
    epg = EXPERTS_PER_GROUP
    best_sum = None
    best_group = None
    for g in range(N_GROUPS):
        vals = row[g * epg:(g + 1) * epg]
        top2 = None
        for a in range(epg):
            for b in range(a + 1, epg):
                pair = vals[a] + vals[b]
                top2 = pair if top2 is None else jnp.maximum(top2, pair)
        if g == 0:
            best_sum = top2
            best_group = jnp.zeros_like(top2, dtype=jnp.int32)
        else:
            upd = top2 > best_sum
            best_sum = jnp.where(upd, top2, best_sum)
            best_group = jnp.where(upd, g, best_group)
    gp = []
    for j in range(epg):
        sel = row[j]
        for g in range(1, N_GROUPS):
            sel = jnp.where(best_group == g, row[g * epg + j], sel)
        gp.append(sel)
    v1 = gp[0]
    i1 = jnp.zeros_like(best_group)
    for j in range(1, epg):
        upd = gp[j] > v1
        v1 = jnp.where(upd, gp[j], v1)
        i1 = jnp.where(upd, j, i1)
    v2 = jnp.full_like(v1, -1.0)
    i2 = jnp.zeros_like(best_group)
    for j in range(epg):
        upd = (i1 != j) & (gp[j] > v2)
        v2 = jnp.where(upd, gp[j], v2)
        i2 = jnp.where(upd, j, i2)
    denom = v1 + v2
    e1 = best_group * epg + i1
    e2 = best_group * epg + i2
    eidx = lax.broadcasted_iota(jnp.int32, logits.shape, 0)
    gates = jnp.where(eidx == e1, v1 / denom, 0.0) + jnp.where(eidx == e2, v2 / denom, 0.0)
    return gates, best_group


MOE_BLOCK = 1024
MOE_CHUNK = 256
MOE_MAX_CHUNKS = MOE_BLOCK // MOE_CHUNK
MOE_SLAB = LANES
MOE_GATE_ROWS = 16
MOE_VMEM_LIMIT = 56 * 1024 * 1024


def _moe_kernel(xt_ref, rwt_ref, rb_ref, tri_ref, wgu_ref, wd_ref, lw_ref, lb_ref, o_ref, ob_ref,
                xg_ref, gates_ref, grpc_ref, keyc_ref, grpr_ref, keyr_ref, cnt_ref,
                xs_ref, gw_ref, ych_ref, yacc_ref):
    e = pl.program_id(1)
    g = e // EXPERTS_PER_GROUP
    j = e % EXPERTS_PER_GROUP
    d, bt = xt_ref.shape
    n = MOE_CHUNK
    gr = MOE_GATE_ROWS

    @pl.when(e == 0)
    def _route_block():
        xt = xt_ref[...]
        xg_ref[:d, :] = xt.astype(BF16)
        logits = jnp.dot(rwt_ref[...], xt, preferred_element_type=F32, precision=HIGHEST) + rb_ref[...]
        gates, best_group = _route(logits)
        gidx = lax.broadcasted_iota(jnp.int32, (SUBLANES, bt), 0)
        onehot = jnp.where(gidx == best_group, 1.0, 0.0)
        rank = jnp.dot(onehot.astype(BF16), tri_ref[...], preferred_element_type=F32)
        key = jnp.sum(onehot * rank, axis=0, keepdims=True)
        grp_f = best_group.astype(F32)
        grpr_ref[...] = grp_f
        keyr_ref[...] = key
        stack = jnp.concatenate([grp_f, key, jnp.zeros((LANES - 2, bt), F32)], axis=0)
        col = stack.T
        grpc_ref[...] = col[:, 0:1]
        keyc_ref[...] = col[:, 1:2]
        for gg in range(N_GROUPS):
            cnt_ref[gg] = jnp.sum(onehot[gg:gg + 1, :]).astype(jnp.int32)
            gates_ref[gg] = jnp.concatenate(
                [gates[gg * EXPERTS_PER_GROUP:(gg + 1) * EXPERTS_PER_GROUP],
                 jnp.zeros((gr - EXPERTS_PER_GROUP, bt), F32)], axis=0)
        yacc_ref[...] = jnp.zeros_like(yacc_ref)

    count = cnt_ref[g]
    g_f = g.astype(F32)

    @pl.when(j == 0)
    def _gather_group():
        gt = gates_ref[g]
        hi = gt.astype(BF16)
        xg_ref[d:d + gr, :] = hi
        xg_ref[d + gr:d + 2 * gr, :] = (gt - hi.astype(F32)).astype(BF16)
        key_mine = jnp.where(grpc_ref[...] == g_f, keyc_ref[...], -1.0)
        lane = lax.broadcasted_iota(jnp.int32, (bt, n), 1).astype(F32)
        for c in range(MOE_MAX_CHUNKS):
            @pl.when(c * n < count)
            def _():
                selt = jnp.where(key_mine - float(c * n) == lane, 1.0, 0.0).astype(BF16)
                gx = jnp.dot(xg_ref[...], selt, preferred_element_type=F32)
                xs_ref[c] = gx[:d].astype(BF16)
                gw = gx[d:d + gr] + gx[d + gr:d + 2 * gr]
                for jj in range(EXPERTS_PER_GROUP):
                    gw_ref[c, jj] = jnp.broadcast_to(gw[jj:jj + 1, :], (SUBLANES, n))
                ych_ref[c] = jnp.zeros((d, n), F32)

    for c in range(MOE_MAX_CHUNKS):
        @pl.when(c * n < count)
        def _():
            xs = xs_ref[c]
            gw = gw_ref[c, j][0:1, :]
            half = D_EXPERT // 2
            gu = [(jnp.dot(wgu_ref[0, s * half:(s + 1) * half, :], xs, preferred_element_type=F32),
                   jnp.dot(wgu_ref[0, D_EXPERT + s * half:D_EXPERT + (s + 1) * half, :], xs,
                           preferred_element_type=F32)) for s in range(2)]
            y = ych_ref[c]
            for s in range(2):
                hid = (_silu(gu[s][0]) * gu[s][1] * gw).astype(BF16)
                y = y + jnp.dot(wd_ref[0, :, s * half:(s + 1) * half], hid, preferred_element_type=F32)
            ych_ref[c] = y

    @pl.when(j == EXPERTS_PER_GROUP - 1)
    def _scatter_group():
        key_mine = jnp.where(grpr_ref[...] == g_f, keyr_ref[...], -1.0)
        sub = lax.broadcasted_iota(jnp.int32, (n, bt), 0).astype(F32)
        for c in range(MOE_MAX_CHUNKS):
            @pl.when(c * n < count)
            def _():
                sel = jnp.where(key_mine - float(c * n) == sub, 1.0, 0.0).astype(BF16)
                yacc_ref[...] += jnp.dot(ych_ref[c].astype(BF16), sel, preferred_element_type=F32)

    @pl.when(e == N_EXPERTS - 1)
    def _finish():
        for c in range(bt // MOE_SLAB):
            cols = slice(c * MOE_SLAB, (c + 1) * MOE_SLAB)
            z = DN_ALPHA * xt_ref[:, cols] + yacc_ref[:, cols]
            y = _layer_norm(z.T, lw_ref[...], lb_ref[...])
            o_ref[cols, :] = y
            ob_ref[cols, :] = y.astype(BF16)


def moe_residual_ln(xt, rwt, rb, tri, wgu_t, wd_t, lw, lb):
    d, t = xt.shape
    bt = MOE_BLOCK
    n = MOE_CHUNK
    assert t % bt == 0 and bt % MOE_SLAB == 0
    return pl.pallas_call(
        _moe_kernel,
        grid=(t // bt, N_EXPERTS),
        in_specs=[pl.BlockSpec((d, bt), lambda i, e: (0, i)),
                  pl.BlockSpec((N_EXPERTS, d), lambda i, e: (0, 0)),
                  pl.BlockSpec((N_EXPERTS, 1), lambda i, e: (0, 0)),
                  pl.BlockSpec((bt, bt), lambda i, e: (0, 0)),
                  pl.BlockSpec((1, 2 * D_EXPERT, d), lambda i, e: (e, 0, 0)),
                  pl.BlockSpec((1, d, D_EXPERT), lambda i, e: (e, 0, 0)),
                  pl.BlockSpec((1, d), lambda i, e: (0, 0)),
                  pl.BlockSpec((1, d), lambda i, e: (0, 0))],
        out_specs=[pl.BlockSpec((bt, d), lambda i, e: (i, 0)),
                   pl.BlockSpec((bt, d), lambda i, e: (i, 0))],
        out_shape=[jax.ShapeDtypeStruct((t, d), F32),
                   jax.ShapeDtypeStruct((t, d), BF16)],
        scratch_shapes=[pltpu.VMEM((d + 2 * MOE_GATE_ROWS, bt), BF16),
                        pltpu.VMEM((N_GROUPS, MOE_GATE_ROWS, bt), F32),
                        pltpu.VMEM((bt, 1), F32), pltpu.VMEM((bt, 1), F32),
                        pltpu.VMEM((1, bt), F32), pltpu.VMEM((1, bt), F32),
                        pltpu.SMEM((N_GROUPS,), jnp.int32),
                        pltpu.VMEM((MOE_MAX_CHUNKS, d, n), BF16),
                        pltpu.VMEM((MOE_MAX_CHUNKS, EXPERTS_PER_GROUP, SUBLANES, n), F32),
                        pltpu.VMEM((MOE_MAX_CHUNKS, d, n), F32),
                        pltpu.VMEM((d, bt), F32)],
        compiler_params=pltpu.CompilerParams(dimension_semantics=("arbitrary", "arbitrary"),
                                             vmem_limit_bytes=MOE_VMEM_LIMIT),
        name="moe_residual_ln",
    )(xt, rwt, rb, tri, wgu_t, wd_t, lw.reshape(1, d), lb.reshape(1, d))


def _lambda_init_for(layer):
    return 0.8 - 0.6 * math.exp(-0.3 * layer)


def kernel(x, a_w_in, a_conv, a_A_log, a_dt_bias, a_norm_w, a_w_out, b_w_q, b_w_kv, b_lambda, b_subln_w, b_w_out, rel_bias, router_w, router_b, moe_w_gate, moe_w_up, moe_w_down, ln_w, ln_b):
    bsz, t, d = x.shape
    assert bsz == 1 and d == D_MODEL and t % A_CHUNK == 0
    xf = x.reshape(t, d).astype(F32)
    xb = xf.astype(BF16)

    gate_tile = min(512, t)
    tri = jnp.asarray(np.kron(np.eye(gate_tile // A_CHUNK, dtype=np.float32),
                              np.tril(np.ones((A_CHUNK, A_CHUNK), np.float32))))
    moe_tri = jnp.asarray(np.triu(np.ones((MOE_BLOCK, MOE_BLOCK), np.float32), k=1), dtype=BF16)
    rwt = router_w.T.astype(F32)
    rb = router_b.reshape(N_EXPERTS, 1).astype(F32)
    attn_tile = min(512, t)
    assert attn_tile % BIAS_BLOCK == 0
    bias_tables = attention_bias_tables(rel_bias.astype(F32))

    kb = None
    vt = None
    for l in range(DEPTH):
        if l < N_A_LAYERS:
            w_in = a_w_in[l]
            w_main = w_in[:, :A_CONV_CH + A_V].astype(BF16)
            w_ab = jnp.pad(w_in[:, A_CONV_CH + A_V:], ((0, 0), (0, LANES - 2 * A_HEADS))).astype(BF16)
            conv_pad = jnp.pad(a_conv[l].astype(F32), ((0, 0), (0, A_V)))
            par = jnp.zeros((SUBLANES, LANES), F32)
            par = par.at[0, :A_HEADS].set(a_A_log[l].astype(F32))
            par = par.at[1, :A_HEADS].set(a_dt_bias[l].astype(F32))
            proj = gdn_inproj(xb, w_main, conv_pad)
            fac, fact = gdn_gates(xb, w_ab, par, tri)
            fact3 = fact.reshape(2 * A_HEADS, t // A_CHUNK, A_CHUNK).transpose(1, 0, 2)
            u, wq, qk, kd, dec = gdn_intra(proj, fac, fact3)
            mixed = gdn_scan(u, wq, qk, kd, dec, proj, a_norm_w[l].astype(F32))
            w_out = a_w_out[l].astype(BF16)
        else:
            j = l - N_A_LAYERS
            if kb is None:
                kb = matmul_bf16(xb, b_w_kv[:, :B_QK].astype(BF16))
                vt = v_transposed_proj(xb, b_w_kv[:, B_QK:].T.astype(BF16), attn_tile)
            qb = matmul_bf16(xb, b_w_q[j].astype(BF16), scale=B_DH ** -0.5 * LOG2E)
            mixed = diff_attention(qb, kb, vt, bias_tables, b_lambda[j].astype(F32),
                                   b_subln_w[j].astype(F32), _lambda_init_for(l), tile=attn_tile)
            w_out = b_w_out[j].astype(BF16)
        xt = matmul_residual_ln_t(mixed, w_out, xf, ln_w[l, 0].astype(F32), ln_b[l, 0].astype(F32))
        wgu_t = jnp.concatenate([moe_w_gate[l].transpose(0, 2, 1), moe_w_up[l].transpose(0, 2, 1)],
                                axis=1).astype(BF16)
        wd_t = moe_w_down[l].transpose(0, 2, 1).astype(BF16)
        xf, xb = moe_residual_ln(xt, rwt, rb, moe_tri, wgu_t, wd_t,
                                 ln_w[l, 1].astype(F32), ln_b[l, 1].astype(F32))
    return xf.reshape(bsz, t, d).astype(x.dtype)
```

```python
import functools
import math

import jax
import jax.numpy as jnp
import numpy as np
from jax import lax
from jax.experimental import pallas as pl
from jax.experimental.pallas import tpu as pltpu

F32 = jnp.float32
BF16 = jnp.bfloat16

D_MODEL = 1024
DEPTH = 4
N_A_LAYERS = DEPTH // 2
A_HEADS = 8
A_DK = 128
A_DV = 128
A_QK = A_HEADS * A_DK
A_V = A_HEADS * A_DV
A_CONV = 4
A_CONV_CH = 2 * A_QK + A_V
A_CHUNK = 64
B_HEADS = 8
B_DH = 64
B_QK = B_HEADS * 2 * B_DH
REL_BUCKETS = 32
REL_MAX_DIST = 128
N_EXPERTS = 16
N_GROUPS = 4
EXPERTS_PER_GROUP = N_EXPERTS // N_GROUPS
D_EXPERT = 512
DN_ALPHA = (2 * DEPTH) ** 0.25
LN_EPS = 1e-5
RMS_EPS = 1e-6

LANES = 128
SUBLANES = 8
VMEM_LIMIT = 48 * 1024 * 1024
MASK_VALUE = -1e30

HIGHEST = lax.Precision.HIGHEST
NT_DIMS = (((1,), (1,)), ((), ()))
TN_DIMS = (((0,), (0,)), ((), ()))


def _params(*sem):
    return pltpu.CompilerParams(dimension_semantics=sem, vmem_limit_bytes=VMEM_LIMIT)


def _sigmoid(x):
    return 1.0 / (1.0 + jnp.exp(-x))


def _silu(x):
    return x * _sigmoid(x)


def _layer_norm(v, w, b):
    mu = jnp.mean(v, axis=-1, keepdims=True)
    c = v - mu
    var = jnp.mean(c * c, axis=-1, keepdims=True)
    return c * lax.rsqrt(var + LN_EPS) * w + b


def _mm_kernel(a_ref, w_ref, o_ref, *, scale):
    y = jnp.dot(a_ref[...], w_ref[...], preferred_element_type=F32)
    if scale != 1.0:
        y = y * scale
    o_ref[...] = y.astype(o_ref.dtype)


def matmul_bf16(a, w, *, scale=1.0, tm=512, tn=1024):
    m, k = a.shape
    n = w.shape[1]
    tm = min(tm, m)
    return pl.pallas_call(
        functools.partial(_mm_kernel, scale=scale),
        grid=(n // tn, m // tm),
        in_specs=[pl.BlockSpec((tm, k), lambda j, i: (i, 0)),
                  pl.BlockSpec((k, tn), lambda j, i: (0, j))],
        out_specs=pl.BlockSpec((tm, tn), lambda j, i: (i, j)),
        out_shape=jax.ShapeDtypeStruct((m, n), BF16),
        compiler_params=_params("parallel", "arbitrary"),
        name="matmul_bf16",
    )(a, w)


def _mm_ln_kernel(a_ref, w_ref, x_ref, lw_ref, lb_ref, ot_ref):
    h = jnp.dot(a_ref[...], w_ref[...], preferred_element_type=F32)
    y = _layer_norm(DN_ALPHA * x_ref[...] + h, lw_ref[...], lb_ref[...])
    ot_ref[...] = y.T


def matmul_residual_ln_t(a, w, x, lw, lb, *, tm=512):
    m, k = a.shape
    n = w.shape[1]
    tm = min(tm, m)
    return pl.pallas_call(
        _mm_ln_kernel,
        grid=(m // tm,),
        in_specs=[pl.BlockSpec((tm, k), lambda i: (i, 0)),
                  pl.BlockSpec((k, n), lambda i: (0, 0)),
                  pl.BlockSpec((tm, n), lambda i: (i, 0)),
                  pl.BlockSpec((1, n), lambda i: (0, 0)),
                  pl.BlockSpec((1, n), lambda i: (0, 0))],
        out_specs=pl.BlockSpec((n, tm), lambda i: (0, i)),
        out_shape=jax.ShapeDtypeStruct((n, m), F32),
        compiler_params=_params("parallel"),
        name="matmul_residual_ln",
    )(a, w, x, lw.reshape(1, n), lb.reshape(1, n))


def _gdn_inproj_kernel(x_ref, w_ref, cw_ref, o_ref, halo_ref):
    n = pl.program_id(0)
    m = pl.program_id(1)
    tm, tn = o_ref.shape
    parts = 2
    pr = tm // parts

    @pl.when(m == 0)
    def _():
        halo_ref[...] = jnp.zeros_like(halo_ref)

    def project():
        return [jnp.dot(x_ref[p * pr:(p + 1) * pr, :], w_ref[...], preferred_element_type=F32)
                for p in range(parts)]

    def conv_silu(ys):
        cw = cw_ref[...]
        rows = lax.broadcasted_iota(jnp.int32, (SUBLANES, tn), 0)
        out = []
        halo = halo_ref[...]
        for y in ys:
            acc = y * cw[A_CONV - 1:A_CONV]
            for shift in range(1, A_CONV):
                sh = pltpu.roll(y, shift, 0)
                top = jnp.where(rows < shift, pltpu.roll(halo, shift, 0), sh[:SUBLANES])
                sh = jnp.concatenate([top, sh[SUBLANES:]], axis=0)
                acc = acc + sh * cw[A_CONV - 1 - shift:A_CONV - shift]
            halo = y[pr - SUBLANES:]
            out.append(_silu(acc))
        halo_ref[...] = halo
        return out

    @pl.when(n == 3)
    def _():
        for p, y in enumerate(project()):
            o_ref[p * pr:(p + 1) * pr, :] = y.astype(o_ref.dtype)

    @pl.when(n == 2)
    def _():
        for p, a in enumerate(conv_silu(project())):
            o_ref[p * pr:(p + 1) * pr, :] = a.astype(o_ref.dtype)

    @pl.when(n < 2)
    def _():
        post = jnp.where(n == 0, A_DK ** -0.5, 1.0).astype(F32)
        for p, a in enumerate(conv_silu(project())):
            for h in range(tn // A_DK):
                seg = a[:, h * A_DK:(h + 1) * A_DK]
                ss = jnp.sum(seg * seg, axis=-1, keepdims=True)
                o_ref[p * pr:(p + 1) * pr, h * A_DK:(h + 1) * A_DK] = (
                    seg * (lax.rsqrt(ss + RMS_EPS) * post)).astype(o_ref.dtype)


def gdn_inproj(xb, w_main, conv_w_pad, *, tm=512):
    t, k = xb.shape
    n = w_main.shape[1]
    tn = A_QK
    tm = min(tm, t)
    return pl.pallas_call(
        _gdn_inproj_kernel,
        grid=(n // tn, t // tm),
        in_specs=[pl.BlockSpec((tm, k), lambda j, i: (i, 0)),
                  pl.BlockSpec((k, tn), lambda j, i: (0, j)),
                  pl.BlockSpec((A_CONV, tn), lambda j, i: (0, j))],
        out_specs=pl.BlockSpec((tm, tn), lambda j, i: (i, j)),
        out_shape=jax.ShapeDtypeStruct((t, n), BF16),
        scratch_shapes=[pltpu.VMEM((SUBLANES, tn), F32)],
        compiler_params=_params("arbitrary", "arbitrary"),
        name="gdn_inproj",
    )(xb, w_main, conv_w_pad)


def _gdn_gates_kernel(x_ref, w_ref, par_ref, tri_ref, fac_ref, fact_ref):
    ab = jnp.dot(x_ref[...], w_ref[...], preferred_element_type=F32)
    lane = lax.broadcasted_iota(jnp.int32, ab.shape, 1)
    neg_a = -jnp.exp(par_ref[0:1, :])
    z = ab + par_ref[1:2, :]
    softplus = jnp.maximum(z, 0.0) + jnp.log1p(jnp.exp(-jnp.abs(z)))
    g = jnp.where(lane < A_HEADS, neg_a * softplus, 0.0)
    beta = _sigmoid(ab)
    gc = jnp.dot(tri_ref[...], g, preferred_element_type=F32, precision=HIGHEST)
    fac = jnp.where(lane < A_HEADS, gc, jnp.where(lane < 2 * A_HEADS, beta, 0.0))
    fac_ref[...] = fac
    fact_ref[...] = fac.T[:2 * A_HEADS, :]


def gdn_gates(xb, w_ab_pad, par, tri, *, tm=512):
    t, k = xb.shape
    tm = min(tm, t)
    return pl.pallas_call(
        _gdn_gates_kernel,
        grid=(t // tm,),
        in_specs=[pl.BlockSpec((tm, k), lambda i: (i, 0)),
                  pl.BlockSpec((k, LANES), lambda i: (0, 0)),
                  pl.BlockSpec((SUBLANES, LANES), lambda i: (0, 0)),
                  pl.BlockSpec((tm, tm), lambda i: (0, 0))],
        out_specs=[pl.BlockSpec((tm, LANES), lambda i: (i, 0)),
                   pl.BlockSpec((2 * A_HEADS, tm), lambda i: (0, i))],
        out_shape=[jax.ShapeDtypeStruct((t, LANES), F32),
                   jax.ShapeDtypeStruct((2 * A_HEADS, t), F32)],
        compiler_params=_params("parallel"),
        name="gdn_gates",
    )(xb, w_ab_pad, par, tri)


GDN_INTRA_CHUNKS = 2
GDN_SCAN_CHUNKS = 8


def _mm_bf16(a, b):
    return jnp.dot(a.astype(BF16), b.astype(BF16), preferred_element_type=F32)


def _gdn_intra_kernel(q_ref, k_ref, v_ref, fac_ref, fact_ref, u_ref, wq_ref, qk_ref, kd_ref, dec_ref):
    c = A_CHUNK
    ii = lax.broadcasted_iota(jnp.int32, (c, c), 0)
    jj = lax.broadcasted_iota(jnp.int32, (c, c), 1)
    incl = ii >= jj
    strict = ii > jj
    units = [(cb, h) for cb in range(GDN_INTRA_CHUNKS) for h in range(A_HEADS)]
    rows = {cb: slice(cb * c, (cb + 1) * c) for cb in range(GDN_INTRA_CHUNKS)}
    cols = {h: slice(h * A_DK, (h + 1) * A_DK) for h in range(A_HEADS)}

    q, k, kf, beta, eg, decay = {}, {}, {}, {}, {}, {}
    for cb, h in units:
        fac = fac_ref[rows[cb], :]
        gc_col = fac[:, h:h + 1]
        gc_row = fact_ref[cb][h:h + 1, :]
        g_last = gc_col[c - 1:c, :]
        beta[cb, h] = fac[:, A_HEADS + h:A_HEADS + h + 1]
        eg[cb, h] = jnp.exp(gc_col)
        decay[cb, h] = jnp.exp(jnp.where(incl, gc_col - gc_row, MASK_VALUE))
        q[cb, h] = q_ref[rows[cb], cols[h]]
        k[cb, h] = k_ref[rows[cb], cols[h]]
        kf[cb, h] = k[cb, h].astype(F32)
        kd_ref[rows[cb], cols[h]] = (kf[cb, h] * jnp.exp(g_last - gc_col)).astype(kd_ref.dtype)
        dec_ref[cb, h:h + 1, :] = jnp.broadcast_to(jnp.exp(g_last), (1, A_DV))
        wq_ref[cb, c:, cols[h]] = (q[cb, h].astype(F32) * eg[cb, h]).astype(wq_ref.dtype)

    mmat, x, p = {}, {}, {}
    for u_ in units:
        kk = lax.dot_general(k[u_], k[u_], NT_DIMS, preferred_element_type=F32)
        mmat[u_] = jnp.where(strict, beta[u_] * kk * decay[u_], 0.0)
    for u_ in units:
        cb, h = u_
        qk = lax.dot_general(q[u_], k[u_], NT_DIMS, preferred_element_type=F32) * decay[u_]
        qk_ref[cb, h] = qk.astype(qk_ref.dtype)
    for u_ in units:
        x[u_] = -mmat[u_]
        p[u_] = _mm_bf16(mmat[u_], mmat[u_])
    for level in range(5):
        for u_ in units:
            x[u_] = x[u_] + p[u_] + _mm_bf16(x[u_], p[u_])
        if level < 4:
            for u_ in units:
                p[u_] = _mm_bf16(p[u_], p[u_])
    for u_ in units:
        cb, h = u_
        v = v_ref[rows[cb], cols[h]].astype(F32)
        rhs = jnp.concatenate([beta[u_] * v, (beta[u_] * eg[u_]) * kf[u_]], axis=1)
        uw = rhs + _mm_bf16(x[u_], rhs)
        u_ref[rows[cb], cols[h]] = uw[:, :A_DV].astype(u_ref.dtype)
        wq_ref[cb, :c, cols[h]] = uw[:, A_DV:].astype(wq_ref.dtype)


def gdn_intra(proj, fac, fact3):
    t = proj.shape[0]
    c = A_CHUNK
    cb = GDN_INTRA_CHUNKS
    nc = t // c
    return pl.pallas_call(
        _gdn_intra_kernel,
        grid=(nc // cb,),
        in_specs=[pl.BlockSpec((cb * c, A_QK), lambda i: (i, 0)),
                  pl.BlockSpec((cb * c, A_QK), lambda i: (i, 1)),
                  pl.BlockSpec((cb * c, A_V), lambda i: (i, 2)),
                  pl.BlockSpec((cb * c, LANES), lambda i: (i, 0)),
                  pl.BlockSpec((cb, 2 * A_HEADS, c), lambda i: (i, 0, 0))],
        out_specs=[pl.BlockSpec((cb * c, A_V), lambda i: (i, 0)),
                   pl.BlockSpec((cb, 2 * c, A_QK), lambda i: (i, 0, 0)),
                   pl.BlockSpec((cb, A_HEADS, c, c), lambda i: (i, 0, 0, 0)),
                   pl.BlockSpec((cb * c, A_QK), lambda i: (i, 0)),
                   pl.BlockSpec((cb, A_HEADS, A_DV), lambda i: (i, 0, 0))],
        out_shape=[jax.ShapeDtypeStruct((t, A_V), BF16),
                   jax.ShapeDtypeStruct((nc, 2 * c, A_QK), BF16),
                   jax.ShapeDtypeStruct((nc, A_HEADS, c, c), BF16),
                   jax.ShapeDtypeStruct((t, A_QK), BF16),
                   jax.ShapeDtypeStruct((nc, A_HEADS, A_DV), F32)],
        compiler_params=_params("parallel"),
        name="gdn_intra",
    )(proj, proj, proj, fac, fact3)


def _gdn_scan_kernel(u_ref, wq_ref, qk_ref, kd_ref, dec_ref, z_ref, nw_ref, o_ref, s_ref):
    c = A_CHUNK

    @pl.when(pl.program_id(0) == 0)
    def _():
        s_ref[...] = jnp.zeros_like(s_ref)

    nw = nw_ref[...]
    heads = range(A_HEADS)
    cols = [slice(h * A_DK, (h + 1) * A_DK) for h in heads]
    for cb in range(GDN_SCAN_CHUNKS):
        rows = slice(cb * c, (cb + 1) * c)
        s = [s_ref[h] for h in heads]
        sb = [s[h].astype(BF16) for h in heads]
        ws_qs = [jnp.dot(wq_ref[cb, :, cols[h]], sb[h], preferred_element_type=F32) for h in heads]
        v_new = [(u_ref[rows, cols[h]].astype(F32) - ws_qs[h][:c]).astype(BF16) for h in heads]
        o = [ws_qs[h][c:] + jnp.dot(qk_ref[cb, h], v_new[h], preferred_element_type=F32) for h in heads]
        for h in heads:
            s_ref[h] = s[h] * dec_ref[cb, h:h + 1, :] + lax.dot_general(
                kd_ref[rows, cols[h]], v_new[h], TN_DIMS, preferred_element_type=F32)
        for h in heads:
            z = z_ref[rows, cols[h]].astype(F32)
            ms = jnp.mean(o[h] * o[h], axis=-1, keepdims=True)
            o_ref[rows, cols[h]] = (o[h] * lax.rsqrt(ms + RMS_EPS) * nw * _silu(z)).astype(o_ref.dtype)


def gdn_scan(u, wq, qk, kd, dec, proj, norm_w):
    t = u.shape[0]
    c = A_CHUNK
    cb = GDN_SCAN_CHUNKS
    return pl.pallas_call(
        _gdn_scan_kernel,
        grid=(t // (cb * c),),
        in_specs=[pl.BlockSpec((cb * c, A_V), lambda i: (i, 0)),
                  pl.BlockSpec((cb, 2 * c, A_QK), lambda i: (i, 0, 0)),
                  pl.BlockSpec((cb, A_HEADS, c, c), lambda i: (i, 0, 0, 0)),
                  pl.BlockSpec((cb * c, A_QK), lambda i: (i, 0)),
                  pl.BlockSpec((cb, A_HEADS, A_DV), lambda i: (i, 0, 0)),
                  pl.BlockSpec((cb * c, A_V), lambda i: (i, 3)),
                  pl.BlockSpec((1, A_DV), lambda i: (0, 0))],
        out_specs=pl.BlockSpec((cb * c, A_V), lambda i: (i, 0)),
        out_shape=jax.ShapeDtypeStruct((t, A_V), BF16),
        scratch_shapes=[pltpu.VMEM((A_HEADS, A_DK, A_DV), F32)],
        compiler_params=_params("arbitrary"),
        name="gdn_scan",
    )(u, wq, qk, kd, dec, proj, norm_w.reshape(1, A_DV))


LOG2E = math.log2(math.e)
VT_ROWS = 2 * B_DH + 16


def _vt_proj_kernel(x_ref, w_ref, o_ref):
    tk = x_ref.shape[0]
    y = lax.dot_general(w_ref[...], x_ref[...], NT_DIMS, preferred_element_type=F32)
    ones = jnp.ones((VT_ROWS - 2 * B_DH, tk), o_ref.dtype)
    for h in range(B_HEADS):
        o_ref[0, h, :2 * B_DH, :] = y[h * 2 * B_DH:(h + 1) * 2 * B_DH].astype(o_ref.dtype)
        o_ref[0, h, 2 * B_DH:, :] = ones


def v_transposed_proj(xb, w_vt, tile):
    t, k = xb.shape
    n = w_vt.shape[0]
    return pl.pallas_call(
        _vt_proj_kernel,
        grid=(t // tile,),
        in_specs=[pl.BlockSpec((tile, k), lambda i: (i, 0)),
                  pl.BlockSpec((n, k), lambda i: (0, 0))],
        out_specs=pl.BlockSpec((1, B_HEADS, VT_ROWS, tile), lambda i: (i, 0, 0, 0)),
        out_shape=jax.ShapeDtypeStruct((t // tile, B_HEADS, VT_ROWS, tile), BF16),
        compiler_params=_params("parallel"),
        name="v_transposed_proj",
    )(xb, w_vt)


ATTN_HEADS_PER_STEP = 2
BIAS_BLOCK = REL_MAX_DIST


def _attn_kernel(q_ref, k_ref, vt_ref, tab_ref, lam_ref, swb_ref, o_ref, m_ref, acc_ref,
                 st_ref, mt_ref, bias_ref, *, lambda_init, tile):
    qi = pl.program_id(1)
    dv = 2 * B_DH
    nh = ATTN_HEADS_PER_STEP
    first_map = lax.broadcasted_iota(jnp.int32, (tile, dv), 1) < B_DH

    @pl.when(qi == 0)
    def _():
        nblk = tile // BIAS_BLOCK
        zeros = jnp.zeros((BIAS_BLOCK, BIAS_BLOCK), F32)
        masked = jnp.full((BIAS_BLOCK, BIAS_BLOCK), MASK_VALUE, F32)
        for h in range(nh):
            for ja in range(nblk):
                for ib in range(nblk):
                    rows = slice(ja * BIAS_BLOCK, (ja + 1) * BIAS_BLOCK)
                    cols = slice(ib * BIAS_BLOCK, (ib + 1) * BIAS_BLOCK)
                    if ib == ja:
                        blk = tab_ref[h, 0]
                    elif ib == ja + 1:
                        blk = tab_ref[h, 1]
                    else:
                        blk = zeros if ib > ja else masked
                    bias_ref[h, 0, rows, cols] = blk
                    bias_ref[h, 1, rows, cols] = tab_ref[h, 1] if (ja == nblk - 1 and ib == 0) else zeros

    m_ref[...] = jnp.full_like(m_ref, MASK_VALUE)
    acc_ref[...] = jnp.zeros_like(acc_ref)

    def produce(h, j):
        hc = slice(h * dv, (h + 1) * dv)
        q = q_ref[:, hc]
        k = k_ref[pl.ds(pl.multiple_of(j * tile, tile), tile), hc]
        zero = jnp.zeros_like(k)
        k2 = jnp.concatenate([jnp.where(first_map, k, zero), jnp.where(first_map, zero, k)], axis=0)
        st = lax.dot_general(k2, q, NT_DIMS, preferred_element_type=F32)
        st_ref[h] = st
        for i in range(2):
            mt_ref[h, i] = jnp.max(st[i * tile:(i + 1) * tile], axis=0, keepdims=True)

    def consume(h, j, slot):
        vt = vt_ref[j, h]
        for i in range(2):
            s = st_ref[h, i * tile:(i + 1) * tile, :]
            m_prev = m_ref[h, i]
            if slot is None:
                m_new = jnp.maximum(m_prev, mt_ref[h, i])
            else:
                s = s + bias_ref[h, slot]
                m_new = jnp.maximum(m_prev, jnp.max(s, axis=0, keepdims=True))
            alpha = jnp.exp2(m_prev - m_new)
            p = jnp.exp2(s - m_new).astype(BF16)
            acc_ref[h, i] = alpha * acc_ref[h, i] + jnp.dot(vt, p, preferred_element_type=F32)
            m_ref[h, i] = m_new

    n_far = jnp.maximum(qi - 1, 0)
    produce(0, 0)

    def far_units(j):
        produce(1, j)
        consume(0, j, None)
        produce(0, j + 1)
        consume(1, j, None)

    def far_pair_body(jj, carry):
        far_units(2 * jj)
        far_units(2 * jj + 1)
        return carry

    lax.fori_loop(0, n_far // 2, far_pair_body, 0)

    @pl.when(n_far % 2 == 1)
    def _():
        far_units(n_far - 1)

    @pl.when(qi >= 1)
    def _():
        produce(1, qi - 1)
        consume(0, qi - 1, 1)
        produce(0, qi)
        consume(1, qi - 1, 1)
        produce(1, qi)
        consume(0, qi, 0)
        consume(1, qi, 0)

    @pl.when(qi == 0)
    def _():
        produce(1, 0)
        consume(0, 0, 0)
        consume(1, 0, 0)

    lam = lam_ref[...]
    lam_val = (jnp.exp(jnp.sum(lam[0:1] * lam[1:2], axis=-1, keepdims=True))
               - jnp.exp(jnp.sum(lam[2:3] * lam[3:4], axis=-1, keepdims=True)) + lambda_init)
    for h in range(nh):
        a0 = acc_ref[h, 0]
        a1 = acc_ref[h, 1]
        ot = a0[:dv] / a0[dv:dv + 1] - lam_val * (a1[:dv] / a1[dv:dv + 1])
        ms = jnp.mean(ot * ot, axis=0, keepdims=True)
        ot = ot * lax.rsqrt(ms + RMS_EPS) * swb_ref[...] * (1.0 - lambda_init)
        o_ref[:, h * dv:(h + 1) * dv] = ot.T.astype(o_ref.dtype)


def diff_attention(qb, kb, vt, bias_tables, lam, subln_w, lambda_init, *, tile):
    t = qb.shape[0]
    nb = t // tile
    dv = 2 * B_DH
    nh = ATTN_HEADS_PER_STEP
    resident = pl.Buffered(1)
    swb = jnp.broadcast_to(subln_w.reshape(dv, 1), (dv, tile))
    return pl.pallas_call(
        functools.partial(_attn_kernel, lambda_init=lambda_init, tile=tile),
        grid=(B_HEADS // nh, nb),
        in_specs=[pl.BlockSpec((tile, nh * dv), lambda h, i: (i, h)),
                  pl.BlockSpec((t, nh * dv), lambda h, i: (0, h), pipeline_mode=resident),
                  pl.BlockSpec((nb, nh, VT_ROWS, tile), lambda h, i: (0, h, 0, 0), pipeline_mode=resident),
                  pl.BlockSpec((nh, 2, BIAS_BLOCK, BIAS_BLOCK), lambda h, i: (h, 0, 0, 0)),
                  pl.BlockSpec((4, B_DH), lambda h, i: (0, 0)),
                  pl.BlockSpec((dv, tile), lambda h, i: (0, 0))],
        out_specs=pl.BlockSpec((tile, nh * dv), lambda h, i: (i, h)),
        out_shape=jax.ShapeDtypeStruct((t, B_HEADS * dv), BF16),
        scratch_shapes=[pltpu.VMEM((nh, 2, 1, tile), F32),
                        pltpu.VMEM((nh, 2, VT_ROWS, tile), F32),
                        pltpu.VMEM((nh, 2 * tile, tile), F32),
                        pltpu.VMEM((nh, 2, 1, tile), F32),
                        pltpu.VMEM((nh, 2, tile, tile), F32)],
        compiler_params=_params("arbitrary", "arbitrary"),
        name="diff_attention",
    )(qb, kb, vt, bias_tables, lam, swb)


def _t5_bucket(n):
    max_exact = REL_BUCKETS // 2
    nf = jnp.maximum(n, 1).astype(F32)
    large = max_exact + (jnp.log(nf / max_exact) / math.log(REL_MAX_DIST / max_exact)
                         * (REL_BUCKETS - max_exact)).astype(jnp.int32)
    large = jnp.minimum(large, REL_BUCKETS - 1)
    return jnp.where(n < max_exact, n, large)


def attention_bias_tables(rel_bias):
    tile = BIAS_BLOCK
    n = jnp.arange(2 * tile)
    far = rel_bias[REL_BUCKETS - 1]
    db = ((rel_bias[_t5_bucket(n)] - far[None, :]) * LOG2E).T.astype(F32)
    nh = db.shape[0]
    full = jnp.concatenate([jnp.full((nh, tile - 1), MASK_VALUE, F32), db], axis=1)

    def toeplitz(vec):
        w = jnp.concatenate([vec, jnp.zeros((nh, 1), F32)], axis=1)
        flat = jnp.tile(w, (1, tile))[:, :tile * (2 * tile - 1)]
        return flat.reshape(nh, tile, 2 * tile - 1)[:, :, tile - 1:]

    diag = toeplitz(full[:, :2 * tile - 1])
    near = toeplitz(full[:, tile:3 * tile - 1])
    return jnp.stack([diag, near], axis=1)


def _route(logits):
    mx = jnp.max(logits, axis=0, keepdims=True)
    ex = jnp.exp(logits - mx)
    probs = ex / jnp.sum(ex, axis=0, keepdims=True)
    row = [probs[e:e + 1, :] for e in range(N_EXPERTS)]
    epg = EXPERTS_PER_GROUP
    best_sum = None
    best_group = None
    for g in range(N_GROUPS):
        vals = row[g * epg:(g + 1) * epg]
        top2 = None
        for a in range(epg):
            for b in range(a + 1, epg):
                pair = vals[a] + vals[b]
                top2 = pair if top2 is None else jnp.maximum(top2, pair)
        if g == 0:
            best_sum = top2
            best_group = jnp.zeros_like(top2, dtype=jnp.int32)
        else:
            upd = top2 > best_sum
            best_sum = jnp.where(upd, top2, best_sum)
            best_group = jnp.where(upd, g, best_group)
    gp = []
    for j in range(epg):
        sel = row[j]
        for g in range(1, N_GROUPS):
            sel = jnp.where(best_group == g, row[g * epg + j], sel)
        gp.append(sel)
    v1 = gp[0]
    i1 = jnp.zeros_like(best_group)
    for j in range(1, epg):
        upd = gp[j] > v1
        v1 = jnp.where(upd, gp[j], v1)
        i1 = jnp.where(upd, j, i1)
    v2 = jnp.full_like(v1, -1.0)
    i2 = jnp.zeros_like(best_group)
    for j in range(epg):
        upd = (i1 != j) & (gp[j] > v2)
        v2 = jnp.where(upd, gp[j], v2)
        i2 = jnp.where(upd, j, i2)
    denom = v1 + v2
    e1 = best_group * epg + i1
    e2 = best_group * epg + i2
    eidx = lax.broadcasted_iota(jnp.int32, logits.shape, 0)
    gates = jnp.where(eidx == e1, v1 / denom, 0.0) + jnp.where(eidx == e2, v2 / denom, 0.0)
    return gates, best_group


MOE_BLOCK = 1024
MOE_CHUNK = 256
MOE_MAX_CHUNKS = MOE_BLOCK // MOE_CHUNK
MOE_SLAB = LANES
MOE_GATE_ROWS = 16
MOE_VMEM_LIMIT = 56 * 1024 * 1024


def _moe_kernel(xt_ref, rwt_ref, rb_ref, tri_ref, wgu_ref, wd_ref, lw_ref, lb_ref, o_ref, ob_ref,
                xg_ref, gates_ref, grpc_ref, keyc_ref, grpr_ref, keyr_ref, cnt_ref,
                xs_ref, gw_ref, ych_ref, yacc_ref):
    e = pl.program_id(1)
    g = e // EXPERTS_PER_GROUP
    j = e % EXPERTS_PER_GROUP
    d, bt = xt_ref.shape
    n = MOE_CHUNK
    gr = MOE_GATE_ROWS

    @pl.when(e == 0)
    def _route_block():
        xt = xt_ref[...]
        xg_ref[:d, :] = xt.astype(BF16)
        logits = jnp.dot(rwt_ref[...], xt, preferred_element_type=F32, precision=HIGHEST) + rb_ref[...]
        gates, best_group = _route(logits)
        gidx = lax.broadcasted_iota(jnp.int32, (SUBLANES, bt), 0)
        onehot = jnp.where(gidx == best_group, 1.0, 0.0)
        rank = jnp.dot(onehot.astype(BF16), tri_ref[...], preferred_element_type=F32)
        key = jnp.sum(onehot * rank, axis=0, keepdims=True)
        grp_f = best_group.astype(F32)
        grpr_ref[...] = grp_f
        keyr_ref[...] = key
        stack = jnp.concatenate([grp_f, key, jnp.zeros((LANES - 2, bt), F32)], axis=0)
        col = stack.T
        grpc_ref[...] = col[:, 0:1]
        keyc_ref[...] = col[:, 1:2]
        for gg in range(N_GROUPS):
            cnt_ref[gg] = jnp.sum(onehot[gg:gg + 1, :]).astype(jnp.int32)
            gates_ref[gg] = jnp.concatenate(
                [gates[gg * EXPERTS_PER_GROUP:(gg + 1) * EXPERTS_PER_GROUP],
                 jnp.zeros((gr - EXPERTS_PER_GROUP, bt), F32)], axis=0)
        yacc_ref[...] = jnp.zeros_like(yacc_ref)

    count = cnt_ref[g]
    g_f = g.astype(F32)

    @pl.when(j == 0)
    def _gather_group():
        gt = gates_ref[g]
        hi = gt.astype(BF16)
        xg_ref[d:d + gr, :] = hi
        xg_ref[d + gr:d + 2 * gr, :] = (gt - hi.astype(F32)).astype(BF16)
        key_mine = jnp.where(grpc_ref[...] == g_f, keyc_ref[...], -1.0)
        lane = lax.broadcasted_iota(jnp.int32, (bt, n), 1).astype(F32)
        for c in range(MOE_MAX_CHUNKS):
            @pl.when(c * n < count)
            def _():
                selt = jnp.where(key_mine - float(c * n) == lane, 1.0, 0.0).astype(BF16)
                gx = jnp.dot(xg_ref[...], selt, preferred_element_type=F32)
                xs_ref[c] = gx[:d].astype(BF16)
                gw = gx[d:d + gr] + gx[d + gr:d + 2 * gr]
                for jj in range(EXPERTS_PER_GROUP):
                    gw_ref[c, jj] = jnp.broadcast_to(gw[jj:jj + 1, :], (SUBLANES, n))
                ych_ref[c] = jnp.zeros((d, n), F32)

    for c in range(MOE_MAX_CHUNKS):
        @pl.when(c * n < count)
        def _():
            xs = xs_ref[c]
            gw = gw_ref[c, j][0:1, :]
            half = D_EXPERT // 2
            gu = [(jnp.dot(wgu_ref[0, s * half:(s + 1) * half, :], xs, preferred_element_type=F32),
                   jnp.dot(wgu_ref[0, D_EXPERT + s * half:D_EXPERT + (s + 1) * half, :], xs,
                           preferred_element_type=F32)) for s in range(2)]
            y = ych_ref[c]
            for s in range(2):
                hid = (_silu(gu[s][0]) * gu[s][1] * gw).astype(BF16)
                y = y + jnp.dot(wd_ref[0, :, s * half:(s + 1) * half], hid, preferred_element_type=F32)
            ych_ref[c] = y

    @pl.when(j == EXPERTS_PER_GROUP - 1)
    def _scatter_group():
        key_mine = jnp.where(grpr_ref[...] == g_f, keyr_ref[...], -1.0)
        sub = lax.broadcasted_iota(jnp.int32, (n, bt), 0).astype(F32)
        for c in range(MOE_MAX_CHUNKS):
            @pl.when(c * n < count)
            def _():
                sel = jnp.where(key_mine - float(c * n) == sub, 1.0, 0.0).astype(BF16)
                yacc_ref[...] += jnp.dot(ych_ref[c].astype(BF16), sel, preferred_element_type=F32)

    @pl.when(e == N_EXPERTS - 1)
    def _finish():
        for c in range(bt // MOE_SLAB):
            cols = slice(c * MOE_SLAB, (c + 1) * MOE_SLAB)
            z = DN_ALPHA * xt_ref[:, cols] + yacc_ref[:, cols]
            y = _layer_norm(z.T, lw_ref[...], lb_ref[...])
            o_ref[cols, :] = y
            ob_ref[cols, :] = y.astype(BF16)


def _transpose_cast_kernel(*refs):
    *in_refs, o_ref = refs
    rows = 0
    for w_ref in in_refs:
        wt = w_ref[0, 0].astype(F32).T
        o_ref[0, rows:rows + wt.shape[0], :] = wt.astype(o_ref.dtype)
        rows += wt.shape[0]


def transpose_cast(layer, *ws):
    _, ne, r, c = ws[0].shape
    return pl.pallas_call(
        _transpose_cast_kernel,
        grid=(ne,),
        in_specs=[pl.BlockSpec((1, 1, r, c), lambda e: (layer, e, 0, 0)) for _ in ws],
        out_specs=pl.BlockSpec((1, len(ws) * c, r), lambda e: (e, 0, 0)),
        out_shape=jax.ShapeDtypeStruct((ne, len(ws) * c, r), BF16),
        compiler_params=_params("parallel"),
        name="transpose_cast",
    )(*ws)


def moe_residual_ln(xt, rwt, rb, tri, wgu_t, wd_t, lw, lb):
    d, t = xt.shape
    bt = MOE_BLOCK
    n = MOE_CHUNK
    assert t % bt == 0 and bt % MOE_SLAB == 0
    return pl.pallas_call(
        _moe_kernel,
        grid=(t // bt, N_EXPERTS),
        in_specs=[pl.BlockSpec((d, bt), lambda i, e: (0, i)),
                  pl.BlockSpec((N_EXPERTS, d), lambda i, e: (0, 0)),
                  pl.BlockSpec((N_EXPERTS, 1), lambda i, e: (0, 0)),
                  pl.BlockSpec((bt, bt), lambda i, e: (0, 0)),
                  pl.BlockSpec((1, 2 * D_EXPERT, d), lambda i, e: (e, 0, 0)),
                  pl.BlockSpec((1, d, D_EXPERT), lambda i, e: (e, 0, 0)),
                  pl.BlockSpec((1, d), lambda i, e: (0, 0)),
                  pl.BlockSpec((1, d), lambda i, e: (0, 0))],
        out_specs=[pl.BlockSpec((bt, d), lambda i, e: (i, 0)),
                   pl.BlockSpec((bt, d), lambda i, e: (i, 0))],
        out_shape=[jax.ShapeDtypeStruct((t, d), F32),
                   jax.ShapeDtypeStruct((t, d), BF16)],
        scratch_shapes=[pltpu.VMEM((d + 2 * MOE_GATE_ROWS, bt), BF16),
                        pltpu.VMEM((N_GROUPS, MOE_GATE_ROWS, bt), F32),
                        pltpu.VMEM((bt, 1), F32), pltpu.VMEM((bt, 1), F32),
                        pltpu.VMEM((1, bt), F32), pltpu.VMEM((1, bt), F32),
                        pltpu.SMEM((N_GROUPS,), jnp.int32),
                        pltpu.VMEM((MOE_MAX_CHUNKS, d, n), BF16),
                        pltpu.VMEM((MOE_MAX_CHUNKS, EXPERTS_PER_GROUP, SUBLANES, n), F32),
                        pltpu.VMEM((MOE_MAX_CHUNKS, d, n), F32),
                        pltpu.VMEM((d, bt), F32)],
        compiler_params=pltpu.CompilerParams(dimension_semantics=("arbitrary", "arbitrary"),
                                             vmem_limit_bytes=MOE_VMEM_LIMIT),
        name="moe_residual_ln",
    )(xt, rwt, rb, tri, wgu_t, wd_t, lw.reshape(1, d), lb.reshape(1, d))


def _lambda_init_for(layer):
    return 0.8 - 0.6 * math.exp(-0.3 * layer)


def kernel(x, a_w_in, a_conv, a_A_log, a_dt_bias, a_norm_w, a_w_out, b_w_q, b_w_kv, b_lambda, b_subln_w, b_w_out, rel_bias, router_w, router_b, moe_w_gate, moe_w_up, moe_w_down, ln_w, ln_b):
    bsz, t, d = x.shape
    assert bsz == 1 and d == D_MODEL and t % A_CHUNK == 0
    xf = x.reshape(t, d).astype(F32)
    xb = xf.astype(BF16)

    gate_tile = min(512, t)
    tri = jnp.asarray(np.kron(np.eye(gate_tile // A_CHUNK, dtype=np.float32),
                              np.tril(np.ones((A_CHUNK, A_CHUNK), np.float32))))
    moe_tri = jnp.asarray(np.triu(np.ones((MOE_BLOCK, MOE_BLOCK), np.float32), k=1), dtype=BF16)
    rwt = router_w.T.astype(F32)
    rb = router_b.reshape(N_EXPERTS, 1).astype(F32)
    attn_tile = min(512, t)
    assert attn_tile % BIAS_BLOCK == 0
    bias_tables = attention_bias_tables(rel_bias.astype(F32))

    kb = None
    vt = None
    for l in range(DEPTH):
        if l < N_A_LAYERS:
            w_in = a_w_in[l]
            w_main = w_in[:, :A_CONV_CH + A_V].astype(BF16)
            w_ab = jnp.pad(w_in[:, A_CONV_CH + A_V:], ((0, 0), (0, LANES - 2 * A_HEADS))).astype(BF16)
            conv_pad = jnp.pad(a_conv[l].astype(F32), ((0, 0), (0, A_V)))
            par = jnp.zeros((SUBLANES, LANES), F32)
            par = par.at[0, :A_HEADS].set(a_A_log[l].astype(F32))
            par = par.at[1, :A_HEADS].set(a_dt_bias[l].astype(F32))
            proj = gdn_inproj(xb, w_main, conv_pad)
            fac, fact = gdn_gates(xb, w_ab, par, tri)
            fact3 = fact.reshape(2 * A_HEADS, t // A_CHUNK, A_CHUNK).transpose(1, 0, 2)
            u, wq, qk, kd, dec = gdn_intra(proj, fac, fact3)
            mixed = gdn_scan(u, wq, qk, kd, dec, proj, a_norm_w[l].astype(F32))
            w_out = a_w_out[l].astype(BF16)
        else:
            j = l - N_A_LAYERS
            if kb is None:
                kb = matmul_bf16(xb, b_w_kv[:, :B_QK].astype(BF16))
                vt = v_transposed_proj(xb, b_w_kv[:, B_QK:].T.astype(BF16), attn_tile)
            qb = matmul_bf16(xb, b_w_q[j].astype(BF16), scale=B_DH ** -0.5 * LOG2E)
            mixed = diff_attention(qb, kb, vt, bias_tables, b_lambda[j].astype(F32),
                                   b_subln_w[j].astype(F32), _lambda_init_for(l), tile=attn_tile)
            w_out = b_w_out[j].astype(BF16)
        xt = matmul_residual_ln_t(mixed, w_out, xf, ln_w[l, 0].astype(F32), ln_b[l, 0].astype(F32))
        wgu_t = transpose_cast(l, moe_w_gate, moe_w_up)
        wd_t = transpose_cast(l, moe_w_down)
        xf, xb = moe_residual_ln(xt, rwt, rb, moe_tri, wgu_t, wd_t,
                                 ln_w[l, 1].astype(F32), ln_b[l, 1].astype(F32))
    return xf.reshape(bsz, t, d).astype(x.dtype)
```

```python
import functools
import math

import jax
import jax.numpy as jnp
import numpy as np
from jax import lax
from jax.experimental import pallas as pl
from jax.experimental.pallas import tpu as pltpu

F32 = jnp.float32
BF16 = jnp.bfloat16

D_MODEL = 1024
DEPTH = 4
N_A_LAYERS = DEPTH // 2
A_HEADS = 8
A_DK = 128
A_DV = 128
A_QK = A_HEADS * A_DK
A_V = A_HEADS * A_DV
A_CONV = 4
A_CONV_CH = 2 * A_QK + A_V
A_CHUNK = 64
B_HEADS = 8
B_DH = 64
B_QK = B_HEADS * 2 * B_DH
REL_BUCKETS = 32
REL_MAX_DIST = 128
N_EXPERTS = 16
N_GROUPS = 4
EXPERTS_PER_GROUP = N_EXPERTS // N_GROUPS
D_EXPERT = 512
DN_ALPHA = (2 * DEPTH) ** 0.25
LN_EPS = 1e-5
RMS_EPS = 1e-6

LANES = 128
SUBLANES = 8
VMEM_LIMIT = 48 * 1024 * 1024
MASK_VALUE = -1e30

HIGHEST = lax.Precision.HIGHEST
NT_DIMS = (((1,), (1,)), ((), ()))
TN_DIMS = (((0,), (0,)), ((), ()))


def _params(*sem):
    return pltpu.CompilerParams(dimension_semantics=sem, vmem_limit_bytes=VMEM_LIMIT)


def _sigmoid(x):
    return 1.0 / (1.0 + jnp.exp(-x))


def _silu(x):
    return x * _sigmoid(x)


def _layer_norm(v, w, b):
    mu = jnp.mean(v, axis=-1, keepdims=True)
    c = v - mu
    var = jnp.mean(c * c, axis=-1, keepdims=True)
    return c * lax.rsqrt(var + LN_EPS) * w + b


def _mm_kernel(a_ref, w_ref, o_ref, *, scale):
    y = jnp.dot(a_ref[...], w_ref[...], preferred_element_type=F32)
    if scale != 1.0:
        y = y * scale
    o_ref[...] = y.astype(o_ref.dtype)


def matmul_bf16(a, w, *, scale=1.0, tm=512, tn=1024):
    m, k = a.shape
    n = w.shape[1]
    tm = min(tm, m)
    return pl.pallas_call(
        functools.partial(_mm_kernel, scale=scale),
        grid=(n // tn, m // tm),
        in_specs=[pl.BlockSpec((tm, k), lambda j, i: (i, 0)),
                  pl.BlockSpec((k, tn), lambda j, i: (0, j))],
        out_specs=pl.BlockSpec((tm, tn), lambda j, i: (i, j)),
        out_shape=jax.ShapeDtypeStruct((m, n), BF16),
        compiler_params=_params("parallel", "arbitrary"),
        name="matmul_bf16",
    )(a, w)


def _mm_ln_kernel(a_ref, w_ref, x_ref, lw_ref, lb_ref, ot_ref):
    h = jnp.dot(a_ref[...], w_ref[...], preferred_element_type=F32)
    y = _layer_norm(DN_ALPHA * x_ref[...] + h, lw_ref[...], lb_ref[...])
    ot_ref[...] = y.T


def matmul_residual_ln_t(a, w, x, lw, lb, *, tm=512):
    m, k = a.shape
    n = w.shape[1]
    tm = min(tm, m)
    return pl.pallas_call(
        _mm_ln_kernel,
        grid=(m // tm,),
        in_specs=[pl.BlockSpec((tm, k), lambda i: (i, 0)),
                  pl.BlockSpec((k, n), lambda i: (0, 0)),
                  pl.BlockSpec((tm, n), lambda i: (i, 0)),
                  pl.BlockSpec((1, n), lambda i: (0, 0)),
                  pl.BlockSpec((1, n), lambda i: (0, 0))],
        out_specs=pl.BlockSpec((n, tm), lambda i: (0, i)),
        out_shape=jax.ShapeDtypeStruct((n, m), F32),
        compiler_params=_params("parallel"),
        name="matmul_residual_ln",
    )(a, w, x, lw.reshape(1, n), lb.reshape(1, n))


def _gdn_inproj_kernel(x_ref, w_ref, cw_ref, o_ref, halo_ref):
    n = pl.program_id(0)
    m = pl.program_id(1)
    tm, tn = o_ref.shape
    parts = 2
    pr = tm // parts

    @pl.when(m == 0)
    def _():
        halo_ref[...] = jnp.zeros_like(halo_ref)

    def project():
        return [jnp.dot(x_ref[p * pr:(p + 1) * pr, :], w_ref[...], preferred_element_type=F32)
                for p in range(parts)]

    def conv_silu(ys):
        cw = cw_ref[...]
        rows = lax.broadcasted_iota(jnp.int32, (SUBLANES, tn), 0)
        out = []
        halo = halo_ref[...]
        for y in ys:
            acc = y * cw[A_CONV - 1:A_CONV]
            for shift in range(1, A_CONV):
                sh = pltpu.roll(y, shift, 0)
                top = jnp.where(rows < shift, pltpu.roll(halo, shift, 0), sh[:SUBLANES])
                sh = jnp.concatenate([top, sh[SUBLANES:]], axis=0)
                acc = acc + sh * cw[A_CONV - 1 - shift:A_CONV - shift]
            halo = y[pr - SUBLANES:]
            out.append(_silu(acc))
        halo_ref[...] = halo
        return out

    @pl.when(n == 3)
    def _():
        for p, y in enumerate(project()):
            o_ref[p * pr:(p + 1) * pr, :] = y.astype(o_ref.dtype)

    @pl.when(n == 2)
    def _():
        for p, a in enumerate(conv_silu(project())):
            o_ref[p * pr:(p + 1) * pr, :] = a.astype(o_ref.dtype)

    @pl.when(n < 2)
    def _():
        post = jnp.where(n == 0, A_DK ** -0.5, 1.0).astype(F32)
        for p, a in enumerate(conv_silu(project())):
            for h in range(tn // A_DK):
                seg = a[:, h * A_DK:(h + 1) * A_DK]
                ss = jnp.sum(seg * seg, axis=-1, keepdims=True)
                o_ref[p * pr:(p + 1) * pr, h * A_DK:(h + 1) * A_DK] = (
                    seg * (lax.rsqrt(ss + RMS_EPS) * post)).astype(o_ref.dtype)


def gdn_inproj(xb, w_main, conv_w_pad, *, tm=512):
    t, k = xb.shape
    n = w_main.shape[1]
    tn = A_QK
    tm = min(tm, t)
    return pl.pallas_call(
        _gdn_inproj_kernel,
        grid=(n // tn, t // tm),
        in_specs=[pl.BlockSpec((tm, k), lambda j, i: (i, 0)),
                  pl.BlockSpec((k, tn), lambda j, i: (0, j)),
                  pl.BlockSpec((A_CONV, tn), lambda j, i: (0, j))],
        out_specs=pl.BlockSpec((tm, tn), lambda j, i: (i, j)),
        out_shape=jax.ShapeDtypeStruct((t, n), BF16),
        scratch_shapes=[pltpu.VMEM((SUBLANES, tn), F32)],
        compiler_params=_params("arbitrary", "arbitrary"),
        name="gdn_inproj",
    )(xb, w_main, conv_w_pad)


def _gdn_gates_kernel(x_ref, w_ref, par_ref, tri_ref, fac_ref, fact_ref):
    ab = jnp.dot(x_ref[...], w_ref[...], preferred_element_type=F32)
    lane = lax.broadcasted_iota(jnp.int32, ab.shape, 1)
    neg_a = -jnp.exp(par_ref[0:1, :])
    z = ab + par_ref[1:2, :]
    softplus = jnp.maximum(z, 0.0) + jnp.log1p(jnp.exp(-jnp.abs(z)))
    g = jnp.where(lane < A_HEADS, neg_a * softplus, 0.0)
    beta = _sigmoid(ab)
    gc = jnp.dot(tri_ref[...], g, preferred_element_type=F32, precision=HIGHEST)
    fac = jnp.where(lane < A_HEADS, gc, jnp.where(lane < 2 * A_HEADS, beta, 0.0))
    fac_ref[...] = fac
    fact_ref[...] = fac.T[:2 * A_HEADS, :]


def gdn_gates(xb, w_ab_pad, par, tri, *, tm=512):
    t, k = xb.shape
    tm = min(tm, t)
    return pl.pallas_call(
        _gdn_gates_kernel,
        grid=(t // tm,),
        in_specs=[pl.BlockSpec((tm, k), lambda i: (i, 0)),
                  pl.BlockSpec((k, LANES), lambda i: (0, 0)),
                  pl.BlockSpec((SUBLANES, LANES), lambda i: (0, 0)),
                  pl.BlockSpec((tm, tm), lambda i: (0, 0))],
        out_specs=[pl.BlockSpec((tm, LANES), lambda i: (i, 0)),
                   pl.BlockSpec((2 * A_HEADS, tm), lambda i: (0, i))],
        out_shape=[jax.ShapeDtypeStruct((t, LANES), F32),
                   jax.ShapeDtypeStruct((2 * A_HEADS, t), F32)],
        compiler_params=_params("parallel"),
        name="gdn_gates",
    )(xb, w_ab_pad, par, tri)


GDN_INTRA_CHUNKS = 2
GDN_SCAN_CHUNKS = 8


def _mm_bf16(a, b):
    return jnp.dot(a.astype(BF16), b.astype(BF16), preferred_element_type=F32)


def _gdn_intra_kernel(q_ref, k_ref, v_ref, fac_ref, fact_ref, u_ref, wq_ref, qk_ref, kd_ref, dec_ref):
    c = A_CHUNK
    ii = lax.broadcasted_iota(jnp.int32, (c, c), 0)
    jj = lax.broadcasted_iota(jnp.int32, (c, c), 1)
    incl = ii >= jj
    strict = ii > jj
    units = [(cb, h) for cb in range(GDN_INTRA_CHUNKS) for h in range(A_HEADS)]
    rows = {cb: slice(cb * c, (cb + 1) * c) for cb in range(GDN_INTRA_CHUNKS)}
    cols = {h: slice(h * A_DK, (h + 1) * A_DK) for h in range(A_HEADS)}

    q, k, kf, beta, eg, decay = {}, {}, {}, {}, {}, {}
    for cb, h in units:
        fac = fac_ref[rows[cb], :]
        gc_col = fac[:, h:h + 1]
        gc_row = fact_ref[cb][h:h + 1, :]
        g_last = gc_col[c - 1:c, :]
        beta[cb, h] = fac[:, A_HEADS + h:A_HEADS + h + 1]
        eg[cb, h] = jnp.exp(gc_col)
        decay[cb, h] = jnp.exp(jnp.where(incl, gc_col - gc_row, MASK_VALUE))
        q[cb, h] = q_ref[rows[cb], cols[h]]
        k[cb, h] = k_ref[rows[cb], cols[h]]
        kf[cb, h] = k[cb, h].astype(F32)
        kd_ref[rows[cb], cols[h]] = (kf[cb, h] * jnp.exp(g_last - gc_col)).astype(kd_ref.dtype)
        dec_ref[cb, h:h + 1, :] = jnp.broadcast_to(jnp.exp(g_last), (1, A_DV))
        wq_ref[cb, c:, cols[h]] = (q[cb, h].astype(F32) * eg[cb, h]).astype(wq_ref.dtype)

    mmat, x, p = {}, {}, {}
    for u_ in units:
        kk = lax.dot_general(k[u_], k[u_], NT_DIMS, preferred_element_type=F32)
        mmat[u_] = jnp.where(strict, beta[u_] * kk * decay[u_], 0.0)
    for u_ in units:
        cb, h = u_
        qk = lax.dot_general(q[u_], k[u_], NT_DIMS, preferred_element_type=F32) * decay[u_]
        qk_ref[cb, h] = qk.astype(qk_ref.dtype)
    for u_ in units:
        x[u_] = -mmat[u_]
        p[u_] = _mm_bf16(mmat[u_], mmat[u_])
    for level in range(5):
        for u_ in units:
            x[u_] = x[u_] + p[u_] + _mm_bf16(x[u_], p[u_])
        if level < 4:
            for u_ in units:
                p[u_] = _mm_bf16(p[u_], p[u_])
    for u_ in units:
        cb, h = u_
        v = v_ref[rows[cb], cols[h]].astype(F32)
        rhs = jnp.concatenate([beta[u_] * v, (beta[u_] * eg[u_]) * kf[u_]], axis=1)
        uw = rhs + _mm_bf16(x[u_], rhs)
        u_ref[rows[cb], cols[h]] = uw[:, :A_DV].astype(u_ref.dtype)
        wq_ref[cb, :c, cols[h]] = uw[:, A_DV:].astype(wq_ref.dtype)


def gdn_intra(proj, fac, fact3):
    t = proj.shape[0]
    c = A_CHUNK
    cb = GDN_INTRA_CHUNKS
    nc = t // c
    return pl.pallas_call(
        _gdn_intra_kernel,
        grid=(nc // cb,),
        in_specs=[pl.BlockSpec((cb * c, A_QK), lambda i: (i, 0)),
                  pl.BlockSpec((cb * c, A_QK), lambda i: (i, 1)),
                  pl.BlockSpec((cb * c, A_V), lambda i: (i, 2)),
                  pl.BlockSpec((cb * c, LANES), lambda i: (i, 0)),
                  pl.BlockSpec((cb, 2 * A_HEADS, c), lambda i: (i, 0, 0))],
        out_specs=[pl.BlockSpec((cb * c, A_V), lambda i: (i, 0)),
                   pl.BlockSpec((cb, 2 * c, A_QK), lambda i: (i, 0, 0)),
                   pl.BlockSpec((cb, A_HEADS, c, c), lambda i: (i, 0, 0, 0)),
                   pl.BlockSpec((cb * c, A_QK), lambda i: (i, 0)),
                   pl.BlockSpec((cb, A_HEADS, A_DV), lambda i: (i, 0, 0))],
        out_shape=[jax.ShapeDtypeStruct((t, A_V), BF16),
                   jax.ShapeDtypeStruct((nc, 2 * c, A_QK), BF16),
                   jax.ShapeDtypeStruct((nc, A_HEADS, c, c), BF16),
                   jax.ShapeDtypeStruct((t, A_QK), BF16),
                   jax.ShapeDtypeStruct((nc, A_HEADS, A_DV), F32)],
        compiler_params=_params("parallel"),
        name="gdn_intra",
    )(proj, proj, proj, fac, fact3)


def _gdn_scan_kernel(u_ref, wq_ref, qk_ref, kd_ref, dec_ref, z_ref, nw_ref, o_ref, s_ref):
    c = A_CHUNK

    @pl.when(pl.program_id(0) == 0)
    def _():
        s_ref[...] = jnp.zeros_like(s_ref)

    nw = nw_ref[...]
    heads = range(A_HEADS)
    cols = [slice(h * A_DK, (h + 1) * A_DK) for h in heads]
    for cb in range(GDN_SCAN_CHUNKS):
        rows = slice(cb * c, (cb + 1) * c)
        s = [s_ref[h] for h in heads]
        sb = [s[h].astype(BF16) for h in heads]
        ws_qs = [jnp.dot(wq_ref[cb, :, cols[h]], sb[h], preferred_element_type=F32) for h in heads]
        v_new = [(u_ref[rows, cols[h]].astype(F32) - ws_qs[h][:c]).astype(BF16) for h in heads]
        o = [ws_qs[h][c:] + jnp.dot(qk_ref[cb, h], v_new[h], preferred_element_type=F32) for h in heads]
        for h in heads:
            s_ref[h] = s[h] * dec_ref[cb, h:h + 1, :] + lax.dot_general(
                kd_ref[rows, cols[h]], v_new[h], TN_DIMS, preferred_element_type=F32)
        for h in heads:
            z = z_ref[rows, cols[h]].astype(F32)
            ms = jnp.mean(o[h] * o[h], axis=-1, keepdims=True)
            o_ref[rows, cols[h]] = (o[h] * lax.rsqrt(ms + RMS_EPS) * nw * _silu(z)).astype(o_ref.dtype)


def gdn_scan(u, wq, qk, kd, dec, proj, norm_w):
    t = u.shape[0]
    c = A_CHUNK
    cb = GDN_SCAN_CHUNKS
    return pl.pallas_call(
        _gdn_scan_kernel,
        grid=(t // (cb * c),),
        in_specs=[pl.BlockSpec((cb * c, A_V), lambda i: (i, 0)),
                  pl.BlockSpec((cb, 2 * c, A_QK), lambda i: (i, 0, 0)),
                  pl.BlockSpec((cb, A_HEADS, c, c), lambda i: (i, 0, 0, 0)),
                  pl.BlockSpec((cb * c, A_QK), lambda i: (i, 0)),
                  pl.BlockSpec((cb, A_HEADS, A_DV), lambda i: (i, 0, 0)),
                  pl.BlockSpec((cb * c, A_V), lambda i: (i, 3)),
                  pl.BlockSpec((1, A_DV), lambda i: (0, 0))],
        out_specs=pl.BlockSpec((cb * c, A_V), lambda i: (i, 0)),
        out_shape=jax.ShapeDtypeStruct((t, A_V), BF16),
        scratch_shapes=[pltpu.VMEM((A_HEADS, A_DK, A_DV), F32)],
        compiler_params=_params("arbitrary"),
        name="gdn_scan",
    )(u, wq, qk, kd, dec, proj, norm_w.reshape(1, A_DV))


LOG2E = math.log2(math.e)
VT_ROWS = 2 * B_DH + 16


def _vt_proj_kernel(x_ref, w_ref, o_ref):
    tk = x_ref.shape[0]
    y = lax.dot_general(w_ref[...], x_ref[...], NT_DIMS, preferred_element_type=F32)
    ones = jnp.ones((VT_ROWS - 2 * B_DH, tk), o_ref.dtype)
    for h in range(B_HEADS):
        o_ref[0, h, :2 * B_DH, :] = y[h * 2 * B_DH:(h + 1) * 2 * B_DH].astype(o_ref.dtype)
        o_ref[0, h, 2 * B_DH:, :] = ones


def v_transposed_proj(xb, w_vt, tile):
    t, k = xb.shape
    n = w_vt.shape[0]
    return pl.pallas_call(
        _vt_proj_kernel,
        grid=(t // tile,),
        in_specs=[pl.BlockSpec((tile, k), lambda i: (i, 0)),
                  pl.BlockSpec((n, k), lambda i: (0, 0))],
        out_specs=pl.BlockSpec((1, B_HEADS, VT_ROWS, tile), lambda i: (i, 0, 0, 0)),
        out_shape=jax.ShapeDtypeStruct((t // tile, B_HEADS, VT_ROWS, tile), BF16),
        compiler_params=_params("parallel"),
        name="v_transposed_proj",
    )(xb, w_vt)


ATTN_HEADS_PER_STEP = 2
BIAS_BLOCK = REL_MAX_DIST


def _attn_kernel(q_ref, k_ref, vt_ref, tab_ref, lam_ref, swb_ref, o_ref, m_ref, acc_ref,
                 st_ref, mt_ref, bias_ref, *, lambda_init, tile):
    qi = pl.program_id(1)
    dv = 2 * B_DH
    nh = ATTN_HEADS_PER_STEP
    first_map = lax.broadcasted_iota(jnp.int32, (tile, dv), 1) < B_DH

    @pl.when(qi == 0)
    def _():
        nblk = tile // BIAS_BLOCK
        zeros = jnp.zeros((BIAS_BLOCK, BIAS_BLOCK), F32)
        masked = jnp.full((BIAS_BLOCK, BIAS_BLOCK), MASK_VALUE, F32)
        for h in range(nh):
            for ja in range(nblk):
                for ib in range(nblk):
                    rows = slice(ja * BIAS_BLOCK, (ja + 1) * BIAS_BLOCK)
                    cols = slice(ib * BIAS_BLOCK, (ib + 1) * BIAS_BLOCK)
                    if ib == ja:
                        blk = tab_ref[h, 0]
                    elif ib == ja + 1:
                        blk = tab_ref[h, 1]
                    else:
                        blk = zeros if ib > ja else masked
                    bias_ref[h, 0, rows, cols] = blk
                    bias_ref[h, 1, rows, cols] = tab_ref[h, 1] if (ja == nblk - 1 and ib == 0) else zeros

    m_ref[...] = jnp.full_like(m_ref, MASK_VALUE)
    acc_ref[...] = jnp.zeros_like(acc_ref)

    def produce(h, j):
        hc = slice(h * dv, (h + 1) * dv)
        q = q_ref[:, hc]
        k = k_ref[pl.ds(pl.multiple_of(j * tile, tile), tile), hc]
        zero = jnp.zeros_like(k)
        k2 = jnp.concatenate([jnp.where(first_map, k, zero), jnp.where(first_map, zero, k)], axis=0)
        st = lax.dot_general(k2, q, NT_DIMS, preferred_element_type=F32)
        st_ref[h] = st
        for i in range(2):
            mt_ref[h, i] = jnp.max(st[i * tile:(i + 1) * tile], axis=0, keepdims=True)

    def consume(h, j, slot):
        vt = vt_ref[j, h]
        for i in range(2):
            s = st_ref[h, i * tile:(i + 1) * tile, :]
            m_prev = m_ref[h, i]
            if slot is None:
                m_new = jnp.maximum(m_prev, mt_ref[h, i])
            else:
                s = s + bias_ref[h, slot]
                m_new = jnp.maximum(m_prev, jnp.max(s, axis=0, keepdims=True))
            alpha = jnp.exp2(m_prev - m_new)
            p = jnp.exp2(s - m_new).astype(BF16)
            acc_ref[h, i] = alpha * acc_ref[h, i] + jnp.dot(vt, p, preferred_element_type=F32)
            m_ref[h, i] = m_new

    n_far = jnp.maximum(qi - 1, 0)
    produce(0, 0)

    def far_units(j):
        produce(1, j)
        consume(0, j, None)
        produce(0, j + 1)
        consume(1, j, None)

    def far_pair_body(jj, carry):
        far_units(2 * jj)
        far_units(2 * jj + 1)
        return carry

    lax.fori_loop(0, n_far // 2, far_pair_body, 0)

    @pl.when(n_far % 2 == 1)
    def _():
        far_units(n_far - 1)

    @pl.when(qi >= 1)
    def _():
        produce(1, qi - 1)
        consume(0, qi - 1, 1)
        produce(0, qi)
        consume(1, qi - 1, 1)
        produce(1, qi)
        consume(0, qi, 0)
        consume(1, qi, 0)

    @pl.when(qi == 0)
    def _():
        produce(1, 0)
        consume(0, 0, 0)
        consume(1, 0, 0)

    lam = lam_ref[...]
    lam_val = (jnp.exp(jnp.sum(lam[0:1] * lam[1:2], axis=-1, keepdims=True))
               - jnp.exp(jnp.sum(lam[2:3] * lam[3:4], axis=-1, keepdims=True)) + lambda_init)
    for h in range(nh):
        a0 = acc_ref[h, 0]
        a1 = acc_ref[h, 1]
        ot = a0[:dv] / a0[dv:dv + 1] - lam_val * (a1[:dv] / a1[dv:dv + 1])
        ms = jnp.mean(ot * ot, axis=0, keepdims=True)
        ot = ot * lax.rsqrt(ms + RMS_EPS) * swb_ref[...] * (1.0 - lambda_init)
        o_ref[:, h * dv:(h + 1) * dv] = ot.T.astype(o_ref.dtype)


def diff_attention(qb, kb, vt, bias_tables, lam, subln_w, lambda_init, *, tile):
    t = qb.shape[0]
    nb = t // tile
    dv = 2 * B_DH
    nh = ATTN_HEADS_PER_STEP
    resident = pl.Buffered(1)
    swb = jnp.broadcast_to(subln_w.reshape(dv, 1), (dv, tile))
    return pl.pallas_call(
        functools.partial(_attn_kernel, lambda_init=lambda_init, tile=tile),
        grid=(B_HEADS // nh, nb),
        in_specs=[pl.BlockSpec((tile, nh * dv), lambda h, i: (i, h)),
                  pl.BlockSpec((t, nh * dv), lambda h, i: (0, h), pipeline_mode=resident),
                  pl.BlockSpec((nb, nh, VT_ROWS, tile), lambda h, i: (0, h, 0, 0), pipeline_mode=resident),
                  pl.BlockSpec((nh, 2, BIAS_BLOCK, BIAS_BLOCK), lambda h, i: (h, 0, 0, 0)),
                  pl.BlockSpec((4, B_DH), lambda h, i: (0, 0)),
                  pl.BlockSpec((dv, tile), lambda h, i: (0, 0))],
        out_specs=pl.BlockSpec((tile, nh * dv), lambda h, i: (i, h)),
        out_shape=jax.ShapeDtypeStruct((t, B_HEADS * dv), BF16),
        scratch_shapes=[pltpu.VMEM((nh, 2, 1, tile), F32),
                        pltpu.VMEM((nh, 2, VT_ROWS, tile), F32),
                        pltpu.VMEM((nh, 2 * tile, tile), F32),
                        pltpu.VMEM((nh, 2, 1, tile), F32),
                        pltpu.VMEM((nh, 2, tile, tile), F32)],
        compiler_params=_params("arbitrary", "arbitrary"),
        name="diff_attention",
    )(qb, kb, vt, bias_tables, lam, swb)


def _t5_bucket(n):
    max_exact = REL_BUCKETS // 2
    nf = jnp.maximum(n, 1).astype(F32)
    large = max_exact + (jnp.log(nf / max_exact) / math.log(REL_MAX_DIST / max_exact)
                         * (REL_BUCKETS - max_exact)).astype(jnp.int32)
    large = jnp.minimum(large, REL_BUCKETS - 1)
    return jnp.where(n < max_exact, n, large)


def attention_bias_tables(rel_bias):
    tile = BIAS_BLOCK
    n = jnp.arange(2 * tile)
    far = rel_bias[REL_BUCKETS - 1]
    db = ((rel_bias[_t5_bucket(n)] - far[None, :]) * LOG2E).T.astype(F32)
    nh = db.shape[0]
    full = jnp.concatenate([jnp.full((nh, tile - 1), MASK_VALUE, F32), db], axis=1)

    def toeplitz(vec):
        w = jnp.concatenate([vec, jnp.zeros((nh, 1), F32)], axis=1)
        flat = jnp.tile(w, (1, tile))[:, :tile * (2 * tile - 1)]
        return flat.reshape(nh, tile, 2 * tile - 1)[:, :, tile - 1:]

    diag = toeplitz(full[:, :2 * tile - 1])
    near = toeplitz(full[:, tile:3 * tile - 1])
    return jnp.stack([diag, near], axis=1)


def _route(logits):
    mx = jnp.max(logits, axis=0, keepdims=True)
    ex = jnp.exp(logits - mx)
    probs = ex / jnp.sum(ex, axis=0, keepdims=True)
    row = [probs[e:e + 1, :] for e in range(N_EXPERTS)]
    epg = EXPERTS_PER_GROUP
    best_sum = None
    best_group = None
    for g in range(N_GROUPS):
        vals = row[g * epg:(g + 1) * epg]
        top2 = None
        for a in range(epg):
            for b in range(a + 1, epg):
                pair = vals[a] + vals[b]
                top2 = pair if top2 is None else jnp.maximum(top2, pair)
        if g == 0:
            best_sum = top2
            best_group = jnp.zeros_like(top2, dtype=jnp.int32)
        else:
            upd = top2 > best_sum
            best_sum = jnp.where(upd, top2, best_sum)
            best_group = jnp.where(upd, g, best_group)
    gp = []
    for j in range(epg):
        sel = row[j]
        for g in range(1, N_GROUPS):
            sel = jnp.where(best_group == g, row[g * epg + j], sel)
        gp.append(sel)
    v1 = gp[0]
    i1 = jnp.zeros_like(best_group)
    for j in range(1, epg):
        upd = gp[j] > v1
        v1 = jnp.where(upd, gp[j], v1)
        i1 = jnp.where(upd, j, i1)
    v2 = jnp.full_like(v1, -1.0)
    i2 = jnp.zeros_like(best_group)
    for j in range(epg):
        upd = (i1 != j) & (gp[j] > v2)
        v2 = jnp.where(upd, gp[j], v2)
        i2 = jnp.where(upd, j, i2)
    denom = v1 + v2
    e1 = best_group * epg + i1
    e2 = best_group * epg + i2
    eidx = lax.broadcasted_iota(jnp.int32, logits.shape, 0)
    gates = jnp.where(eidx == e1, v1 / denom, 0.0) + jnp.where(eidx == e2, v2 / denom, 0.0)
    return gates, best_group


MOE_BLOCK = 896
MOE_TAIL_BLOCK = 1024
MOE_CHUNK = 256
MOE_SLAB = LANES
MOE_GATE_ROWS = 16
MOE_VMEM_LIMIT = 56 * 1024 * 1024


def _moe_kernel(xt_ref, rwt_ref, rb_ref, tri_ref, wgu_ref, wd_ref, lw_ref, lb_ref, o_hbm, ob_hbm,
                o_ref, ob_ref,
                xg_ref, gates_ref, grpc_ref, keyc_ref, grpr_ref, keyr_ref, cnt_ref,
                xs_ref, gw_ref, ych_ref, yacc_ref):
    del o_hbm, ob_hbm
    e = pl.program_id(1)
    g = e // EXPERTS_PER_GROUP
    j = e % EXPERTS_PER_GROUP
    d, bt = xt_ref.shape
    n = MOE_CHUNK
    gr = MOE_GATE_ROWS
    max_chunks = xs_ref.shape[0]

    @pl.when(e == 0)
    def _route_block():
        xt = xt_ref[...]
        xg_ref[:d, :] = xt.astype(BF16)
        logits = jnp.dot(rwt_ref[...], xt, preferred_element_type=F32, precision=HIGHEST) + rb_ref[...]
        gates, best_group = _route(logits)
        gidx = lax.broadcasted_iota(jnp.int32, (SUBLANES, bt), 0)
        onehot = jnp.where(gidx == best_group, 1.0, 0.0)
        rank = jnp.dot(onehot.astype(BF16), tri_ref[...], preferred_element_type=F32)
        key = jnp.sum(onehot * rank, axis=0, keepdims=True)
        grp_f = best_group.astype(F32)
        grpr_ref[...] = grp_f
        keyr_ref[...] = key
        stack = jnp.concatenate([grp_f, key, jnp.zeros((LANES - 2, bt), F32)], axis=0)
        col = stack.T
        grpc_ref[...] = col[:, 0:1]
        keyc_ref[...] = col[:, 1:2]
        for gg in range(N_GROUPS):
            cnt_ref[gg] = jnp.sum(onehot[gg:gg + 1, :]).astype(jnp.int32)
            gates_ref[gg] = jnp.concatenate(
                [gates[gg * EXPERTS_PER_GROUP:(gg + 1) * EXPERTS_PER_GROUP],
                 jnp.zeros((gr - EXPERTS_PER_GROUP, bt), F32)], axis=0)
        yacc_ref[...] = jnp.zeros_like(yacc_ref)

    count = cnt_ref[g]
    g_f = g.astype(F32)

    @pl.when(j == 0)
    def _gather_group():
        gt = gates_ref[g]
        hi = gt.astype(BF16)
        xg_ref[d:d + gr, :] = hi
        xg_ref[d + gr:d + 2 * gr, :] = (gt - hi.astype(F32)).astype(BF16)
        key_mine = jnp.where(grpc_ref[...] == g_f, keyc_ref[...], -1.0)
        lane = lax.broadcasted_iota(jnp.int32, (bt, n), 1).astype(F32)
        for c in range(max_chunks):
            @pl.when(c * n < count)
            def _():
                selt = jnp.where(key_mine - float(c * n) == lane, 1.0, 0.0).astype(BF16)
                gx = jnp.dot(xg_ref[...], selt, preferred_element_type=F32)
                xs_ref[c] = gx[:d].astype(BF16)
                gw = gx[d:d + gr] + gx[d + gr:d + 2 * gr]
                for jj in range(EXPERTS_PER_GROUP):
                    gw_ref[c, jj] = jnp.broadcast_to(gw[jj:jj + 1, :], (SUBLANES, n))
                ych_ref[c] = jnp.zeros((d, n), F32)

    for c in range(max_chunks):
        @pl.when(c * n < count)
        def _():
            xs = xs_ref[c]
            gw = gw_ref[c, j][0:1, :]
            half = D_EXPERT // 2
            gu = [(jnp.dot(wgu_ref[0, s * half:(s + 1) * half, :], xs, preferred_element_type=F32),
                   jnp.dot(wgu_ref[0, D_EXPERT + s * half:D_EXPERT + (s + 1) * half, :], xs,
                           preferred_element_type=F32)) for s in range(2)]
            y = ych_ref[c]
            for s in range(2):
                hid = (_silu(gu[s][0]) * gu[s][1] * gw).astype(BF16)
                y = y + jnp.dot(wd_ref[0, :, s * half:(s + 1) * half], hid, preferred_element_type=F32)
            ych_ref[c] = y

    @pl.when(j == EXPERTS_PER_GROUP - 1)
    def _scatter_group():
        key_mine = jnp.where(grpr_ref[...] == g_f, keyr_ref[...], -1.0)
        sub = lax.broadcasted_iota(jnp.int32, (n, bt), 0).astype(F32)
        for c in range(max_chunks):
            @pl.when(c * n < count)
            def _():
                sel = jnp.where(key_mine - float(c * n) == sub, 1.0, 0.0).astype(BF16)
                yacc_ref[...] += jnp.dot(ych_ref[c].astype(BF16), sel, preferred_element_type=F32)

    @pl.when(e == N_EXPERTS - 1)
    def _finish():
        for c in range(bt // MOE_SLAB):
            cols = slice(c * MOE_SLAB, (c + 1) * MOE_SLAB)
            z = DN_ALPHA * xt_ref[:, cols] + yacc_ref[:, cols]
            y = _layer_norm(z.T, lw_ref[...], lb_ref[...])
            o_ref[cols, :] = y
            ob_ref[cols, :] = y.astype(BF16)


def _transpose_cast_kernel(*refs):
    *in_refs, o_ref = refs
    rows = 0
    for w_ref in in_refs:
        wt = w_ref[0, 0].astype(F32).T
        o_ref[0, rows:rows + wt.shape[0], :] = wt.astype(o_ref.dtype)
        rows += wt.shape[0]


def transpose_cast(layer, *ws):
    _, ne, r, c = ws[0].shape
    return pl.pallas_call(
        _transpose_cast_kernel,
        grid=(ne,),
        in_specs=[pl.BlockSpec((1, 1, r, c), lambda e: (layer, e, 0, 0)) for _ in ws],
        out_specs=pl.BlockSpec((1, len(ws) * c, r), lambda e: (e, 0, 0)),
        out_shape=jax.ShapeDtypeStruct((ne, len(ws) * c, r), BF16),
        compiler_params=_params("parallel"),
        name="transpose_cast",
    )(*ws)


def _moe_blocks(xt, rwt, rb, wgu_t, wd_t, lw, lb, out_f32, out_bf16, *, bt, first_block, num_blocks):
    d, t = xt.shape
    n = MOE_CHUNK
    max_chunks = pl.cdiv(bt, n)
    assert bt % MOE_SLAB == 0 and (first_block + num_blocks) * bt <= t
    tri = jnp.asarray(np.triu(np.ones((bt, bt), np.float32), k=1), dtype=BF16)
    return pl.pallas_call(
        _moe_kernel,
        grid=(num_blocks, N_EXPERTS),
        in_specs=[pl.BlockSpec((d, bt), lambda i, e: (0, first_block + i)),
                  pl.BlockSpec((N_EXPERTS, d), lambda i, e: (0, 0)),
                  pl.BlockSpec((N_EXPERTS, 1), lambda i, e: (0, 0)),
                  pl.BlockSpec((bt, bt), lambda i, e: (0, 0)),
                  pl.BlockSpec((1, 2 * D_EXPERT, d), lambda i, e: (e, 0, 0)),
                  pl.BlockSpec((1, d, D_EXPERT), lambda i, e: (e, 0, 0)),
                  pl.BlockSpec((1, d), lambda i, e: (0, 0)),
                  pl.BlockSpec((1, d), lambda i, e: (0, 0)),
                  pl.BlockSpec(memory_space=pl.ANY),
                  pl.BlockSpec(memory_space=pl.ANY)],
        out_specs=[pl.BlockSpec((bt, d), lambda i, e: (first_block + i, 0)),
                   pl.BlockSpec((bt, d), lambda i, e: (first_block + i, 0))],
        out_shape=[jax.ShapeDtypeStruct((t, d), F32),
                   jax.ShapeDtypeStruct((t, d), BF16)],
        input_output_aliases={8: 0, 9: 1},
        scratch_shapes=[pltpu.VMEM((d + 2 * MOE_GATE_ROWS, bt), BF16),
                        pltpu.VMEM((N_GROUPS, MOE_GATE_ROWS, bt), F32),
                        pltpu.VMEM((bt, 1), F32), pltpu.VMEM((bt, 1), F32),
                        pltpu.VMEM((1, bt), F32), pltpu.VMEM((1, bt), F32),
                        pltpu.SMEM((N_GROUPS,), jnp.int32),
                        pltpu.VMEM((max_chunks, d, n), BF16),
                        pltpu.VMEM((max_chunks, EXPERTS_PER_GROUP, SUBLANES, n), F32),
                        pltpu.VMEM((max_chunks, d, n), F32),
                        pltpu.VMEM((d, bt), F32)],
        compiler_params=pltpu.CompilerParams(dimension_semantics=("arbitrary", "arbitrary"),
                                             vmem_limit_bytes=MOE_VMEM_LIMIT),
        name="moe_residual_ln",
    )(xt, rwt, rb, tri, wgu_t, wd_t, lw.reshape(1, d), lb.reshape(1, d), out_f32, out_bf16)


def moe_residual_ln(xt, rwt, rb, wgu_t, wd_t, lw, lb, buf_f32, buf_bf16):
    d, t = xt.shape
    head_blocks = max((k for k in range(t // MOE_BLOCK + 1)
                       if (t - k * MOE_BLOCK) % MOE_TAIL_BLOCK == 0), default=0)
    tail_start = head_blocks * MOE_BLOCK
    assert (t - tail_start) % MOE_TAIL_BLOCK == 0
    out = (buf_f32, buf_bf16)
    if head_blocks:
        out = _moe_blocks(xt, rwt, rb, wgu_t, wd_t, lw, lb, *out,
                          bt=MOE_BLOCK, first_block=0, num_blocks=head_blocks)
    if tail_start < t:
        out = _moe_blocks(xt, rwt, rb, wgu_t, wd_t, lw, lb, *out, bt=MOE_TAIL_BLOCK,
                          first_block=tail_start // MOE_TAIL_BLOCK,
                          num_blocks=(t - tail_start) // MOE_TAIL_BLOCK)
    return out


def _lambda_init_for(layer):
    return 0.8 - 0.6 * math.exp(-0.3 * layer)


def kernel(x, a_w_in, a_conv, a_A_log, a_dt_bias, a_norm_w, a_w_out, b_w_q, b_w_kv, b_lambda, b_subln_w, b_w_out, rel_bias, router_w, router_b, moe_w_gate, moe_w_up, moe_w_down, ln_w, ln_b):
    bsz, t, d = x.shape
    assert bsz == 1 and d == D_MODEL and t % A_CHUNK == 0
    xf = x.reshape(t, d).astype(F32)
    xb = xf.astype(BF16)

    gate_tile = min(512, t)
    tri = jnp.asarray(np.kron(np.eye(gate_tile // A_CHUNK, dtype=np.float32),
                              np.tril(np.ones((A_CHUNK, A_CHUNK), np.float32))))
    rwt = router_w.T.astype(F32)
    rb = router_b.reshape(N_EXPERTS, 1).astype(F32)
    attn_tile = min(512, t)
    assert attn_tile % BIAS_BLOCK == 0
    bias_tables = attention_bias_tables(rel_bias.astype(F32))

    kb = None
    vt = None
    for l in range(DEPTH):
        if l < N_A_LAYERS:
            w_in = a_w_in[l]
            w_main = w_in[:, :A_CONV_CH + A_V].astype(BF16)
            w_ab = jnp.pad(w_in[:, A_CONV_CH + A_V:], ((0, 0), (0, LANES - 2 * A_HEADS))).astype(BF16)
            conv_pad = jnp.pad(a_conv[l].astype(F32), ((0, 0), (0, A_V)))
            par = jnp.zeros((SUBLANES, LANES), F32)
            par = par.at[0, :A_HEADS].set(a_A_log[l].astype(F32))
            par = par.at[1, :A_HEADS].set(a_dt_bias[l].astype(F32))
            proj = gdn_inproj(xb, w_main, conv_pad)
            fac, fact = gdn_gates(xb, w_ab, par, tri)
            fact3 = fact.reshape(2 * A_HEADS, t // A_CHUNK, A_CHUNK).transpose(1, 0, 2)
            u, wq, qk, kd, dec = gdn_intra(proj, fac, fact3)
            mixed = gdn_scan(u, wq, qk, kd, dec, proj, a_norm_w[l].astype(F32))
            w_out = a_w_out[l].astype(BF16)
        else:
            j = l - N_A_LAYERS
            if kb is None:
                kb = matmul_bf16(xb, b_w_kv[:, :B_QK].astype(BF16))
                vt = v_transposed_proj(xb, b_w_kv[:, B_QK:].T.astype(BF16), attn_tile)
            qb = matmul_bf16(xb, b_w_q[j].astype(BF16), scale=B_DH ** -0.5 * LOG2E)
            mixed = diff_attention(qb, kb, vt, bias_tables, b_lambda[j].astype(F32),
                                   b_subln_w[j].astype(F32), _lambda_init_for(l), tile=attn_tile)
            w_out = b_w_out[j].astype(BF16)
        xt = matmul_residual_ln_t(mixed, w_out, xf, ln_w[l, 0].astype(F32), ln_b[l, 0].astype(F32))
        wgu_t = transpose_cast(l, moe_w_gate, moe_w_up)
        wd_t = transpose_cast(l, moe_w_down)
        xf, xb = moe_residual_ln(xt, rwt, rb, wgu_t, wd_t,
                                 ln_w[l, 1].astype(F32), ln_b[l, 1].astype(F32), xf, xb)
    return xf.reshape(bsz, t, d).astype(x.dtype)
```

```python
import functools
import math

import jax
import jax.numpy as jnp
import numpy as np
from jax import lax
from jax.experimental import pallas as pl
from jax.experimental.pallas import tpu as pltpu

F32 = jnp.float32
BF16 = jnp.bfloat16

D_MODEL = 1024
DEPTH = 4
N_A_LAYERS = DEPTH // 2
A_HEADS = 8
A_DK = 128
A_DV = 128
A_QK = A_HEADS * A_DK
A_V = A_HEADS * A_DV
A_CONV = 4
A_CONV_CH = 2 * A_QK + A_V
A_CHUNK = 64
B_HEADS = 8
B_DH = 64
B_QK = B_HEADS * 2 * B_DH
REL_BUCKETS = 32
REL_MAX_DIST = 128
N_EXPERTS = 16
N_GROUPS = 4
EXPERTS_PER_GROUP = N_EXPERTS // N_GROUPS
D_EXPERT = 512
DN_ALPHA = (2 * DEPTH) ** 0.25
LN_EPS = 1e-5
RMS_EPS = 1e-6

LANES = 128
SUBLANES = 8
VMEM_LIMIT = 48 * 1024 * 1024
MASK_VALUE = -1e30

HIGHEST = lax.Precision.HIGHEST
NT_DIMS = (((1,), (1,)), ((), ()))
TN_DIMS = (((0,), (0,)), ((), ()))


def _params(*sem):
    return pltpu.CompilerParams(dimension_semantics=sem, vmem_limit_bytes=VMEM_LIMIT)


def _sigmoid(x):
    return 1.0 / (1.0 + jnp.exp(-x))


def _silu(x):
    return x * _sigmoid(x)


def _layer_norm(v, w, b):
    mu = jnp.mean(v, axis=-1, keepdims=True)
    c = v - mu
    var = jnp.mean(c * c, axis=-1, keepdims=True)
    return c * lax.rsqrt(var + LN_EPS) * w + b


def _mm_kernel(a_ref, w_ref, o_ref, *, scale):
    y = jnp.dot(a_ref[...], w_ref[...], preferred_element_type=F32)
    if scale != 1.0:
        y = y * scale
    o_ref[...] = y.astype(o_ref.dtype)


def matmul_bf16(a, w, *, scale=1.0, tm=512, tn=1024):
    m, k = a.shape
    n = w.shape[1]
    tm = min(tm, m)
    return pl.pallas_call(
        functools.partial(_mm_kernel, scale=scale),
        grid=(n // tn, m // tm),
        in_specs=[pl.BlockSpec((tm, k), lambda j, i: (i, 0)),
                  pl.BlockSpec((k, tn), lambda j, i: (0, j))],
        out_specs=pl.BlockSpec((tm, tn), lambda j, i: (i, j)),
        out_shape=jax.ShapeDtypeStruct((m, n), BF16),
        compiler_params=_params("parallel", "arbitrary"),
        name="matmul_bf16",
    )(a, w)


def _mm_ln_kernel(a_ref, w_ref, x_ref, lw_ref, lb_ref, ot_ref):
    tm = a_ref.shape[0]
    parts = 2
    pr = tm // parts
    hs = [jnp.dot(a_ref[p * pr:(p + 1) * pr, :], w_ref[...], preferred_element_type=F32) for p in range(parts)]
    for p, h in enumerate(hs):
        y = _layer_norm(DN_ALPHA * x_ref[p * pr:(p + 1) * pr, :] + h, lw_ref[...], lb_ref[...])
        ot_ref[:, p * pr:(p + 1) * pr] = y.T


def matmul_residual_ln_t(a, w, x, lw, lb, *, tm=512):
    m, k = a.shape
    n = w.shape[1]
    tm = min(tm, m)
    return pl.pallas_call(
        _mm_ln_kernel,
        grid=(m // tm,),
        in_specs=[pl.BlockSpec((tm, k), lambda i: (i, 0)),
                  pl.BlockSpec((k, n), lambda i: (0, 0)),
                  pl.BlockSpec((tm, n), lambda i: (i, 0)),
                  pl.BlockSpec((1, n), lambda i: (0, 0)),
                  pl.BlockSpec((1, n), lambda i: (0, 0))],
        out_specs=pl.BlockSpec((n, tm), lambda i: (0, i)),
        out_shape=jax.ShapeDtypeStruct((n, m), F32),
        compiler_params=_params("parallel"),
        name="matmul_residual_ln",
    )(a, w, x, lw.reshape(1, n), lb.reshape(1, n))


def _gdn_inproj_kernel(x_ref, w_ref, cw_ref, o_ref, halo_ref):
    n = pl.program_id(0)
    m = pl.program_id(1)
    tm, tn = o_ref.shape
    parts = 2
    pr = tm // parts

    @pl.when(m == 0)
    def _():
        halo_ref[...] = jnp.zeros_like(halo_ref)

    def project():
        return [jnp.dot(x_ref[p * pr:(p + 1) * pr, :], w_ref[...], preferred_element_type=F32)
                for p in range(parts)]

    def conv_silu(ys):
        cw = cw_ref[...]
        rows = lax.broadcasted_iota(jnp.int32, (SUBLANES, tn), 0)
        out = []
        halo = halo_ref[...]
        for y in ys:
            acc = y * cw[A_CONV - 1:A_CONV]
            for shift in range(1, A_CONV):
                sh = pltpu.roll(y, shift, 0)
                top = jnp.where(rows < shift, pltpu.roll(halo, shift, 0), sh[:SUBLANES])
                sh = jnp.concatenate([top, sh[SUBLANES:]], axis=0)
                acc = acc + sh * cw[A_CONV - 1 - shift:A_CONV - shift]
            halo = y[pr - SUBLANES:]
            out.append(_silu(acc))
        halo_ref[...] = halo
        return out

    @pl.when(n == 3)
    def _():
        for p, y in enumerate(project()):
            o_ref[p * pr:(p + 1) * pr, :] = y.astype(o_ref.dtype)

    @pl.when(n == 2)
    def _():
        for p, a in enumerate(conv_silu(project())):
            o_ref[p * pr:(p + 1) * pr, :] = a.astype(o_ref.dtype)

    @pl.when(n < 2)
    def _():
        post = jnp.where(n == 0, A_DK ** -0.5, 1.0).astype(F32)
        for p, a in enumerate(conv_silu(project())):
            for h in range(tn // A_DK):
                seg = a[:, h * A_DK:(h + 1) * A_DK]
                ss = jnp.sum(seg * seg, axis=-1, keepdims=True)
                o_ref[p * pr:(p + 1) * pr, h * A_DK:(h + 1) * A_DK] = (
                    seg * (lax.rsqrt(ss + RMS_EPS) * post)).astype(o_ref.dtype)


def gdn_inproj(xb, w_main, conv_w_pad, *, tm=512):
    t, k = xb.shape
    n = w_main.shape[1]
    tn = A_QK
    tm = min(tm, t)
    return pl.pallas_call(
        _gdn_inproj_kernel,
        grid=(n // tn, t // tm),
        in_specs=[pl.BlockSpec((tm, k), lambda j, i: (i, 0)),
                  pl.BlockSpec((k, tn), lambda j, i: (0, j)),
                  pl.BlockSpec((A_CONV, tn), lambda j, i: (0, j))],
        out_specs=pl.BlockSpec((tm, tn), lambda j, i: (i, j)),
        out_shape=jax.ShapeDtypeStruct((t, n), BF16),
        scratch_shapes=[pltpu.VMEM((SUBLANES, tn), F32)],
        compiler_params=_params("arbitrary", "arbitrary"),
        name="gdn_inproj",
    )(xb, w_main, conv_w_pad)


def _gdn_gates_kernel(x_ref, w_ref, par_ref, tri_ref, fac_ref, fact_ref):
    ab = jnp.dot(x_ref[...], w_ref[...], preferred_element_type=F32)
    lane = lax.broadcasted_iota(jnp.int32, ab.shape, 1)
    neg_a = -jnp.exp(par_ref[0:1, :])
    z = ab + par_ref[1:2, :]
    softplus = jnp.maximum(z, 0.0) + jnp.log1p(jnp.exp(-jnp.abs(z)))
    g = jnp.where(lane < A_HEADS, neg_a * softplus, 0.0)
    beta = _sigmoid(ab)
    gc = jnp.dot(tri_ref[...], g, preferred_element_type=F32, precision=HIGHEST)
    fac = jnp.where(lane < A_HEADS, gc, jnp.where(lane < 2 * A_HEADS, beta, 0.0))
    fac_ref[...] = fac
    fact_ref[...] = fac.T[:2 * A_HEADS, :]


def gdn_gates(xb, w_ab_pad, par, tri, *, tm=512):
    t, k = xb.shape
    tm = min(tm, t)
    return pl.pallas_call(
        _gdn_gates_kernel,
        grid=(t // tm,),
        in_specs=[pl.BlockSpec((tm, k), lambda i: (i, 0)),
                  pl.BlockSpec((k, LANES), lambda i: (0, 0)),
                  pl.BlockSpec((SUBLANES, LANES), lambda i: (0, 0)),
                  pl.BlockSpec((tm, tm), lambda i: (0, 0))],
        out_specs=[pl.BlockSpec((tm, LANES), lambda i: (i, 0)),
                   pl.BlockSpec((2 * A_HEADS, tm), lambda i: (0, i))],
        out_shape=[jax.ShapeDtypeStruct((t, LANES), F32),
                   jax.ShapeDtypeStruct((2 * A_HEADS, t), F32)],
        compiler_params=_params("parallel"),
        name="gdn_gates",
    )(xb, w_ab_pad, par, tri)


GDN_INTRA_CHUNKS = 2
GDN_SCAN_CHUNKS = 8


def _mm_bf16(a, b):
    return jnp.dot(a.astype(BF16), b.astype(BF16), preferred_element_type=F32)


def _gdn_intra_kernel(q_ref, k_ref, v_ref, fac_ref, fact_ref, u_ref, wq_ref, qk_ref, kd_ref, dec_ref):
    c = A_CHUNK
    ii = lax.broadcasted_iota(jnp.int32, (c, c), 0)
    jj = lax.broadcasted_iota(jnp.int32, (c, c), 1)
    incl = ii >= jj
    strict = ii > jj
    units = [(cb, h) for cb in range(GDN_INTRA_CHUNKS) for h in range(A_HEADS)]
    rows = {cb: slice(cb * c, (cb + 1) * c) for cb in range(GDN_INTRA_CHUNKS)}
    cols = {h: slice(h * A_DK, (h + 1) * A_DK) for h in range(A_HEADS)}

    q, k, kf, beta, eg, decay = {}, {}, {}, {}, {}, {}
    for cb, h in units:
        fac = fac_ref[rows[cb], :]
        gc_col = fac[:, h:h + 1]
        gc_row = fact_ref[cb][h:h + 1, :]
        g_last = gc_col[c - 1:c, :]
        beta[cb, h] = fac[:, A_HEADS + h:A_HEADS + h + 1]
        eg[cb, h] = jnp.exp(gc_col)
        decay[cb, h] = jnp.exp(jnp.where(incl, gc_col - gc_row, MASK_VALUE))
        q[cb, h] = q_ref[rows[cb], cols[h]]
        k[cb, h] = k_ref[rows[cb], cols[h]]
        kf[cb, h] = k[cb, h].astype(F32)
        kd_ref[rows[cb], cols[h]] = (kf[cb, h] * jnp.exp(g_last - gc_col)).astype(kd_ref.dtype)
        dec_ref[cb, h:h + 1, :] = jnp.broadcast_to(jnp.exp(g_last), (1, A_DV))
        wq_ref[cb, c:, cols[h]] = (q[cb, h].astype(F32) * eg[cb, h]).astype(wq_ref.dtype)

    mmat, x, p = {}, {}, {}
    for u_ in units:
        kk = lax.dot_general(k[u_], k[u_], NT_DIMS, preferred_element_type=F32)
        mmat[u_] = jnp.where(strict, beta[u_] * kk * decay[u_], 0.0)
    for u_ in units:
        cb, h = u_
        qk = lax.dot_general(q[u_], k[u_], NT_DIMS, preferred_element_type=F32) * decay[u_]
        qk_ref[cb, h] = qk.astype(qk_ref.dtype)
    for u_ in units:
        x[u_] = -mmat[u_]
        p[u_] = _mm_bf16(mmat[u_], mmat[u_])
    for level in range(5):
        for u_ in units:
            x[u_] = x[u_] + p[u_] + _mm_bf16(x[u_], p[u_])
        if level < 4:
            for u_ in units:
                p[u_] = _mm_bf16(p[u_], p[u_])
    for u_ in units:
        cb, h = u_
        v = v_ref[rows[cb], cols[h]].astype(F32)
        rhs = jnp.concatenate([beta[u_] * v, (beta[u_] * eg[u_]) * kf[u_]], axis=1)
        uw = rhs + _mm_bf16(x[u_], rhs)
        u_ref[rows[cb], cols[h]] = uw[:, :A_DV].astype(u_ref.dtype)
        wq_ref[cb, :c, cols[h]] = uw[:, A_DV:].astype(wq_ref.dtype)


def gdn_intra(proj, fac, fact3):
    t = proj.shape[0]
    c = A_CHUNK
    cb = GDN_INTRA_CHUNKS
    nc = t // c
    return pl.pallas_call(
        _gdn_intra_kernel,
        grid=(nc // cb,),
        in_specs=[pl.BlockSpec((cb * c, A_QK), lambda i: (i, 0)),
                  pl.BlockSpec((cb * c, A_QK), lambda i: (i, 1)),
                  pl.BlockSpec((cb * c, A_V), lambda i: (i, 2)),
                  pl.BlockSpec((cb * c, LANES), lambda i: (i, 0)),
                  pl.BlockSpec((cb, 2 * A_HEADS, c), lambda i: (i, 0, 0))],
        out_specs=[pl.BlockSpec((cb * c, A_V), lambda i: (i, 0)),
                   pl.BlockSpec((cb, 2 * c, A_QK), lambda i: (i, 0, 0)),
                   pl.BlockSpec((cb, A_HEADS, c, c), lambda i: (i, 0, 0, 0)),
                   pl.BlockSpec((cb * c, A_QK), lambda i: (i, 0)),
                   pl.BlockSpec((cb, A_HEADS, A_DV), lambda i: (i, 0, 0))],
        out_shape=[jax.ShapeDtypeStruct((t, A_V), BF16),
                   jax.ShapeDtypeStruct((nc, 2 * c, A_QK), BF16),
                   jax.ShapeDtypeStruct((nc, A_HEADS, c, c), BF16),
                   jax.ShapeDtypeStruct((t, A_QK), BF16),
                   jax.ShapeDtypeStruct((nc, A_HEADS, A_DV), F32)],
        compiler_params=_params("parallel"),
        name="gdn_intra",
    )(proj, proj, proj, fac, fact3)


def _gdn_scan_kernel(u_ref, wq_ref, qk_ref, kd_ref, dec_ref, z_ref, nw_ref, o_ref, s_ref):
    c = A_CHUNK

    @pl.when(pl.program_id(0) == 0)
    def _():
        s_ref[...] = jnp.zeros_like(s_ref)

    nw = nw_ref[...]
    heads = range(A_HEADS)
    cols = [slice(h * A_DK, (h + 1) * A_DK) for h in heads]
    for cb in range(GDN_SCAN_CHUNKS):
        rows = slice(cb * c, (cb + 1) * c)
        s = [s_ref[h] for h in heads]
        sb = [s[h].astype(BF16) for h in heads]
        ws_qs = [jnp.dot(wq_ref[cb, :, cols[h]], sb[h], preferred_element_type=F32) for h in heads]
        v_new = [(u_ref[rows, cols[h]].astype(F32) - ws_qs[h][:c]).astype(BF16) for h in heads]
        o = [ws_qs[h][c:] + jnp.dot(qk_ref[cb, h], v_new[h], preferred_element_type=F32) for h in heads]
        for h in heads:
            s_ref[h] = s[h] * dec_ref[cb, h:h + 1, :] + lax.dot_general(
                kd_ref[rows, cols[h]], v_new[h], TN_DIMS, preferred_element_type=F32)
        for h in heads:
            z = z_ref[rows, cols[h]].astype(F32)
            ms = jnp.mean(o[h] * o[h], axis=-1, keepdims=True)
            o_ref[rows, cols[h]] = (o[h] * lax.rsqrt(ms + RMS_EPS) * nw * _silu(z)).astype(o_ref.dtype)


def gdn_scan(u, wq, qk, kd, dec, proj, norm_w):
    t = u.shape[0]
    c = A_CHUNK
    cb = GDN_SCAN_CHUNKS
    return pl.pallas_call(
        _gdn_scan_kernel,
        grid=(t // (cb * c),),
        in_specs=[pl.BlockSpec((cb * c, A_V), lambda i: (i, 0)),
                  pl.BlockSpec((cb, 2 * c, A_QK), lambda i: (i, 0, 0)),
                  pl.BlockSpec((cb, A_HEADS, c, c), lambda i: (i, 0, 0, 0)),
                  pl.BlockSpec((cb * c, A_QK), lambda i: (i, 0)),
                  pl.BlockSpec((cb, A_HEADS, A_DV), lambda i: (i, 0, 0)),
                  pl.BlockSpec((cb * c, A_V), lambda i: (i, 3)),
                  pl.BlockSpec((1, A_DV), lambda i: (0, 0))],
        out_specs=pl.BlockSpec((cb * c, A_V), lambda i: (i, 0)),
        out_shape=jax.ShapeDtypeStruct((t, A_V), BF16),
        scratch_shapes=[pltpu.VMEM((A_HEADS, A_DK, A_DV), F32)],
        compiler_params=_params("arbitrary"),
        name="gdn_scan",
    )(u, wq, qk, kd, dec, proj, norm_w.reshape(1, A_DV))


LOG2E = math.log2(math.e)
VT_ROWS = 2 * B_DH + 16


def _vt_proj_kernel(x_ref, w_ref, o_ref):
    tk = x_ref.shape[0]
    y = lax.dot_general(w_ref[...], x_ref[...], NT_DIMS, preferred_element_type=F32)
    ones = jnp.ones((VT_ROWS - 2 * B_DH, tk), o_ref.dtype)
    for h in range(B_HEADS):
        o_ref[0, h, :2 * B_DH, :] = y[h * 2 * B_DH:(h + 1) * 2 * B_DH].astype(o_ref.dtype)
        o_ref[0, h, 2 * B_DH:, :] = ones


def v_transposed_proj(xb, w_vt, tile):
    t, k = xb.shape
    n = w_vt.shape[0]
    return pl.pallas_call(
        _vt_proj_kernel,
        grid=(t // tile,),
        in_specs=[pl.BlockSpec((tile, k), lambda i: (i, 0)),
                  pl.BlockSpec((n, k), lambda i: (0, 0))],
        out_specs=pl.BlockSpec((1, B_HEADS, VT_ROWS, tile), lambda i: (i, 0, 0, 0)),
        out_shape=jax.ShapeDtypeStruct((t // tile, B_HEADS, VT_ROWS, tile), BF16),
        compiler_params=_params("parallel"),
        name="v_transposed_proj",
    )(xb, w_vt)


ATTN_HEADS_PER_STEP = 2
BIAS_BLOCK = REL_MAX_DIST


def _attn_kernel(q_ref, k_ref, vt_ref, tab_ref, lam_ref, swb_ref, o_ref, m_ref, acc_ref,
                 st_ref, mt_ref, bias_ref, *, lambda_init, tile):
    qi = pl.program_id(1)
    dv = 2 * B_DH
    nh = ATTN_HEADS_PER_STEP
    first_map = lax.broadcasted_iota(jnp.int32, (tile, dv), 1) < B_DH

    @pl.when(qi == 0)
    def _():
        nblk = tile // BIAS_BLOCK
        zeros = jnp.zeros((BIAS_BLOCK, BIAS_BLOCK), F32)
        masked = jnp.full((BIAS_BLOCK, BIAS_BLOCK), MASK_VALUE, F32)
        for h in range(nh):
            for ja in range(nblk):
                for ib in range(nblk):
                    rows = slice(ja * BIAS_BLOCK, (ja + 1) * BIAS_BLOCK)
                    cols = slice(ib * BIAS_BLOCK, (ib + 1) * BIAS_BLOCK)
                    if ib == ja:
                        blk = tab_ref[h, 0]
                    elif ib == ja + 1:
                        blk = tab_ref[h, 1]
                    else:
                        blk = zeros if ib > ja else masked
                    bias_ref[h, 0, rows, cols] = blk
                    bias_ref[h, 1, rows, cols] = tab_ref[h, 1] if (ja == nblk - 1 and ib == 0) else zeros

    m_ref[...] = jnp.full_like(m_ref, MASK_VALUE)
    acc_ref[...] = jnp.zeros_like(acc_ref)

    def produce(h, j):
        hc = slice(h * dv, (h + 1) * dv)
        q = q_ref[:, hc]
        k = k_ref[pl.ds(pl.multiple_of(j * tile, tile), tile), hc]
        zero = jnp.zeros_like(k)
        k2 = jnp.concatenate([jnp.where(first_map, k, zero), jnp.where(first_map, zero, k)], axis=0)
        st = lax.dot_general(k2, q, NT_DIMS, preferred_element_type=F32)
        st_ref[h] = st
        for i in range(2):
            mt_ref[h, i] = jnp.max(st[i * tile:(i + 1) * tile], axis=0, keepdims=True)

    def consume(h, j, slot):
        vt = vt_ref[j, h]
        for i in range(2):
            s = st_ref[h, i * tile:(i + 1) * tile, :]
            m_prev = m_ref[h, i]
            if slot is None:
                m_new = jnp.maximum(m_prev, mt_ref[h, i])
            else:
                s = s + bias_ref[h, slot]
                m_new = jnp.maximum(m_prev, jnp.max(s, axis=0, keepdims=True))
            alpha = jnp.exp2(m_prev - m_new)
            p = jnp.exp2(s - m_new).astype(BF16)
            acc_ref[h, i] = alpha * acc_ref[h, i] + jnp.dot(vt, p, preferred_element_type=F32)
            m_ref[h, i] = m_new

    n_far = jnp.maximum(qi - 1, 0)
    produce(0, 0)

    def far_units(j):
        produce(1, j)
        consume(0, j, None)
        produce(0, j + 1)
        consume(1, j, None)

    def far_pair_body(jj, carry):
        far_units(2 * jj)
        far_units(2 * jj + 1)
        return carry

    lax.fori_loop(0, n_far // 2, far_pair_body, 0)

    @pl.when(n_far % 2 == 1)
    def _():
        far_units(n_far - 1)

    @pl.when(qi >= 1)
    def _():
        produce(1, qi - 1)
        consume(0, qi - 1, 1)
        produce(0, qi)
        consume(1, qi - 1, 1)
        produce(1, qi)
        consume(0, qi, 0)
        consume(1, qi, 0)

    @pl.when(qi == 0)
    def _():
        produce(1, 0)
        consume(0, 0, 0)
        consume(1, 0, 0)

    lam = lam_ref[...]
    lam_val = (jnp.exp(jnp.sum(lam[0:1] * lam[1:2], axis=-1, keepdims=True))
               - jnp.exp(jnp.sum(lam[2:3] * lam[3:4], axis=-1, keepdims=True)) + lambda_init)
    for h in range(nh):
        a0 = acc_ref[h, 0]
        a1 = acc_ref[h, 1]
        ot = a0[:dv] / a0[dv:dv + 1] - lam_val * (a1[:dv] / a1[dv:dv + 1])
        ms = jnp.mean(ot * ot, axis=0, keepdims=True)
        ot = ot * lax.rsqrt(ms + RMS_EPS) * swb_ref[...] * (1.0 - lambda_init)
        o_ref[:, h * dv:(h + 1) * dv] = ot.T.astype(o_ref.dtype)


def diff_attention(qb, kb, vt, bias_tables, lam, subln_w, lambda_init, *, tile):
    t = qb.shape[0]
    nb = t // tile
    dv = 2 * B_DH
    nh = ATTN_HEADS_PER_STEP
    resident = pl.Buffered(1)
    swb = jnp.broadcast_to(subln_w.reshape(dv, 1), (dv, tile))
    return pl.pallas_call(
        functools.partial(_attn_kernel, lambda_init=lambda_init, tile=tile),
        grid=(B_HEADS // nh, nb),
        in_specs=[pl.BlockSpec((tile, nh * dv), lambda h, i: (i, h)),
                  pl.BlockSpec((t, nh * dv), lambda h, i: (0, h), pipeline_mode=resident),
                  pl.BlockSpec((nb, nh, VT_ROWS, tile), lambda h, i: (0, h, 0, 0), pipeline_mode=resident),
                  pl.BlockSpec((nh, 2, BIAS_BLOCK, BIAS_BLOCK), lambda h, i: (h, 0, 0, 0)),
                  pl.BlockSpec((4, B_DH), lambda h, i: (0, 0)),
                  pl.BlockSpec((dv, tile), lambda h, i: (0, 0))],
        out_specs=pl.BlockSpec((tile, nh * dv), lambda h, i: (i, h)),
        out_shape=jax.ShapeDtypeStruct((t, B_HEADS * dv), BF16),
        scratch_shapes=[pltpu.VMEM((nh, 2, 1, tile), F32),
                        pltpu.VMEM((nh, 2, VT_ROWS, tile), F32),
                        pltpu.VMEM((nh, 2 * tile, tile), F32),
                        pltpu.VMEM((nh, 2, 1, tile), F32),
                        pltpu.VMEM((nh, 2, tile, tile), F32)],
        compiler_params=_params("arbitrary", "arbitrary"),
        name="diff_attention",
    )(qb, kb, vt, bias_tables, lam, swb)


def _t5_bucket(n):
    max_exact = REL_BUCKETS // 2
    nf = jnp.maximum(n, 1).astype(F32)
    large = max_exact + (jnp.log(nf / max_exact) / math.log(REL_MAX_DIST / max_exact)
                         * (REL_BUCKETS - max_exact)).astype(jnp.int32)
    large = jnp.minimum(large, REL_BUCKETS - 1)
    return jnp.where(n < max_exact, n, large)


def attention_bias_tables(rel_bias):
    tile = BIAS_BLOCK
    n = jnp.arange(2 * tile)
    far = rel_bias[REL_BUCKETS - 1]
    db = ((rel_bias[_t5_bucket(n)] - far[None, :]) * LOG2E).T.astype(F32)
    nh = db.shape[0]
    full = jnp.concatenate([jnp.full((nh, tile - 1), MASK_VALUE, F32), db], axis=1)

    def toeplitz(vec):
        w = jnp.concatenate([vec, jnp.zeros((nh, 1), F32)], axis=1)
        flat = jnp.tile(w, (1, tile))[:, :tile * (2 * tile - 1)]
        return flat.reshape(nh, tile, 2 * tile - 1)[:, :, tile - 1:]

    diag = toeplitz(full[:, :2 * tile - 1])
    near = toeplitz(full[:, tile:3 * tile - 1])
    return jnp.stack([diag, near], axis=1)


def _route(logits):
    mx = jnp.max(logits, axis=0, keepdims=True)
    ex = jnp.exp(logits - mx)
    probs = ex / jnp.sum(ex, axis=0, keepdims=True)
    row = [probs[e:e + 1, :] for e in range(N_EXPERTS)]
    epg = EXPERTS_PER_GROUP
    best_sum = None
    best_group = None
    for g in range(N_GROUPS):
        vals = row[g * epg:(g + 1) * epg]
        top2 = None
        for a in range(epg):
            for b in range(a + 1, epg):
                pair = vals[a] + vals[b]
                top2 = pair if top2 is None else jnp.maximum(top2, pair)
        if g == 0:
            best_sum = top2
            best_group = jnp.zeros_like(top2, dtype=jnp.int32)
        else:
            upd = top2 > best_sum
            best_sum = jnp.where(upd, top2, best_sum)
            best_group = jnp.where(upd, g, best_group)
    gp = []
    for j in range(epg):
        sel = row[j]
        for g in range(1, N_GROUPS):
            sel = jnp.where(best_group == g, row[g * epg + j], sel)
        gp.append(sel)
    v1 = gp[0]
    i1 = jnp.zeros_like(best_group)
    for j in range(1, epg):
        upd = gp[j] > v1
        v1 = jnp.where(upd, gp[j], v1)
        i1 = jnp.where(upd, j, i1)
    v2 = jnp.full_like(v1, -1.0)
    i2 = jnp.zeros_like(best_group)
    for j in range(epg):
        upd = (i1 != j) & (gp[j] > v2)
        v2 = jnp.where(upd, gp[j], v2)
        i2 = jnp.where(upd, j, i2)
    denom = v1 + v2
    e1 = best_group * epg + i1
    e2 = best_group * epg + i2
    eidx = lax.broadcasted_iota(jnp.int32, logits.shape, 0)
    gates = jnp.where(eidx == e1, v1 / denom, 0.0) + jnp.where(eidx == e2, v2 / denom, 0.0)
    return gates, best_group


MOE_BLOCK = 896
MOE_TAIL_BLOCK = 1024
MOE_CHUNK = 256
MOE_EXPERTS_PER_STEP = 2
MOE_SLAB = LANES
MOE_GATE_ROWS = 16
MOE_VMEM_LIMIT = 56 * 1024 * 1024


def _moe_kernel(xt_ref, rwt_ref, rb_ref, tri_ref, wgu_ref, wd_ref, lw_ref, lb_ref, o_hbm, ob_hbm,
                o_ref, ob_ref,
                xg_ref, gates_ref, grpc_ref, keyc_ref, grpr_ref, keyr_ref, cnt_ref,
                xs_ref, gw_ref, ych_ref, yacc_ref):
    del o_hbm, ob_hbm
    e = pl.program_id(1) * MOE_EXPERTS_PER_STEP
    g = e // EXPERTS_PER_GROUP
    first_j = e % EXPERTS_PER_GROUP
    d, bt = xt_ref.shape
    n = MOE_CHUNK
    gr = MOE_GATE_ROWS
    max_chunks = xs_ref.shape[0]

    @pl.when(e == 0)
    def _route_block():
        xt = xt_ref[...]
        xg_ref[:d, :] = xt.astype(BF16)
        logits = jnp.dot(rwt_ref[...], xt, preferred_element_type=F32, precision=HIGHEST) + rb_ref[...]
        gates, best_group = _route(logits)
        gidx = lax.broadcasted_iota(jnp.int32, (SUBLANES, bt), 0)
        onehot = jnp.where(gidx == best_group, 1.0, 0.0)
        rank = jnp.dot(onehot.astype(BF16), tri_ref[...], preferred_element_type=F32)
        key = jnp.sum(onehot * rank, axis=0, keepdims=True)
        grp_f = best_group.astype(F32)
        grpr_ref[...] = grp_f
        keyr_ref[...] = key
        stack = jnp.concatenate([grp_f, key, jnp.zeros((LANES - 2, bt), F32)], axis=0)
        col = stack.T
        grpc_ref[...] = col[:, 0:1]
        keyc_ref[...] = col[:, 1:2]
        for gg in range(N_GROUPS):
            cnt_ref[gg] = jnp.sum(onehot[gg:gg + 1, :]).astype(jnp.int32)
            gates_ref[gg] = jnp.concatenate(
                [gates[gg * EXPERTS_PER_GROUP:(gg + 1) * EXPERTS_PER_GROUP],
                 jnp.zeros((gr - EXPERTS_PER_GROUP, bt), F32)], axis=0)
        yacc_ref[...] = jnp.zeros_like(yacc_ref)

    count = cnt_ref[g]
    g_f = g.astype(F32)

    @pl.when(first_j == 0)
    def _gather_group():
        gt = gates_ref[g]
        hi = gt.astype(BF16)
        xg_ref[d:d + gr, :] = hi
        xg_ref[d + gr:d + 2 * gr, :] = (gt - hi.astype(F32)).astype(BF16)
        key_mine = jnp.where(grpc_ref[...] == g_f, keyc_ref[...], -1.0)
        lane = lax.broadcasted_iota(jnp.int32, (bt, n), 1).astype(F32)
        for c in range(max_chunks):
            @pl.when(c * n < count)
            def _():
                selt = jnp.where(key_mine - float(c * n) == lane, 1.0, 0.0).astype(BF16)
                gx = jnp.dot(xg_ref[...], selt, preferred_element_type=F32)
                xs_ref[c] = gx[:d].astype(BF16)
                gw = gx[d:d + gr] + gx[d + gr:d + 2 * gr]
                for jj in range(EXPERTS_PER_GROUP):
                    gw_ref[c, jj] = jnp.broadcast_to(gw[jj:jj + 1, :], (SUBLANES, n))
                ych_ref[c] = jnp.zeros((d, n), F32)

    for c in range(max_chunks):
        @pl.when(c * n < count)
        def _():
            xs = xs_ref[c]
            half = D_EXPERT // 2
            y = ych_ref[c]
            for ee in range(MOE_EXPERTS_PER_STEP):
                gw = gw_ref[c, first_j + ee][0:1, :]
                gu = [(jnp.dot(wgu_ref[ee, s * half:(s + 1) * half, :], xs, preferred_element_type=F32),
                       jnp.dot(wgu_ref[ee, D_EXPERT + s * half:D_EXPERT + (s + 1) * half, :], xs,
                               preferred_element_type=F32)) for s in range(2)]
                for s in range(2):
                    hid = (_silu(gu[s][0]) * gu[s][1] * gw).astype(BF16)
                    y = y + jnp.dot(wd_ref[ee, :, s * half:(s + 1) * half], hid, preferred_element_type=F32)
            ych_ref[c] = y

    @pl.when(first_j + MOE_EXPERTS_PER_STEP == EXPERTS_PER_GROUP)
    def _scatter_group():
        key_mine = jnp.where(grpr_ref[...] == g_f, keyr_ref[...], -1.0)
        sub = lax.broadcasted_iota(jnp.int32, (n, bt), 0).astype(F32)
        for c in range(max_chunks):
            @pl.when(c * n < count)
            def _():
                sel = jnp.where(key_mine - float(c * n) == sub, 1.0, 0.0).astype(BF16)
                yacc_ref[...] += jnp.dot(ych_ref[c].astype(BF16), sel, preferred_element_type=F32)

    @pl.when(e == N_EXPERTS - MOE_EXPERTS_PER_STEP)
    def _finish():
        for c in range(bt // MOE_SLAB):
            cols = slice(c * MOE_SLAB, (c + 1) * MOE_SLAB)
            z = DN_ALPHA * xt_ref[:, cols] + yacc_ref[:, cols]
            y = _layer_norm(z.T, lw_ref[...], lb_ref[...])
            o_ref[cols, :] = y
            ob_ref[cols, :] = y.astype(BF16)


def _transpose_cast_kernel(*refs):
    *in_refs, o_ref = refs
    rows = 0
    for w_ref in in_refs:
        wt = w_ref[0, 0].astype(F32).T
        o_ref[0, rows:rows + wt.shape[0], :] = wt.astype(o_ref.dtype)
        rows += wt.shape[0]


def transpose_cast(layer, *ws):
    _, ne, r, c = ws[0].shape
    return pl.pallas_call(
        _transpose_cast_kernel,
        grid=(ne,),
        in_specs=[pl.BlockSpec((1, 1, r, c), lambda e: (layer, e, 0, 0)) for _ in ws],
        out_specs=pl.BlockSpec((1, len(ws) * c, r), lambda e: (e, 0, 0)),
        out_shape=jax.ShapeDtypeStruct((ne, len(ws) * c, r), BF16),
        compiler_params=_params("parallel"),
        name="transpose_cast",
    )(*ws)


def _moe_blocks(xt, rwt, rb, wgu_t, wd_t, lw, lb, out_f32, out_bf16, *, bt, first_block, num_blocks):
    d, t = xt.shape
    n = MOE_CHUNK
    max_chunks = pl.cdiv(bt, n)
    assert bt % MOE_SLAB == 0 and (first_block + num_blocks) * bt <= t
    tri = jnp.asarray(np.triu(np.ones((bt, bt), np.float32), k=1), dtype=BF16)
    return pl.pallas_call(
        _moe_kernel,
        grid=(num_blocks, N_EXPERTS // MOE_EXPERTS_PER_STEP),
        in_specs=[pl.BlockSpec((d, bt), lambda i, e: (0, first_block + i)),
                  pl.BlockSpec((N_EXPERTS, d), lambda i, e: (0, 0)),
                  pl.BlockSpec((N_EXPERTS, 1), lambda i, e: (0, 0)),
                  pl.BlockSpec((bt, bt), lambda i, e: (0, 0)),
                  pl.BlockSpec((MOE_EXPERTS_PER_STEP, 2 * D_EXPERT, d), lambda i, e: (e, 0, 0)),
                  pl.BlockSpec((MOE_EXPERTS_PER_STEP, d, D_EXPERT), lambda i, e: (e, 0, 0)),
                  pl.BlockSpec((1, d), lambda i, e: (0, 0)),
                  pl.BlockSpec((1, d), lambda i, e: (0, 0)),
                  pl.BlockSpec(memory_space=pl.ANY),
                  pl.BlockSpec(memory_space=pl.ANY)],
        out_specs=[pl.BlockSpec((bt, d), lambda i, e: (first_block + i, 0)),
                   pl.BlockSpec((bt, d), lambda i, e: (first_block + i, 0))],
        out_shape=[jax.ShapeDtypeStruct((t, d), F32),
                   jax.ShapeDtypeStruct((t, d), BF16)],
        input_output_aliases={8: 0, 9: 1},
        scratch_shapes=[pltpu.VMEM((d + 2 * MOE_GATE_ROWS, bt), BF16),
                        pltpu.VMEM((N_GROUPS, MOE_GATE_ROWS, bt), F32),
                        pltpu.VMEM((bt, 1), F32), pltpu.VMEM((bt, 1), F32),
                        pltpu.VMEM((1, bt), F32), pltpu.VMEM((1, bt), F32),
                        pltpu.SMEM((N_GROUPS,), jnp.int32),
                        pltpu.VMEM((max_chunks, d, n), BF16),
                        pltpu.VMEM((max_chunks, EXPERTS_PER_GROUP, SUBLANES, n), F32),
                        pltpu.VMEM((max_chunks, d, n), F32),
                        pltpu.VMEM((d, bt), F32)],
        compiler_params=pltpu.CompilerParams(dimension_semantics=("arbitrary", "arbitrary"),
                                             vmem_limit_bytes=MOE_VMEM_LIMIT),
        name="moe_residual_ln",
    )(xt, rwt, rb, tri, wgu_t, wd_t, lw.reshape(1, d), lb.reshape(1, d), out_f32, out_bf16)


def moe_residual_ln(xt, rwt, rb, wgu_t, wd_t, lw, lb, buf_f32, buf_bf16):
    d, t = xt.shape
    head_blocks = max((k for k in range(t // MOE_BLOCK + 1)
                       if (t - k * MOE_BLOCK) % MOE_TAIL_BLOCK == 0), default=0)
    tail_start = head_blocks * MOE_BLOCK
    assert (t - tail_start) % MOE_TAIL_BLOCK == 0
    out = (buf_f32, buf_bf16)
    if head_blocks:
        out = _moe_blocks(xt, rwt, rb, wgu_t, wd_t, lw, lb, *out,
                          bt=MOE_BLOCK, first_block=0, num_blocks=head_blocks)
    if tail_start < t:
        out = _moe_blocks(xt, rwt, rb, wgu_t, wd_t, lw, lb, *out, bt=MOE_TAIL_BLOCK,
                          first_block=tail_start // MOE_TAIL_BLOCK,
                          num_blocks=(t - tail_start) // MOE_TAIL_BLOCK)
    return out


def _lambda_init_for(layer):
    return 0.8 - 0.6 * math.exp(-0.3 * layer)


def kernel(x, a_w_in, a_conv, a_A_log, a_dt_bias, a_norm_w, a_w_out, b_w_q, b_w_kv, b_lambda, b_subln_w, b_w_out, rel_bias, router_w, router_b, moe_w_gate, moe_w_up, moe_w_down, ln_w, ln_b):
    bsz, t, d = x.shape
    assert bsz == 1 and d == D_MODEL and t % A_CHUNK == 0
    xf = x.reshape(t, d).astype(F32)
    xb = xf.astype(BF16)

    gate_tile = min(512, t)
    tri = jnp.asarray(np.kron(np.eye(gate_tile // A_CHUNK, dtype=np.float32),
                              np.tril(np.ones((A_CHUNK, A_CHUNK), np.float32))))
    rwt = router_w.T.astype(F32)
    rb = router_b.reshape(N_EXPERTS, 1).astype(F32)
    attn_tile = min(512, t)
    assert attn_tile % BIAS_BLOCK == 0
    bias_tables = attention_bias_tables(rel_bias.astype(F32))

    kb = None
    vt = None
    for l in range(DEPTH):
        if l < N_A_LAYERS:
            w_in = a_w_in[l]
            w_main = w_in[:, :A_CONV_CH + A_V].astype(BF16)
            w_ab = jnp.pad(w_in[:, A_CONV_CH + A_V:], ((0, 0), (0, LANES - 2 * A_HEADS))).astype(BF16)
            conv_pad = jnp.pad(a_conv[l].astype(F32), ((0, 0), (0, A_V)))
            par = jnp.zeros((SUBLANES, LANES), F32)
            par = par.at[0, :A_HEADS].set(a_A_log[l].astype(F32))
            par = par.at[1, :A_HEADS].set(a_dt_bias[l].astype(F32))
            proj = gdn_inproj(xb, w_main, conv_pad)
            fac, fact = gdn_gates(xb, w_ab, par, tri)
            fact3 = fact.reshape(2 * A_HEADS, t // A_CHUNK, A_CHUNK).transpose(1, 0, 2)
            u, wq, qk, kd, dec = gdn_intra(proj, fac, fact3)
            mixed = gdn_scan(u, wq, qk, kd, dec, proj, a_norm_w[l].astype(F32))
            w_out = a_w_out[l].astype(BF16)
        else:
            j = l - N_A_LAYERS
            if kb is None:
                kb = matmul_bf16(xb, b_w_kv[:, :B_QK].astype(BF16))
                vt = v_transposed_proj(xb, b_w_kv[:, B_QK:].T.astype(BF16), attn_tile)
            qb = matmul_bf16(xb, b_w_q[j].astype(BF16), scale=B_DH ** -0.5 * LOG2E)
            mixed = diff_attention(qb, kb, vt, bias_tables, b_lambda[j].astype(F32),
                                   b_subln_w[j].astype(F32), _lambda_init_for(l), tile=attn_tile)
            w_out = b_w_out[j].astype(BF16)
        xt = matmul_residual_ln_t(mixed, w_out, xf, ln_w[l, 0].astype(F32), ln_b[l, 0].astype(F32))
        wgu_t = transpose_cast(l, moe_w_gate, moe_w_up)
        wd_t = transpose_cast(l, moe_w_down)
        xf, xb = moe_residual_ln(xt, rwt, rb, wgu_t, wd_t,
                                 ln_w[l, 1].astype(F32), ln_b[l, 1].astype(F32), xf, xb)
    return xf.reshape(bsz, t, d).astype(x.dtype)
```

```python
import functools
import math

import jax
import jax.numpy as jnp
import numpy as np
from jax import lax
from jax.experimental import pallas as pl
from jax.experimental.pallas import tpu as pltpu

F32 = jnp.float32
BF16 = jnp.bfloat16

D_MODEL = 1024
DEPTH = 4
N_A_LAYERS = DEPTH // 2
A_HEADS = 8
A_DK = 128
A_DV = 128
A_QK = A_HEADS * A_DK
A_V = A_HEADS * A_DV
A_CONV = 4
A_CONV_CH = 2 * A_QK + A_V
A_CHUNK = 64
B_HEADS = 8
B_DH = 64
B_QK = B_HEADS * 2 * B_DH
REL_BUCKETS = 32
REL_MAX_DIST = 128
N_EXPERTS = 16
N_GROUPS = 4
EXPERTS_PER_GROUP = N_EXPERTS // N_GROUPS
D_EXPERT = 512
DN_ALPHA = (2 * DEPTH) ** 0.25
LN_EPS = 1e-5
RMS_EPS = 1e-6

LANES = 128
SUBLANES = 8
VMEM_LIMIT = 48 * 1024 * 1024
MASK_VALUE = -1e30

HIGHEST = lax.Precision.HIGHEST
NT_DIMS = (((1,), (1,)), ((), ()))
TN_DIMS = (((0,), (0,)), ((), ()))


def _params(*sem):
    return pltpu.CompilerParams(dimension_semantics=sem, vmem_limit_bytes=VMEM_LIMIT)


def _sigmoid(x):
    return 1.0 / (1.0 + jnp.exp(-x))


def _silu(x):
    return x * _sigmoid(x)


def _layer_norm(v, w, b):
    mu = jnp.mean(v, axis=-1, keepdims=True)
    c = v - mu
    var = jnp.mean(c * c, axis=-1, keepdims=True)
    return c * lax.rsqrt(var + LN_EPS) * w + b


def _mm_kernel(a_ref, w_ref, o_ref, *, scale):
    y = jnp.dot(a_ref[...], w_ref[...], preferred_element_type=F32)
    if scale != 1.0:
        y = y * scale
    o_ref[...] = y.astype(o_ref.dtype)


def matmul_bf16(a, w, *, scale=1.0, tm=1024, tn=1024):
    m, k = a.shape
    n = w.shape[1]
    tm = min(tm, m)
    return pl.pallas_call(
        functools.partial(_mm_kernel, scale=scale),
        grid=(n // tn, m // tm),
        in_specs=[pl.BlockSpec((tm, k), lambda j, i: (i, 0)),
                  pl.BlockSpec((k, tn), lambda j, i: (0, j))],
        out_specs=pl.BlockSpec((tm, tn), lambda j, i: (i, j)),
        out_shape=jax.ShapeDtypeStruct((m, n), BF16),
        compiler_params=_params("parallel", "arbitrary"),
        name="matmul_bf16",
    )(a, w)


def _mm_ln_kernel(a_ref, w_ref, x_ref, lw_ref, lb_ref, ot_ref):
    tm = a_ref.shape[0]
    parts = 2
    pr = tm // parts
    hs = [jnp.dot(a_ref[p * pr:(p + 1) * pr, :], w_ref[...], preferred_element_type=F32) for p in range(parts)]
    for p, h in enumerate(hs):
        y = _layer_norm(DN_ALPHA * x_ref[p * pr:(p + 1) * pr, :] + h, lw_ref[...], lb_ref[...])
        ot_ref[:, p * pr:(p + 1) * pr] = y.T


def matmul_residual_ln_t(a, w, x, lw, lb, *, tm=1024):
    m, k = a.shape
    n = w.shape[1]
    tm = min(tm, m)
    return pl.pallas_call(
        _mm_ln_kernel,
        grid=(m // tm,),
        in_specs=[pl.BlockSpec((tm, k), lambda i: (i, 0)),
                  pl.BlockSpec((k, n), lambda i: (0, 0)),
                  pl.BlockSpec((tm, n), lambda i: (i, 0)),
                  pl.BlockSpec((1, n), lambda i: (0, 0)),
                  pl.BlockSpec((1, n), lambda i: (0, 0))],
        out_specs=pl.BlockSpec((n, tm), lambda i: (0, i)),
        out_shape=jax.ShapeDtypeStruct((n, m), F32),
        compiler_params=_params("parallel"),
        name="matmul_residual_ln",
    )(a, w, x, lw.reshape(1, n), lb.reshape(1, n))


def _gdn_inproj_kernel(x_ref, w_ref, cw_ref, o_ref, halo_ref):
    n = pl.program_id(0)
    m = pl.program_id(1)
    tm, tn = o_ref.shape
    parts = 2
    pr = tm // parts

    @pl.when(m == 0)
    def _():
        halo_ref[...] = jnp.zeros_like(halo_ref)

    def project():
        return [jnp.dot(x_ref[p * pr:(p + 1) * pr, :], w_ref[...], preferred_element_type=F32)
                for p in range(parts)]

    def conv_silu(ys):
        cw = cw_ref[...]
        rows = lax.broadcasted_iota(jnp.int32, (SUBLANES, tn), 0)
        out = []
        halo = halo_ref[...]
        for y in ys:
            acc = y * cw[A_CONV - 1:A_CONV]
            for shift in range(1, A_CONV):
                sh = pltpu.roll(y, shift, 0)
                top = jnp.where(rows < shift, pltpu.roll(halo, shift, 0), sh[:SUBLANES])
                sh = jnp.concatenate([top, sh[SUBLANES:]], axis=0)
                acc = acc + sh * cw[A_CONV - 1 - shift:A_CONV - shift]
            halo = y[pr - SUBLANES:]
            out.append(_silu(acc))
        halo_ref[...] = halo
        return out

    @pl.when(n == 3)
    def _():
        for p, y in enumerate(project()):
            o_ref[p * pr:(p + 1) * pr, :] = y.astype(o_ref.dtype)

    @pl.when(n == 2)
    def _():
        for p, a in enumerate(conv_silu(project())):
            o_ref[p * pr:(p + 1) * pr, :] = a.astype(o_ref.dtype)

    @pl.when(n < 2)
    def _():
        post = jnp.where(n == 0, A_DK ** -0.5, 1.0).astype(F32)
        for p, a in enumerate(conv_silu(project())):
            for h in range(tn // A_DK):
                seg = a[:, h * A_DK:(h + 1) * A_DK]
                ss = jnp.sum(seg * seg, axis=-1, keepdims=True)
                o_ref[p * pr:(p + 1) * pr, h * A_DK:(h + 1) * A_DK] = (
                    seg * (lax.rsqrt(ss + RMS_EPS) * post)).astype(o_ref.dtype)


def gdn_inproj(xb, w_main, conv_w_pad, *, tm=1024):
    t, k = xb.shape
    n = w_main.shape[1]
    tn = A_QK
    tm = min(tm, t)
    return pl.pallas_call(
        _gdn_inproj_kernel,
        grid=(n // tn, t // tm),
        in_specs=[pl.BlockSpec((tm, k), lambda j, i: (i, 0)),
                  pl.BlockSpec((k, tn), lambda j, i: (0, j)),
                  pl.BlockSpec((A_CONV, tn), lambda j, i: (0, j))],
        out_specs=pl.BlockSpec((tm, tn), lambda j, i: (i, j)),
        out_shape=jax.ShapeDtypeStruct((t, n), BF16),
        scratch_shapes=[pltpu.VMEM((SUBLANES, tn), F32)],
        compiler_params=_params("arbitrary", "arbitrary"),
        name="gdn_inproj",
    )(xb, w_main, conv_w_pad)


def _gdn_gates_kernel(x_ref, w_ref, par_ref, tri_ref, fac_ref, fact_ref):
    ab = jnp.dot(x_ref[...], w_ref[...], preferred_element_type=F32)
    lane = lax.broadcasted_iota(jnp.int32, ab.shape, 1)
    neg_a = -jnp.exp(par_ref[0:1, :])
    z = ab + par_ref[1:2, :]
    softplus = jnp.maximum(z, 0.0) + jnp.log1p(jnp.exp(-jnp.abs(z)))
    g = jnp.where(lane < A_HEADS, neg_a * softplus, 0.0)
    beta = _sigmoid(ab)
    gc = jnp.dot(tri_ref[...], g, preferred_element_type=F32, precision=HIGHEST)
    fac = jnp.where(lane < A_HEADS, gc, jnp.where(lane < 2 * A_HEADS, beta, 0.0))
    fac_ref[...] = fac
    fact_ref[...] = fac.T[:2 * A_HEADS, :]


def gdn_gates(xb, w_ab_pad, par, tri, *, tm=512):
    t, k = xb.shape
    tm = min(tm, t)
    return pl.pallas_call(
        _gdn_gates_kernel,
        grid=(t // tm,),
        in_specs=[pl.BlockSpec((tm, k), lambda i: (i, 0)),
                  pl.BlockSpec((k, LANES), lambda i: (0, 0)),
                  pl.BlockSpec((SUBLANES, LANES), lambda i: (0, 0)),
                  pl.BlockSpec((tm, tm), lambda i: (0, 0))],
        out_specs=[pl.BlockSpec((tm, LANES), lambda i: (i, 0)),
                   pl.BlockSpec((2 * A_HEADS, tm), lambda i: (0, i))],
        out_shape=[jax.ShapeDtypeStruct((t, LANES), F32),
                   jax.ShapeDtypeStruct((2 * A_HEADS, t), F32)],
        compiler_params=_params("parallel"),
        name="gdn_gates",
    )(xb, w_ab_pad, par, tri)


GDN_INTRA_CHUNKS = 4
GDN_SCAN_CHUNKS = 16


def _mm_bf16(a, b):
    return jnp.dot(a.astype(BF16), b.astype(BF16), preferred_element_type=F32)


def _gdn_intra_kernel(q_ref, k_ref, v_ref, fac_ref, fact_ref, u_ref, wq_ref, qk_ref, kd_ref, dec_ref):
    c = A_CHUNK
    ii = lax.broadcasted_iota(jnp.int32, (c, c), 0)
    jj = lax.broadcasted_iota(jnp.int32, (c, c), 1)
    incl = ii >= jj
    strict = ii > jj
    units = [(cb, h) for cb in range(GDN_INTRA_CHUNKS) for h in range(A_HEADS)]
    rows = {cb: slice(cb * c, (cb + 1) * c) for cb in range(GDN_INTRA_CHUNKS)}
    cols = {h: slice(h * A_DK, (h + 1) * A_DK) for h in range(A_HEADS)}

    q, k, kf, beta, eg, decay = {}, {}, {}, {}, {}, {}
    for cb, h in units:
        fac = fac_ref[rows[cb], :]
        gc_col = fac[:, h:h + 1]
        gc_row = fact_ref[cb][h:h + 1, :]
        g_last = gc_col[c - 1:c, :]
        beta[cb, h] = fac[:, A_HEADS + h:A_HEADS + h + 1]
        eg[cb, h] = jnp.exp(gc_col)
        decay[cb, h] = jnp.exp(jnp.where(incl, gc_col - gc_row, MASK_VALUE))
        q[cb, h] = q_ref[rows[cb], cols[h]]
        k[cb, h] = k_ref[rows[cb], cols[h]]
        kf[cb, h] = k[cb, h].astype(F32)
        kd_ref[rows[cb], cols[h]] = (kf[cb, h] * jnp.exp(g_last - gc_col)).astype(kd_ref.dtype)
        dec_ref[cb, h:h + 1, :] = jnp.broadcast_to(jnp.exp(g_last), (1, A_DV))
        wq_ref[cb, c:, cols[h]] = (q[cb, h].astype(F32) * eg[cb, h]).astype(wq_ref.dtype)

    mmat, x, p = {}, {}, {}
    for u_ in units:
        kk = lax.dot_general(k[u_], k[u_], NT_DIMS, preferred_element_type=F32)
        mmat[u_] = jnp.where(strict, beta[u_] * kk * decay[u_], 0.0)
    for u_ in units:
        cb, h = u_
        qk = lax.dot_general(q[u_], k[u_], NT_DIMS, preferred_element_type=F32) * decay[u_]
        qk_ref[cb, h] = qk.astype(qk_ref.dtype)
    for u_ in units:
        x[u_] = -mmat[u_]
        p[u_] = _mm_bf16(mmat[u_], mmat[u_])
    for level in range(5):
        for u_ in units:
            x[u_] = x[u_] + p[u_] + _mm_bf16(x[u_], p[u_])
        if level < 4:
            for u_ in units:
                p[u_] = _mm_bf16(p[u_], p[u_])
    for u_ in units:
        cb, h = u_
        v = v_ref[rows[cb], cols[h]].astype(F32)
        rhs = jnp.concatenate([beta[u_] * v, (beta[u_] * eg[u_]) * kf[u_]], axis=1)
        uw = rhs + _mm_bf16(x[u_], rhs)
        u_ref[rows[cb], cols[h]] = uw[:, :A_DV].astype(u_ref.dtype)
        wq_ref[cb, :c, cols[h]] = uw[:, A_DV:].astype(wq_ref.dtype)


def gdn_intra(proj, fac, fact3):
    t = proj.shape[0]
    c = A_CHUNK
    cb = GDN_INTRA_CHUNKS
    nc = t // c
    return pl.pallas_call(
        _gdn_intra_kernel,
        grid=(nc // cb,),
        in_specs=[pl.BlockSpec((cb * c, A_QK), lambda i: (i, 0)),
                  pl.BlockSpec((cb * c, A_QK), lambda i: (i, 1)),
                  pl.BlockSpec((cb * c, A_V), lambda i: (i, 2)),
                  pl.BlockSpec((cb * c, LANES), lambda i: (i, 0)),
                  pl.BlockSpec((cb, 2 * A_HEADS, c), lambda i: (i, 0, 0))],
        out_specs=[pl.BlockSpec((cb * c, A_V), lambda i: (i, 0)),
                   pl.BlockSpec((cb, 2 * c, A_QK), lambda i: (i, 0, 0)),
                   pl.BlockSpec((cb, A_HEADS, c, c), lambda i: (i, 0, 0, 0)),
                   pl.BlockSpec((cb * c, A_QK), lambda i: (i, 0)),
                   pl.BlockSpec((cb, A_HEADS, A_DV), lambda i: (i, 0, 0))],
        out_shape=[jax.ShapeDtypeStruct((t, A_V), BF16),
                   jax.ShapeDtypeStruct((nc, 2 * c, A_QK), BF16),
                   jax.ShapeDtypeStruct((nc, A_HEADS, c, c), BF16),
                   jax.ShapeDtypeStruct((t, A_QK), BF16),
                   jax.ShapeDtypeStruct((nc, A_HEADS, A_DV), F32)],
        compiler_params=_params("parallel"),
        name="gdn_intra",
    )(proj, proj, proj, fac, fact3)


def _gdn_scan_kernel(u_ref, wq_ref, qk_ref, kd_ref, dec_ref, z_ref, nw_ref, o_ref, s_ref):
    c = A_CHUNK

    @pl.when(pl.program_id(0) == 0)
    def _():
        s_ref[...] = jnp.zeros_like(s_ref)

    nw = nw_ref[...]
    heads = range(A_HEADS)
    cols = [slice(h * A_DK, (h + 1) * A_DK) for h in heads]
    for cb in range(GDN_SCAN_CHUNKS):
        rows = slice(cb * c, (cb + 1) * c)
        s = [s_ref[h] for h in heads]
        sb = [s[h].astype(BF16) for h in heads]
        ws_qs = [jnp.dot(wq_ref[cb, :, cols[h]], sb[h], preferred_element_type=F32) for h in heads]
        v_new = [(u_ref[rows, cols[h]].astype(F32) - ws_qs[h][:c]).astype(BF16) for h in heads]
        o = [ws_qs[h][c:] + jnp.dot(qk_ref[cb, h], v_new[h], preferred_element_type=F32) for h in heads]
        for h in heads:
            s_ref[h] = s[h] * dec_ref[cb, h:h + 1, :] + lax.dot_general(
                kd_ref[rows, cols[h]], v_new[h], TN_DIMS, preferred_element_type=F32)
        for h in heads:
            z = z_ref[rows, cols[h]].astype(F32)
            ms = jnp.mean(o[h] * o[h], axis=-1, keepdims=True)
            o_ref[rows, cols[h]] = (o[h] * lax.rsqrt(ms + RMS_EPS) * nw * _silu(z)).astype(o_ref.dtype)


def gdn_scan(u, wq, qk, kd, dec, proj, norm_w):
    t = u.shape[0]
    c = A_CHUNK
    cb = GDN_SCAN_CHUNKS
    return pl.pallas_call(
        _gdn_scan_kernel,
        grid=(t // (cb * c),),
        in_specs=[pl.BlockSpec((cb * c, A_V), lambda i: (i, 0)),
                  pl.BlockSpec((cb, 2 * c, A_QK), lambda i: (i, 0, 0)),
                  pl.BlockSpec((cb, A_HEADS, c, c), lambda i: (i, 0, 0, 0)),
                  pl.BlockSpec((cb * c, A_QK), lambda i: (i, 0)),
                  pl.BlockSpec((cb, A_HEADS, A_DV), lambda i: (i, 0, 0)),
                  pl.BlockSpec((cb * c, A_V), lambda i: (i, 3)),
                  pl.BlockSpec((1, A_DV), lambda i: (0, 0))],
        out_specs=pl.BlockSpec((cb * c, A_V), lambda i: (i, 0)),
        out_shape=jax.ShapeDtypeStruct((t, A_V), BF16),
        scratch_shapes=[pltpu.VMEM((A_HEADS, A_DK, A_DV), F32)],
        compiler_params=_params("arbitrary"),
        name="gdn_scan",
    )(u, wq, qk, kd, dec, proj, norm_w.reshape(1, A_DV))


LOG2E = math.log2(math.e)
VT_ROWS = 2 * B_DH + 16


def _vt_proj_kernel(x_ref, w_ref, o_ref):
    tk = x_ref.shape[0]
    y = lax.dot_general(w_ref[...], x_ref[...], NT_DIMS, preferred_element_type=F32)
    ones = jnp.ones((VT_ROWS - 2 * B_DH, tk), o_ref.dtype)
    for h in range(B_HEADS):
        o_ref[0, h, :2 * B_DH, :] = y[h * 2 * B_DH:(h + 1) * 2 * B_DH].astype(o_ref.dtype)
        o_ref[0, h, 2 * B_DH:, :] = ones


def v_transposed_proj(xb, w_vt, tile):
    t, k = xb.shape
    n = w_vt.shape[0]
    return pl.pallas_call(
        _vt_proj_kernel,
        grid=(t // tile,),
        in_specs=[pl.BlockSpec((tile, k), lambda i: (i, 0)),
                  pl.BlockSpec((n, k), lambda i: (0, 0))],
        out_specs=pl.BlockSpec((1, B_HEADS, VT_ROWS, tile), lambda i: (i, 0, 0, 0)),
        out_shape=jax.ShapeDtypeStruct((t // tile, B_HEADS, VT_ROWS, tile), BF16),
        compiler_params=_params("parallel"),
        name="v_transposed_proj",
    )(xb, w_vt)


ATTN_HEADS_PER_STEP = 2
BIAS_BLOCK = REL_MAX_DIST


def _attn_kernel(q_ref, k_ref, vt_ref, tab_ref, lam_ref, swb_ref, o_ref, m_ref, acc_ref,
                 st_ref, mt_ref, bias_ref, *, lambda_init, tile):
    qi = pl.program_id(1)
    dv = 2 * B_DH
    nh = ATTN_HEADS_PER_STEP
    first_map = lax.broadcasted_iota(jnp.int32, (tile, dv), 1) < B_DH

    @pl.when(qi == 0)
    def _():
        nblk = tile // BIAS_BLOCK
        zeros = jnp.zeros((BIAS_BLOCK, BIAS_BLOCK), F32)
        masked = jnp.full((BIAS_BLOCK, BIAS_BLOCK), MASK_VALUE, F32)
        for h in range(nh):
            for ja in range(nblk):
                for ib in range(nblk):
                    rows = slice(ja * BIAS_BLOCK, (ja + 1) * BIAS_BLOCK)
                    cols = slice(ib * BIAS_BLOCK, (ib + 1) * BIAS_BLOCK)
                    if ib == ja:
                        blk = tab_ref[h, 0]
                    elif ib == ja + 1:
                        blk = tab_ref[h, 1]
                    else:
                        blk = zeros if ib > ja else masked
                    bias_ref[h, 0, rows, cols] = blk
                    bias_ref[h, 1, rows, cols] = tab_ref[h, 1] if (ja == nblk - 1 and ib == 0) else zeros

    m_ref[...] = jnp.full_like(m_ref, MASK_VALUE)
    acc_ref[...] = jnp.zeros_like(acc_ref)

    def produce(h, j):
        hc = slice(h * dv, (h + 1) * dv)
        q = q_ref[:, hc]
        k = k_ref[pl.ds(pl.multiple_of(j * tile, tile), tile), hc]
        zero = jnp.zeros_like(k)
        k2 = jnp.concatenate([jnp.where(first_map, k, zero), jnp.where(first_map, zero, k)], axis=0)
        st = lax.dot_general(k2, q, NT_DIMS, preferred_element_type=F32)
        st_ref[h] = st
        for i in range(2):
            mt_ref[h, i] = jnp.max(st[i * tile:(i + 1) * tile], axis=0, keepdims=True)

    def consume(h, j, slot):
        vt = vt_ref[j, h]
        for i in range(2):
            s = st_ref[h, i * tile:(i + 1) * tile, :]
            m_prev = m_ref[h, i]
            if slot is None:
                m_new = jnp.maximum(m_prev, mt_ref[h, i])
            else:
                s = s + bias_ref[h, slot]
                m_new = jnp.maximum(m_prev, jnp.max(s, axis=0, keepdims=True))
            alpha = jnp.exp2(m_prev - m_new)
            p = jnp.exp2(s - m_new).astype(BF16)
            acc_ref[h, i] = alpha * acc_ref[h, i] + jnp.dot(vt, p, preferred_element_type=F32)
            m_ref[h, i] = m_new

    n_far = jnp.maximum(qi - 1, 0)
    produce(0, 0)

    def far_units(j):
        produce(1, j)
        consume(0, j, None)
        produce(0, j + 1)
        consume(1, j, None)

    def far_pair_body(jj, carry):
        far_units(2 * jj)
        far_units(2 * jj + 1)
        return carry

    lax.fori_loop(0, n_far // 2, far_pair_body, 0)

    @pl.when(n_far % 2 == 1)
    def _():
        far_units(n_far - 1)

    @pl.when(qi >= 1)
    def _():
        produce(1, qi - 1)
        consume(0, qi - 1, 1)
        produce(0, qi)
        consume(1, qi - 1, 1)
        produce(1, qi)
        consume(0, qi, 0)
        consume(1, qi, 0)

    @pl.when(qi == 0)
    def _():
        produce(1, 0)
        consume(0, 0, 0)
        consume(1, 0, 0)

    lam = lam_ref[...]
    lam_val = (jnp.exp(jnp.sum(lam[0:1] * lam[1:2], axis=-1, keepdims=True))
               - jnp.exp(jnp.sum(lam[2:3] * lam[3:4], axis=-1, keepdims=True)) + lambda_init)
    for h in range(nh):
        a0 = acc_ref[h, 0]
        a1 = acc_ref[h, 1]
        ot = a0[:dv] / a0[dv:dv + 1] - lam_val * (a1[:dv] / a1[dv:dv + 1])
        ms = jnp.mean(ot * ot, axis=0, keepdims=True)
        ot = ot * lax.rsqrt(ms + RMS_EPS) * swb_ref[...] * (1.0 - lambda_init)
        o_ref[:, h * dv:(h + 1) * dv] = ot.T.astype(o_ref.dtype)


def diff_attention(qb, kb, vt, bias_tables, lam, subln_w, lambda_init, *, tile):
    t = qb.shape[0]
    nb = t // tile
    dv = 2 * B_DH
    nh = ATTN_HEADS_PER_STEP
    resident = pl.Buffered(1)
    swb = jnp.broadcast_to(subln_w.reshape(dv, 1), (dv, tile))
    return pl.pallas_call(
        functools.partial(_attn_kernel, lambda_init=lambda_init, tile=tile),
        grid=(B_HEADS // nh, nb),
        in_specs=[pl.BlockSpec((tile, nh * dv), lambda h, i: (i, h)),
                  pl.BlockSpec((t, nh * dv), lambda h, i: (0, h), pipeline_mode=resident),
                  pl.BlockSpec((nb, nh, VT_ROWS, tile), lambda h, i: (0, h, 0, 0), pipeline_mode=resident),
                  pl.BlockSpec((nh, 2, BIAS_BLOCK, BIAS_BLOCK), lambda h, i: (h, 0, 0, 0)),
                  pl.BlockSpec((4, B_DH), lambda h, i: (0, 0)),
                  pl.BlockSpec((dv, tile), lambda h, i: (0, 0))],
        out_specs=pl.BlockSpec((tile, nh * dv), lambda h, i: (i, h)),
        out_shape=jax.ShapeDtypeStruct((t, B_HEADS * dv), BF16),
        scratch_shapes=[pltpu.VMEM((nh, 2, 1, tile), F32),
                        pltpu.VMEM((nh, 2, VT_ROWS, tile), F32),
                        pltpu.VMEM((nh, 2 * tile, tile), F32),
                        pltpu.VMEM((nh, 2, 1, tile), F32),
                        pltpu.VMEM((nh, 2, tile, tile), F32)],
        compiler_params=_params("arbitrary", "arbitrary"),
        name="diff_attention",
    )(qb, kb, vt, bias_tables, lam, swb)


def _t5_bucket(n):
    max_exact = REL_BUCKETS // 2
    nf = jnp.maximum(n, 1).astype(F32)
    large = max_exact + (jnp.log(nf / max_exact) / math.log(REL_MAX_DIST / max_exact)
                         * (REL_BUCKETS - max_exact)).astype(jnp.int32)
    large = jnp.minimum(large, REL_BUCKETS - 1)
    return jnp.where(n < max_exact, n, large)


def attention_bias_tables(rel_bias):
    tile = BIAS_BLOCK
    n = jnp.arange(2 * tile)
    far = rel_bias[REL_BUCKETS - 1]
    db = ((rel_bias[_t5_bucket(n)] - far[None, :]) * LOG2E).T.astype(F32)
    nh = db.shape[0]
    full = jnp.concatenate([jnp.full((nh, tile - 1), MASK_VALUE, F32), db], axis=1)

    def toeplitz(vec):
        w = jnp.concatenate([vec, jnp.zeros((nh, 1), F32)], axis=1)
        flat = jnp.tile(w, (1, tile))[:, :tile * (2 * tile - 1)]
        return flat.reshape(nh, tile, 2 * tile - 1)[:, :, tile - 1:]

    diag = toeplitz(full[:, :2 * tile - 1])
    near = toeplitz(full[:, tile:3 * tile - 1])
    return jnp.stack([diag, near], axis=1)


def _route(logits):
    mx = jnp.max(logits, axis=0, keepdims=True)
    ex = jnp.exp(logits - mx)
    probs = ex / jnp.sum(ex, axis=0, keepdims=True)
    row = [probs[e:e + 1, :] for e in range(N_EXPERTS)]
    epg = EXPERTS_PER_GROUP
    best_sum = None
    best_group = None
    for g in range(N_GROUPS):
        vals = row[g * epg:(g + 1) * epg]
        top2 = None
        for a in range(epg):
            for b in range(a + 1, epg):
                pair = vals[a] + vals[b]
                top2 = pair if top2 is None else jnp.maximum(top2, pair)
        if g == 0:
            best_sum = top2
            best_group = jnp.zeros_like(top2, dtype=jnp.int32)
        else:
            upd = top2 > best_sum
            best_sum = jnp.where(upd, top2, best_sum)
            best_group = jnp.where(upd, g, best_group)
    gp = []
    for j in range(epg):
        sel = row[j]
        for g in range(1, N_GROUPS):
            sel = jnp.where(best_group == g, row[g * epg + j], sel)
        gp.append(sel)
    v1 = gp[0]
    i1 = jnp.zeros_like(best_group)
    for j in range(1, epg):
        upd = gp[j] > v1
        v1 = jnp.where(upd, gp[j], v1)
        i1 = jnp.where(upd, j, i1)
    v2 = jnp.full_like(v1, -1.0)
    i2 = jnp.zeros_like(best_group)
    for j in range(epg):
        upd = (i1 != j) & (gp[j] > v2)
        v2 = jnp.where(upd, gp[j], v2)
        i2 = jnp.where(upd, j, i2)
    denom = v1 + v2
    e1 = best_group * epg + i1
    e2 = best_group * epg + i2
    eidx = lax.broadcasted_iota(jnp.int32, logits.shape, 0)
    gates = jnp.where(eidx == e1, v1 / denom, 0.0) + jnp.where(eidx == e2, v2 / denom, 0.0)
    return gates, best_group


MOE_BLOCK = 896
MOE_TAIL_BLOCK = 1024
MOE_CHUNK = 256
MOE_EXPERTS_PER_STEP = 2
MOE_SLAB = LANES
MOE_GATE_ROWS = 16
MOE_VMEM_LIMIT = 56 * 1024 * 1024


def _moe_kernel(xt_ref, rwt_ref, rb_ref, tri_ref, wgu_ref, wd_ref, lw_ref, lb_ref, o_hbm, ob_hbm,
                o_ref, ob_ref,
                xg_ref, gates_ref, grpc_ref, keyc_ref, grpr_ref, keyr_ref, cnt_ref,
                xs_ref, gw_ref, ych_ref, yacc_ref):
    del o_hbm, ob_hbm
    e = pl.program_id(1) * MOE_EXPERTS_PER_STEP
    g = e // EXPERTS_PER_GROUP
    first_j = e % EXPERTS_PER_GROUP
    d, bt = xt_ref.shape
    n = MOE_CHUNK
    gr = MOE_GATE_ROWS
    max_chunks = xs_ref.shape[0]

    @pl.when(e == 0)
    def _route_block():
        xt = xt_ref[...]
        xg_ref[:d, :] = xt.astype(BF16)
        logits = jnp.dot(rwt_ref[...], xt, preferred_element_type=F32, precision=HIGHEST) + rb_ref[...]
        gates, best_group = _route(logits)
        gidx = lax.broadcasted_iota(jnp.int32, (SUBLANES, bt), 0)
        onehot = jnp.where(gidx == best_group, 1.0, 0.0)
        rank = jnp.dot(onehot.astype(BF16), tri_ref[...], preferred_element_type=F32)
        key = jnp.sum(onehot * rank, axis=0, keepdims=True)
        grp_f = best_group.astype(F32)
        grpr_ref[...] = grp_f
        keyr_ref[...] = key
        stack = jnp.concatenate([grp_f, key, jnp.zeros((LANES - 2, bt), F32)], axis=0)
        col = stack.T
        grpc_ref[...] = col[:, 0:1]
        keyc_ref[...] = col[:, 1:2]
        for gg in range(N_GROUPS):
            cnt_ref[gg] = jnp.sum(onehot[gg:gg + 1, :]).astype(jnp.int32)
            gates_ref[gg] = jnp.concatenate(
                [gates[gg * EXPERTS_PER_GROUP:(gg + 1) * EXPERTS_PER_GROUP],
                 jnp.zeros((gr - EXPERTS_PER_GROUP, bt), F32)], axis=0)
        yacc_ref[...] = jnp.zeros_like(yacc_ref)

    count = cnt_ref[g]
    g_f = g.astype(F32)

    @pl.when(first_j == 0)
    def _gather_group():
        gt = gates_ref[g]
        hi = gt.astype(BF16)
        xg_ref[d:d + gr, :] = hi
        xg_ref[d + gr:d + 2 * gr, :] = (gt - hi.astype(F32)).astype(BF16)
        key_mine = jnp.where(grpc_ref[...] == g_f, keyc_ref[...], -1.0)
        lane = lax.broadcasted_iota(jnp.int32, (bt, n), 1).astype(F32)
        for c in range(max_chunks):
            @pl.when(c * n < count)
            def _():
                selt = jnp.where(key_mine - float(c * n) == lane, 1.0, 0.0).astype(BF16)
                gx = jnp.dot(xg_ref[...], selt, preferred_element_type=F32)
                xs_ref[c] = gx[:d].astype(BF16)
                gw = gx[d:d + gr] + gx[d + gr:d + 2 * gr]
                for jj in range(EXPERTS_PER_GROUP):
                    gw_ref[c, jj] = jnp.broadcast_to(gw[jj:jj + 1, :], (SUBLANES, n))
                ych_ref[c] = jnp.zeros((d, n), F32)

    for c in range(max_chunks):
        @pl.when(c * n < count)
        def _():
            xs = xs_ref[c]
            half = D_EXPERT // 2
            y = ych_ref[c]
            for ee in range(MOE_EXPERTS_PER_STEP):
                gw = gw_ref[c, first_j + ee][0:1, :]
                gu = [(jnp.dot(wgu_ref[ee, s * half:(s + 1) * half, :], xs, preferred_element_type=F32),
                       jnp.dot(wgu_ref[ee, D_EXPERT + s * half:D_EXPERT + (s + 1) * half, :], xs,
                               preferred_element_type=F32)) for s in range(2)]
                for s in range(2):
                    hid = (_silu(gu[s][0]) * gu[s][1] * gw).astype(BF16)
                    y = y + jnp.dot(wd_ref[ee, :, s * half:(s + 1) * half], hid, preferred_element_type=F32)
            ych_ref[c] = y

    @pl.when(first_j + MOE_EXPERTS_PER_STEP == EXPERTS_PER_GROUP)
    def _scatter_group():
        key_mine = jnp.where(grpr_ref[...] == g_f, keyr_ref[...], -1.0)
        sub = lax.broadcasted_iota(jnp.int32, (n, bt), 0).astype(F32)
        for c in range(max_chunks):
            @pl.when(c * n < count)
            def _():
                sel = jnp.where(key_mine - float(c * n) == sub, 1.0, 0.0).astype(BF16)
                yacc_ref[...] += jnp.dot(ych_ref[c].astype(BF16), sel, preferred_element_type=F32)

    @pl.when(e == N_EXPERTS - MOE_EXPERTS_PER_STEP)
    def _finish():
        for c in range(bt // MOE_SLAB):
            cols = slice(c * MOE_SLAB, (c + 1) * MOE_SLAB)
            z = DN_ALPHA * xt_ref[:, cols] + yacc_ref[:, cols]
            y = _layer_norm(z.T, lw_ref[...], lb_ref[...])
            o_ref[cols, :] = y
            ob_ref[cols, :] = y.astype(BF16)


def _transpose_cast_kernel(*refs):
    *in_refs, o_ref = refs
    rows = 0
    for w_ref in in_refs:
        wt = w_ref[0, 0].astype(F32).T
        o_ref[0, rows:rows + wt.shape[0], :] = wt.astype(o_ref.dtype)
        rows += wt.shape[0]


def transpose_cast(layer, *ws):
    _, ne, r, c = ws[0].shape
    return pl.pallas_call(
        _transpose_cast_kernel,
        grid=(ne,),
        in_specs=[pl.BlockSpec((1, 1, r, c), lambda e: (layer, e, 0, 0)) for _ in ws],
        out_specs=pl.BlockSpec((1, len(ws) * c, r), lambda e: (e, 0, 0)),
        out_shape=jax.ShapeDtypeStruct((ne, len(ws) * c, r), BF16),
        compiler_params=_params("parallel"),
        name="transpose_cast",
    )(*ws)


def _moe_blocks(xt, rwt, rb, wgu_t, wd_t, lw, lb, out_f32, out_bf16, *, bt, first_block, num_blocks):
    d, t = xt.shape
    n = MOE_CHUNK
    max_chunks = pl.cdiv(bt, n)
    assert bt % MOE_SLAB == 0 and (first_block + num_blocks) * bt <= t
    tri = jnp.asarray(np.triu(np.ones((bt, bt), np.float32), k=1), dtype=BF16)
    return pl.pallas_call(
        _moe_kernel,
        grid=(num_blocks, N_EXPERTS // MOE_EXPERTS_PER_STEP),
        in_specs=[pl.BlockSpec((d, bt), lambda i, e: (0, first_block + i)),
                  pl.BlockSpec((N_EXPERTS, d), lambda i, e: (0, 0)),
                  pl.BlockSpec((N_EXPERTS, 1), lambda i, e: (0, 0)),
                  pl.BlockSpec((bt, bt), lambda i, e: (0, 0)),
                  pl.BlockSpec((MOE_EXPERTS_PER_STEP, 2 * D_EXPERT, d), lambda i, e: (e, 0, 0)),
                  pl.BlockSpec((MOE_EXPERTS_PER_STEP, d, D_EXPERT), lambda i, e: (e, 0, 0)),
                  pl.BlockSpec((1, d), lambda i, e: (0, 0)),
                  pl.BlockSpec((1, d), lambda i, e: (0, 0)),
                  pl.BlockSpec(memory_space=pl.ANY),
                  pl.BlockSpec(memory_space=pl.ANY)],
        out_specs=[pl.BlockSpec((bt, d), lambda i, e: (first_block + i, 0)),
                   pl.BlockSpec((bt, d), lambda i, e: (first_block + i, 0))],
        out_shape=[jax.ShapeDtypeStruct((t, d), F32),
                   jax.ShapeDtypeStruct((t, d), BF16)],
        input_output_aliases={8: 0, 9: 1},
        scratch_shapes=[pltpu.VMEM((d + 2 * MOE_GATE_ROWS, bt), BF16),
                        pltpu.VMEM((N_GROUPS, MOE_GATE_ROWS, bt), F32),
                        pltpu.VMEM((bt, 1), F32), pltpu.VMEM((bt, 1), F32),
                        pltpu.VMEM((1, bt), F32), pltpu.VMEM((1, bt), F32),
                        pltpu.SMEM((N_GROUPS,), jnp.int32),
                        pltpu.VMEM((max_chunks, d, n), BF16),
                        pltpu.VMEM((max_chunks, EXPERTS_PER_GROUP, SUBLANES, n), F32),
                        pltpu.VMEM((max_chunks, d, n), F32),
                        pltpu.VMEM((d, bt), F32)],
        compiler_params=pltpu.CompilerParams(dimension_semantics=("arbitrary", "arbitrary"),
                                             vmem_limit_bytes=MOE_VMEM_LIMIT),
        name="moe_residual_ln",
    )(xt, rwt, rb, tri, wgu_t, wd_t, lw.reshape(1, d), lb.reshape(1, d), out_f32, out_bf16)


def moe_residual_ln(xt, rwt, rb, wgu_t, wd_t, lw, lb, buf_f32, buf_bf16):
    d, t = xt.shape
    head_blocks = max((k for k in range(t // MOE_BLOCK + 1)
                       if (t - k * MOE_BLOCK) % MOE_TAIL_BLOCK == 0), default=0)
    tail_start = head_blocks * MOE_BLOCK
    assert (t - tail_start) % MOE_TAIL_BLOCK == 0
    out = (buf_f32, buf_bf16)
    if head_blocks:
        out = _moe_blocks(xt, rwt, rb, wgu_t, wd_t, lw, lb, *out,
                          bt=MOE_BLOCK, first_block=0, num_blocks=head_blocks)
    if tail_start < t:
        out = _moe_blocks(xt, rwt, rb, wgu_t, wd_t, lw, lb, *out, bt=MOE_TAIL_BLOCK,
                          first_block=tail_start // MOE_TAIL_BLOCK,
                          num_blocks=(t - tail_start) // MOE_TAIL_BLOCK)
    return out


def _lambda_init_for(layer):
    return 0.8 - 0.6 * math.exp(-0.3 * layer)


def kernel(x, a_w_in, a_conv, a_A_log, a_dt_bias, a_norm_w, a_w_out, b_w_q, b_w_kv, b_lambda, b_subln_w, b_w_out, rel_bias, router_w, router_b, moe_w_gate, moe_w_up, moe_w_down, ln_w, ln_b):
    bsz, t, d = x.shape
    assert bsz == 1 and d == D_MODEL and t % A_CHUNK == 0
    xf = x.reshape(t, d).astype(F32)
    xb = xf.astype(BF16)

    gate_tile = min(512, t)
    tri = jnp.asarray(np.kron(np.eye(gate_tile // A_CHUNK, dtype=np.float32),
                              np.tril(np.ones((A_CHUNK, A_CHUNK), np.float32))))
    rwt = router_w.T.astype(F32)
    rb = router_b.reshape(N_EXPERTS, 1).astype(F32)
    attn_tile = min(512, t)
    assert attn_tile % BIAS_BLOCK == 0
    bias_tables = attention_bias_tables(rel_bias.astype(F32))

    kb = None
    vt = None
    for l in range(DEPTH):
        if l < N_A_LAYERS:
            w_in = a_w_in[l]
            w_main = w_in[:, :A_CONV_CH + A_V].astype(BF16)
            w_ab = jnp.pad(w_in[:, A_CONV_CH + A_V:], ((0, 0), (0, LANES - 2 * A_HEADS))).astype(BF16)
            conv_pad = jnp.pad(a_conv[l].astype(F32), ((0, 0), (0, A_V)))
            par = jnp.zeros((SUBLANES, LANES), F32)
            par = par.at[0, :A_HEADS].set(a_A_log[l].astype(F32))
            par = par.at[1, :A_HEADS].set(a_dt_bias[l].astype(F32))
            proj = gdn_inproj(xb, w_main, conv_pad)
            fac, fact = gdn_gates(xb, w_ab, par, tri)
            fact3 = fact.reshape(2 * A_HEADS, t // A_CHUNK, A_CHUNK).transpose(1, 0, 2)
            u, wq, qk, kd, dec = gdn_intra(proj, fac, fact3)
            mixed = gdn_scan(u, wq, qk, kd, dec, proj, a_norm_w[l].astype(F32))
            w_out = a_w_out[l].astype(BF16)
        else:
            j = l - N_A_LAYERS
            if kb is None:
                kb = matmul_bf16(xb, b_w_kv[:, :B_QK].astype(BF16))
                vt = v_transposed_proj(xb, b_w_kv[:, B_QK:].T.astype(BF16), attn_tile)
            qb = matmul_bf16(xb, b_w_q[j].astype(BF16), scale=B_DH ** -0.5 * LOG2E)
            mixed = diff_attention(qb, kb, vt, bias_tables, b_lambda[j].astype(F32),
                                   b_subln_w[j].astype(F32), _lambda_init_for(l), tile=attn_tile)
            w_out = b_w_out[j].astype(BF16)
        xt = matmul_residual_ln_t(mixed, w_out, xf, ln_w[l, 0].astype(F32), ln_b[l, 0].astype(F32))
        wgu_t = transpose_cast(l, moe_w_gate, moe_w_up)
        wd_t = transpose_cast(l, moe_w_down)
        xf, xb = moe_residual_ln(xt, rwt, rb, wgu_t, wd_t,
                                 ln_w[l, 1].astype(F32), ln_b[l, 1].astype(F32), xf, xb)
    return xf.reshape(bsz, t, d).astype(x.dtype)
```

```python
import functools
import math

import jax
import jax.numpy as jnp
import numpy as np
from jax import lax
from jax.experimental import pallas as pl
from jax.experimental.pallas import tpu as pltpu

F32 = jnp.float32
BF16 = jnp.bfloat16

D_MODEL = 1024
DEPTH = 4
N_A_LAYERS = DEPTH // 2
A_HEADS = 8
A_DK = 128
A_DV = 128
A_QK = A_HEADS * A_DK
A_V = A_HEADS * A_DV
A_CONV = 4
A_CONV_CH = 2 * A_QK + A_V
A_CHUNK = 64
B_HEADS = 8
B_DH = 64
B_QK = B_HEADS * 2 * B_DH
REL_BUCKETS = 32
REL_MAX_DIST = 128
N_EXPERTS = 16
N_GROUPS = 4
EXPERTS_PER_GROUP = N_EXPERTS // N_GROUPS
D_EXPERT = 512
DN_ALPHA = (2 * DEPTH) ** 0.25
LN_EPS = 1e-5
RMS_EPS = 1e-6

LANES = 128
SUBLANES = 8
VMEM_LIMIT = 48 * 1024 * 1024
MASK_VALUE = -1e30

HIGHEST = lax.Precision.HIGHEST
NT_DIMS = (((1,), (1,)), ((), ()))
TN_DIMS = (((0,), (0,)), ((), ()))


def _params(*sem):
    return pltpu.CompilerParams(dimension_semantics=sem, vmem_limit_bytes=VMEM_LIMIT)


def _sigmoid(x):
    return 1.0 / (1.0 + jnp.exp(-x))


def _silu(x):
    return x * _sigmoid(x)


def _layer_norm(v, w, b):
    mu = jnp.mean(v, axis=-1, keepdims=True)
    c = v - mu
    var = jnp.mean(c * c, axis=-1, keepdims=True)
    return c * lax.rsqrt(var + LN_EPS) * w + b


def _mm_kernel(a_ref, w_ref, o_ref, *, scale):
    y = jnp.dot(a_ref[...], w_ref[...], preferred_element_type=F32)
    if scale != 1.0:
        y = y * scale
    o_ref[...] = y.astype(o_ref.dtype)


def matmul_bf16(a, w, *, scale=1.0, tm=1024, tn=1024):
    m, k = a.shape
    n = w.shape[1]
    tm = min(tm, m)
    return pl.pallas_call(
        functools.partial(_mm_kernel, scale=scale),
        grid=(n // tn, m // tm),
        in_specs=[pl.BlockSpec((tm, k), lambda j, i: (i, 0)),
                  pl.BlockSpec((k, tn), lambda j, i: (0, j))],
        out_specs=pl.BlockSpec((tm, tn), lambda j, i: (i, j)),
        out_shape=jax.ShapeDtypeStruct((m, n), BF16),
        compiler_params=_params("parallel", "arbitrary"),
        name="matmul_bf16",
    )(a, w)


def _mm_ln_kernel(a_ref, w_ref, x_ref, lw_ref, lb_ref, ot_ref):
    tm = a_ref.shape[0]
    parts = 2
    pr = tm // parts
    hs = [jnp.dot(a_ref[p * pr:(p + 1) * pr, :], w_ref[...], preferred_element_type=F32) for p in range(parts)]
    for p, h in enumerate(hs):
        y = _layer_norm(DN_ALPHA * x_ref[p * pr:(p + 1) * pr, :] + h, lw_ref[...], lb_ref[...])
        ot_ref[:, p * pr:(p + 1) * pr] = y.T


def matmul_residual_ln_t(a, w, x, lw, lb, *, tm=1024):
    m, k = a.shape
    n = w.shape[1]
    tm = min(tm, m)
    return pl.pallas_call(
        _mm_ln_kernel,
        grid=(m // tm,),
        in_specs=[pl.BlockSpec((tm, k), lambda i: (i, 0)),
                  pl.BlockSpec((k, n), lambda i: (0, 0)),
                  pl.BlockSpec((tm, n), lambda i: (i, 0)),
                  pl.BlockSpec((1, n), lambda i: (0, 0)),
                  pl.BlockSpec((1, n), lambda i: (0, 0))],
        out_specs=pl.BlockSpec((n, tm), lambda i: (0, i)),
        out_shape=jax.ShapeDtypeStruct((n, m), F32),
        compiler_params=_params("parallel"),
        name="matmul_residual_ln",
    )(a, w, x, lw.reshape(1, n), lb.reshape(1, n))


def _gdn_inproj_kernel(x_ref, w_ref, cw_ref, o_ref, halo_ref):
    n = pl.program_id(0)
    m = pl.program_id(1)
    tm, tn = o_ref.shape
    parts = 2
    pr = tm // parts

    @pl.when(m == 0)
    def _():
        halo_ref[...] = jnp.zeros_like(halo_ref)

    def project():
        return [jnp.dot(x_ref[p * pr:(p + 1) * pr, :], w_ref[...], preferred_element_type=F32)
                for p in range(parts)]

    def conv_silu(ys):
        cw = cw_ref[...]
        rows = lax.broadcasted_iota(jnp.int32, (SUBLANES, tn), 0)
        out = []
        halo = halo_ref[...]
        for y in ys:
            acc = y * cw[A_CONV - 1:A_CONV]
            for shift in range(1, A_CONV):
                sh = pltpu.roll(y, shift, 0)
                top = jnp.where(rows < shift, pltpu.roll(halo, shift, 0), sh[:SUBLANES])
                sh = jnp.concatenate([top, sh[SUBLANES:]], axis=0)
                acc = acc + sh * cw[A_CONV - 1 - shift:A_CONV - shift]
            halo = y[pr - SUBLANES:]
            out.append(_silu(acc))
        halo_ref[...] = halo
        return out

    @pl.when(n == 3)
    def _():
        for p, y in enumerate(project()):
            o_ref[p * pr:(p + 1) * pr, :] = y.astype(o_ref.dtype)

    @pl.when(n == 2)
    def _():
        for p, a in enumerate(conv_silu(project())):
            o_ref[p * pr:(p + 1) * pr, :] = a.astype(o_ref.dtype)

    @pl.when(n < 2)
    def _():
        post = jnp.where(n == 0, A_DK ** -0.5, 1.0).astype(F32)
        for p, a in enumerate(conv_silu(project())):
            for h in range(tn // A_DK):
                seg = a[:, h * A_DK:(h + 1) * A_DK]
                ss = jnp.sum(seg * seg, axis=-1, keepdims=True)
                o_ref[p * pr:(p + 1) * pr, h * A_DK:(h + 1) * A_DK] = (
                    seg * (lax.rsqrt(ss + RMS_EPS) * post)).astype(o_ref.dtype)


def gdn_inproj(xb, w_main, conv_w_pad, *, tm=1024):
    t, k = xb.shape
    n = w_main.shape[1]
    tn = A_QK
    tm = min(tm, t)
    return pl.pallas_call(
        _gdn_inproj_kernel,
        grid=(n // tn, t // tm),
        in_specs=[pl.BlockSpec((tm, k), lambda j, i: (i, 0)),
                  pl.BlockSpec((k, tn), lambda j, i: (0, j)),
                  pl.BlockSpec((A_CONV, tn), lambda j, i: (0, j))],
        out_specs=pl.BlockSpec((tm, tn), lambda j, i: (i, j)),
        out_shape=jax.ShapeDtypeStruct((t, n), BF16),
        scratch_shapes=[pltpu.VMEM((SUBLANES, tn), F32)],
        compiler_params=_params("arbitrary", "arbitrary"),
        name="gdn_inproj",
    )(xb, w_main, conv_w_pad)


def _gdn_gates_kernel(x_ref, w_ref, par_ref, tri_ref, fac_ref, fact_ref):
    ab = jnp.dot(x_ref[...], w_ref[...], preferred_element_type=F32)
    lane = lax.broadcasted_iota(jnp.int32, ab.shape, 1)
    neg_a = -jnp.exp(par_ref[0:1, :])
    z = ab + par_ref[1:2, :]
    softplus = jnp.maximum(z, 0.0) + jnp.log1p(jnp.exp(-jnp.abs(z)))
    g = jnp.where(lane < A_HEADS, neg_a * softplus, 0.0)
    beta = _sigmoid(ab)
    gc = jnp.dot(tri_ref[...], g, preferred_element_type=F32, precision=HIGHEST)
    fac = jnp.where(lane < A_HEADS, gc, jnp.where(lane < 2 * A_HEADS, beta, 0.0))
    fac_ref[...] = fac
    fact_ref[...] = fac.T[:2 * A_HEADS, :]


def gdn_gates(xb, w_ab_pad, par, tri, *, tm=512):
    t, k = xb.shape
    tm = min(tm, t)
    return pl.pallas_call(
        _gdn_gates_kernel,
        grid=(t // tm,),
        in_specs=[pl.BlockSpec((tm, k), lambda i: (i, 0)),
                  pl.BlockSpec((k, LANES), lambda i: (0, 0)),
                  pl.BlockSpec((SUBLANES, LANES), lambda i: (0, 0)),
                  pl.BlockSpec((tm, tm), lambda i: (0, 0))],
        out_specs=[pl.BlockSpec((tm, LANES), lambda i: (i, 0)),
                   pl.BlockSpec((2 * A_HEADS, tm), lambda i: (0, i))],
        out_shape=[jax.ShapeDtypeStruct((t, LANES), F32),
                   jax.ShapeDtypeStruct((2 * A_HEADS, t), F32)],
        compiler_params=_params("parallel"),
        name="gdn_gates",
    )(xb, w_ab_pad, par, tri)


GDN_INTRA_CHUNKS = 4
GDN_SCAN_CHUNKS = 16


def _mm_bf16(a, b):
    return jnp.dot(a.astype(BF16), b.astype(BF16), preferred_element_type=F32)


def _gdn_intra_kernel(q_ref, k_ref, v_ref, fac_ref, fact_ref, u_ref, wq_ref, qk_ref, kd_ref, dec_ref):
    c = A_CHUNK
    ii = lax.broadcasted_iota(jnp.int32, (c, c), 0)
    jj = lax.broadcasted_iota(jnp.int32, (c, c), 1)
    incl = ii >= jj
    strict = ii > jj
    units = [(cb, h) for cb in range(GDN_INTRA_CHUNKS) for h in range(A_HEADS)]
    rows = {cb: slice(cb * c, (cb + 1) * c) for cb in range(GDN_INTRA_CHUNKS)}
    cols = {h: slice(h * A_DK, (h + 1) * A_DK) for h in range(A_HEADS)}

    q, k, kf, beta, eg, decay = {}, {}, {}, {}, {}, {}
    for cb, h in units:
        fac = fac_ref[rows[cb], :]
        gc_col = fac[:, h:h + 1]
        gc_row = fact_ref[cb][h:h + 1, :]
        g_last = gc_col[c - 1:c, :]
        beta[cb, h] = fac[:, A_HEADS + h:A_HEADS + h + 1]
        eg[cb, h] = jnp.exp(gc_col)
        decay[cb, h] = jnp.exp(jnp.where(incl, gc_col - gc_row, MASK_VALUE))
        q[cb, h] = q_ref[rows[cb], cols[h]]
        k[cb, h] = k_ref[rows[cb], cols[h]]
        kf[cb, h] = k[cb, h].astype(F32)
        kd_ref[rows[cb], cols[h]] = (kf[cb, h] * jnp.exp(g_last - gc_col)).astype(kd_ref.dtype)
        dec_ref[cb, h:h + 1, :] = jnp.broadcast_to(jnp.exp(g_last), (1, A_DV))
        wq_ref[cb, c:, cols[h]] = (q[cb, h].astype(F32) * eg[cb, h]).astype(wq_ref.dtype)

    mmat, x, p = {}, {}, {}
    for u_ in units:
        kk = lax.dot_general(k[u_], k[u_], NT_DIMS, preferred_element_type=F32)
        mmat[u_] = jnp.where(strict, beta[u_] * kk * decay[u_], 0.0)
    for u_ in units:
        cb, h = u_
        qk = lax.dot_general(q[u_], k[u_], NT_DIMS, preferred_element_type=F32) * decay[u_]
        qk_ref[cb, h] = qk.astype(qk_ref.dtype)
    for u_ in units:
        x[u_] = -mmat[u_]
        p[u_] = _mm_bf16(mmat[u_], mmat[u_])
    for level in range(5):
        for u_ in units:
            x[u_] = x[u_] + p[u_] + _mm_bf16(x[u_], p[u_])
        if level < 4:
            for u_ in units:
                p[u_] = _mm_bf16(p[u_], p[u_])
    for u_ in units:
        cb, h = u_
        v = v_ref[rows[cb], cols[h]].astype(F32)
        rhs = jnp.concatenate([beta[u_] * v, (beta[u_] * eg[u_]) * kf[u_]], axis=1)
        uw = rhs + _mm_bf16(x[u_], rhs)
        u_ref[rows[cb], cols[h]] = uw[:, :A_DV].astype(u_ref.dtype)
        wq_ref[cb, :c, cols[h]] = uw[:, A_DV:].astype(wq_ref.dtype)


def gdn_intra(proj, fac, fact3):
    t = proj.shape[0]
    c = A_CHUNK
    cb = GDN_INTRA_CHUNKS
    nc = t // c
    return pl.pallas_call(
        _gdn_intra_kernel,
        grid=(nc // cb,),
        in_specs=[pl.BlockSpec((cb * c, A_QK), lambda i: (i, 0)),
                  pl.BlockSpec((cb * c, A_QK), lambda i: (i, 1)),
                  pl.BlockSpec((cb * c, A_V), lambda i: (i, 2)),
                  pl.BlockSpec((cb * c, LANES), lambda i: (i, 0)),
                  pl.BlockSpec((cb, 2 * A_HEADS, c), lambda i: (i, 0, 0))],
        out_specs=[pl.BlockSpec((cb * c, A_V), lambda i: (i, 0)),
                   pl.BlockSpec((cb, 2 * c, A_QK), lambda i: (i, 0, 0)),
                   pl.BlockSpec((cb, A_HEADS, c, c), lambda i: (i, 0, 0, 0)),
                   pl.BlockSpec((cb * c, A_QK), lambda i: (i, 0)),
                   pl.BlockSpec((cb, A_HEADS, A_DV), lambda i: (i, 0, 0))],
        out_shape=[jax.ShapeDtypeStruct((t, A_V), BF16),
                   jax.ShapeDtypeStruct((nc, 2 * c, A_QK), BF16),
                   jax.ShapeDtypeStruct((nc, A_HEADS, c, c), BF16),
                   jax.ShapeDtypeStruct((t, A_QK), BF16),
                   jax.ShapeDtypeStruct((nc, A_HEADS, A_DV), F32)],
        compiler_params=_params("parallel"),
        name="gdn_intra",
    )(proj, proj, proj, fac, fact3)


def _gdn_scan_kernel(u_ref, wq_ref, qk_ref, kd_ref, dec_ref, z_ref, nw_ref, o_ref, s_ref):
    c = A_CHUNK

    @pl.when(pl.program_id(0) == 0)
    def _():
        s_ref[...] = jnp.zeros_like(s_ref)

    nw = nw_ref[...]
    heads = range(A_HEADS)
    cols = [slice(h * A_DK, (h + 1) * A_DK) for h in heads]
    for cb in range(GDN_SCAN_CHUNKS):
        rows = slice(cb * c, (cb + 1) * c)
        s = [s_ref[h] for h in heads]
        sb = [s[h].astype(BF16) for h in heads]
        ws_qs = [jnp.dot(wq_ref[cb, :, cols[h]], sb[h], preferred_element_type=F32) for h in heads]
        v_new = [(u_ref[rows, cols[h]].astype(F32) - ws_qs[h][:c]).astype(BF16) for h in heads]
        o = [ws_qs[h][c:] + jnp.dot(qk_ref[cb, h], v_new[h], preferred_element_type=F32) for h in heads]
        for h in heads:
            s_ref[h] = s[h] * dec_ref[cb, h:h + 1, :] + lax.dot_general(
                kd_ref[rows, cols[h]], v_new[h], TN_DIMS, preferred_element_type=F32)
        for h in heads:
            z = z_ref[rows, cols[h]].astype(F32)
            ms = jnp.mean(o[h] * o[h], axis=-1, keepdims=True)
            o_ref[rows, cols[h]] = (o[h] * lax.rsqrt(ms + RMS_EPS) * nw * _silu(z)).astype(o_ref.dtype)


def gdn_scan(u, wq, qk, kd, dec, proj, norm_w):
    t = u.shape[0]
    c = A_CHUNK
    cb = GDN_SCAN_CHUNKS
    return pl.pallas_call(
        _gdn_scan_kernel,
        grid=(t // (cb * c),),
        in_specs=[pl.BlockSpec((cb * c, A_V), lambda i: (i, 0)),
                  pl.BlockSpec((cb, 2 * c, A_QK), lambda i: (i, 0, 0)),
                  pl.BlockSpec((cb, A_HEADS, c, c), lambda i: (i, 0, 0, 0)),
                  pl.BlockSpec((cb * c, A_QK), lambda i: (i, 0)),
                  pl.BlockSpec((cb, A_HEADS, A_DV), lambda i: (i, 0, 0)),
                  pl.BlockSpec((cb * c, A_V), lambda i: (i, 3)),
                  pl.BlockSpec((1, A_DV), lambda i: (0, 0))],
        out_specs=pl.BlockSpec((cb * c, A_V), lambda i: (i, 0)),
        out_shape=jax.ShapeDtypeStruct((t, A_V), BF16),
        scratch_shapes=[pltpu.VMEM((A_HEADS, A_DK, A_DV), F32)],
        compiler_params=_params("arbitrary"),
        name="gdn_scan",
    )(u, wq, qk, kd, dec, proj, norm_w.reshape(1, A_DV))


LOG2E = math.log2(math.e)
VT_ROWS = 2 * B_DH + 16


def _vt_proj_kernel(x_ref, w_ref, o_ref):
    tk = x_ref.shape[0]
    y = lax.dot_general(w_ref[...], x_ref[...], NT_DIMS, preferred_element_type=F32)
    ones = jnp.ones((VT_ROWS - 2 * B_DH, tk), o_ref.dtype)
    for h in range(B_HEADS):
        o_ref[0, h, :2 * B_DH, :] = y[h * 2 * B_DH:(h + 1) * 2 * B_DH].astype(o_ref.dtype)
        o_ref[0, h, 2 * B_DH:, :] = ones


def v_transposed_proj(xb, w_vt, tile):
    t, k = xb.shape
    n = w_vt.shape[0]
    return pl.pallas_call(
        _vt_proj_kernel,
        grid=(t // tile,),
        in_specs=[pl.BlockSpec((tile, k), lambda i: (i, 0)),
                  pl.BlockSpec((n, k), lambda i: (0, 0))],
        out_specs=pl.BlockSpec((1, B_HEADS, VT_ROWS, tile), lambda i: (i, 0, 0, 0)),
        out_shape=jax.ShapeDtypeStruct((t // tile, B_HEADS, VT_ROWS, tile), BF16),
        compiler_params=_params("parallel"),
        name="v_transposed_proj",
    )(xb, w_vt)


ATTN_HEADS_PER_STEP = 2
BIAS_BLOCK = REL_MAX_DIST


def _attn_kernel(q_ref, k_ref, vt_ref, tab_ref, lam_ref, swb_ref, o_ref, m_ref, acc_ref,
                 st_ref, mt_ref, bias_ref, *, lambda_init, tile):
    qi = pl.program_id(1)
    dv = 2 * B_DH
    nh = ATTN_HEADS_PER_STEP
    first_map = lax.broadcasted_iota(jnp.int32, (tile, dv), 1) < B_DH

    @pl.when(qi == 0)
    def _():
        nblk = tile // BIAS_BLOCK
        zeros = jnp.zeros((BIAS_BLOCK, BIAS_BLOCK), F32)
        masked = jnp.full((BIAS_BLOCK, BIAS_BLOCK), MASK_VALUE, F32)
        for h in range(nh):
            for ja in range(nblk):
                for ib in range(nblk):
                    rows = slice(ja * BIAS_BLOCK, (ja + 1) * BIAS_BLOCK)
                    cols = slice(ib * BIAS_BLOCK, (ib + 1) * BIAS_BLOCK)
                    if ib == ja:
                        blk = tab_ref[h, 0]
                    elif ib == ja + 1:
                        blk = tab_ref[h, 1]
                    else:
                        blk = zeros if ib > ja else masked
                    bias_ref[h, 0, rows, cols] = blk
                    bias_ref[h, 1, rows, cols] = tab_ref[h, 1] if (ja == nblk - 1 and ib == 0) else zeros

    m_ref[...] = jnp.full_like(m_ref, MASK_VALUE)
    acc_ref[...] = jnp.zeros_like(acc_ref)

    def produce(h, j):
        hc = slice(h * dv, (h + 1) * dv)
        q = q_ref[:, hc]
        k = k_ref[pl.ds(pl.multiple_of(j * tile, tile), tile), hc]
        zero = jnp.zeros_like(k)
        k2 = jnp.concatenate([jnp.where(first_map, k, zero), jnp.where(first_map, zero, k)], axis=0)
        st = lax.dot_general(k2, q, NT_DIMS, preferred_element_type=F32)
        st_ref[h] = st
        for i in range(2):
            mt_ref[h, i] = jnp.max(st[i * tile:(i + 1) * tile], axis=0, keepdims=True)

    def consume(h, j, slot):
        vt = vt_ref[j, h]
        for i in range(2):
            s = st_ref[h, i * tile:(i + 1) * tile, :]
            m_prev = m_ref[h, i]
            if slot is None:
                m_new = jnp.maximum(m_prev, mt_ref[h, i])
            else:
                s = s + bias_ref[h, slot]
                m_new = jnp.maximum(m_prev, jnp.max(s, axis=0, keepdims=True))
            alpha = jnp.exp2(m_prev - m_new)
            p = jnp.exp2(s - m_new).astype(BF16)
            acc_ref[h, i] = alpha * acc_ref[h, i] + jnp.dot(vt, p, preferred_element_type=F32)
            m_ref[h, i] = m_new

    n_far = jnp.maximum(qi - 1, 0)
    produce(0, 0)

    def far_units(j):
        produce(1, j)
        consume(0, j, None)
        produce(0, j + 1)
        consume(1, j, None)

    def far_pair_body(jj, carry):
        far_units(2 * jj)
        far_units(2 * jj + 1)
        return carry

    lax.fori_loop(0, n_far // 2, far_pair_body, 0)

    @pl.when(n_far % 2 == 1)
    def _():
        far_units(n_far - 1)

    @pl.when(qi >= 1)
    def _():
        produce(1, qi - 1)
        consume(0, qi - 1, 1)
        produce(0, qi)
        consume(1, qi - 1, 1)
        produce(1, qi)
        consume(0, qi, 0)
        consume(1, qi, 0)

    @pl.when(qi == 0)
    def _():
        produce(1, 0)
        consume(0, 0, 0)
        consume(1, 0, 0)

    lam = lam_ref[...]
    lam_val = (jnp.exp(jnp.sum(lam[0:1] * lam[1:2], axis=-1, keepdims=True))
               - jnp.exp(jnp.sum(lam[2:3] * lam[3:4], axis=-1, keepdims=True)) + lambda_init)
    for h in range(nh):
        a0 = acc_ref[h, 0]
        a1 = acc_ref[h, 1]
        ot = a0[:dv] / a0[dv:dv + 1] - lam_val * (a1[:dv] / a1[dv:dv + 1])
        ms = jnp.mean(ot * ot, axis=0, keepdims=True)
        ot = ot * lax.rsqrt(ms + RMS_EPS) * swb_ref[...] * (1.0 - lambda_init)
        o_ref[:, h * dv:(h + 1) * dv] = ot.T.astype(o_ref.dtype)


def diff_attention(qb, kb, vt, bias_tables, lam, subln_w, lambda_init, *, tile):
    t = qb.shape[0]
    nb = t // tile
    dv = 2 * B_DH
    nh = ATTN_HEADS_PER_STEP
    resident = pl.Buffered(1)
    swb = jnp.broadcast_to(subln_w.reshape(dv, 1), (dv, tile))
    return pl.pallas_call(
        functools.partial(_attn_kernel, lambda_init=lambda_init, tile=tile),
        grid=(B_HEADS // nh, nb),
        in_specs=[pl.BlockSpec((tile, nh * dv), lambda h, i: (i, h)),
                  pl.BlockSpec((t, nh * dv), lambda h, i: (0, h), pipeline_mode=resident),
                  pl.BlockSpec((nb, nh, VT_ROWS, tile), lambda h, i: (0, h, 0, 0), pipeline_mode=resident),
                  pl.BlockSpec((nh, 2, BIAS_BLOCK, BIAS_BLOCK), lambda h, i: (h, 0, 0, 0)),
                  pl.BlockSpec((4, B_DH), lambda h, i: (0, 0)),
                  pl.BlockSpec((dv, tile), lambda h, i: (0, 0))],
        out_specs=pl.BlockSpec((tile, nh * dv), lambda h, i: (i, h)),
        out_shape=jax.ShapeDtypeStruct((t, B_HEADS * dv), BF16),
        scratch_shapes=[pltpu.VMEM((nh, 2, 1, tile), F32),
                        pltpu.VMEM((nh, 2, VT_ROWS, tile), F32),
                        pltpu.VMEM((nh, 2 * tile, tile), F32),
                        pltpu.VMEM((nh, 2, 1, tile), F32),
                        pltpu.VMEM((nh, 2, tile, tile), F32)],
        compiler_params=_params("arbitrary", "arbitrary"),
        name="diff_attention",
    )(qb, kb, vt, bias_tables, lam, swb)


def _t5_bucket(n):
    max_exact = REL_BUCKETS // 2
    nf = jnp.maximum(n, 1).astype(F32)
    large = max_exact + (jnp.log(nf / max_exact) / math.log(REL_MAX_DIST / max_exact)
                         * (REL_BUCKETS - max_exact)).astype(jnp.int32)
    large = jnp.minimum(large, REL_BUCKETS - 1)
    return jnp.where(n < max_exact, n, large)


def attention_bias_tables(rel_bias):
    tile = BIAS_BLOCK
    n = jnp.arange(2 * tile)
    far = rel_bias[REL_BUCKETS - 1]
    db = ((rel_bias[_t5_bucket(n)] - far[None, :]) * LOG2E).T.astype(F32)
    nh = db.shape[0]
    full = jnp.concatenate([jnp.full((nh, tile - 1), MASK_VALUE, F32), db], axis=1)

    def toeplitz(vec):
        w = jnp.concatenate([vec, jnp.zeros((nh, 1), F32)], axis=1)
        flat = jnp.tile(w, (1, tile))[:, :tile * (2 * tile - 1)]
        return flat.reshape(nh, tile, 2 * tile - 1)[:, :, tile - 1:]

    diag = toeplitz(full[:, :2 * tile - 1])
    near = toeplitz(full[:, tile:3 * tile - 1])
    return jnp.stack([diag, near], axis=1)


def _route(logits):
    mx = jnp.max(logits, axis=0, keepdims=True)
    ex = jnp.exp(logits - mx)
    probs = ex / jnp.sum(ex, axis=0, keepdims=True)
    row = [probs[e:e + 1, :] for e in range(N_EXPERTS)]
    epg = EXPERTS_PER_GROUP
    best_sum = None
    best_group = None
    for g in range(N_GROUPS):
        vals = row[g * epg:(g + 1) * epg]
        top2 = None
        for a in range(epg):
            for b in range(a + 1, epg):
                pair = vals[a] + vals[b]
                top2 = pair if top2 is None else jnp.maximum(top2, pair)
        if g == 0:
            best_sum = top2
            best_group = jnp.zeros_like(top2, dtype=jnp.int32)
        else:
            upd = top2 > best_sum
            best_sum = jnp.where(upd, top2, best_sum)
            best_group = jnp.where(upd, g, best_group)
    gp = []
    for j in range(epg):
        sel = row[j]
        for g in range(1, N_GROUPS):
            sel = jnp.where(best_group == g, row[g * epg + j], sel)
        gp.append(sel)
    v1 = gp[0]
    i1 = jnp.zeros_like(best_group)
    for j in range(1, epg):
        upd = gp[j] > v1
        v1 = jnp.where(upd, gp[j], v1)
        i1 = jnp.where(upd, j, i1)
    v2 = jnp.full_like(v1, -1.0)
    i2 = jnp.zeros_like(best_group)
    for j in range(epg):
        upd = (i1 != j) & (gp[j] > v2)
        v2 = jnp.where(upd, gp[j], v2)
        i2 = jnp.where(upd, j, i2)
    denom = v1 + v2
    e1 = best_group * epg + i1
    e2 = best_group * epg + i2
    eidx = lax.broadcasted_iota(jnp.int32, logits.shape, 0)
    gates = jnp.where(eidx == e1, v1 / denom, 0.0) + jnp.where(eidx == e2, v2 / denom, 0.0)
    return gates, best_group


MOE_BLOCK = 896
MOE_TAIL_BLOCK = 1024
MOE_CHUNK = 256
MOE_EXPERTS_PER_STEP = 2
MOE_SLAB = LANES
MOE_GATE_ROWS = 16
MOE_VMEM_LIMIT = 56 * 1024 * 1024


def _moe_kernel(xt_ref, rwt_ref, rb_ref, tri_ref, wgu_ref, wd_ref, lw_ref, lb_ref, o_hbm, ob_hbm,
                o_ref, ob_ref,
                xg_ref, gates_ref, grpc_ref, keyc_ref, grpr_ref, keyr_ref, cnt_ref,
                xs_ref, gw_ref, ych_ref, yacc_ref):
    del o_hbm, ob_hbm
    e = pl.program_id(1) * MOE_EXPERTS_PER_STEP
    g = e // EXPERTS_PER_GROUP
    first_j = e % EXPERTS_PER_GROUP
    d, bt = xt_ref.shape
    n = MOE_CHUNK
    gr = MOE_GATE_ROWS
    max_chunks = xs_ref.shape[0]

    @pl.when(e == 0)
    def _route_block():
        xt = xt_ref[...]
        xt_hi = xt.astype(BF16)
        xg_ref[:d, :] = xt_hi
        xt_lo = (xt - xt_hi.astype(F32)).astype(BF16)
        rw = rwt_ref[...]
        rw_hi = rw.astype(BF16)
        rw_lo = (rw - rw_hi.astype(F32)).astype(BF16)
        logits = (jnp.dot(rw_hi, xt_hi, preferred_element_type=F32)
                  + (jnp.dot(rw_hi, xt_lo, preferred_element_type=F32)
                     + jnp.dot(rw_lo, xt_hi, preferred_element_type=F32))) + rb_ref[...]
        gates, best_group = _route(logits)
        gidx = lax.broadcasted_iota(jnp.int32, (SUBLANES, bt), 0)
        onehot = jnp.where(gidx == best_group, 1.0, 0.0)
        rank = jnp.dot(onehot.astype(BF16), tri_ref[...], preferred_element_type=F32)
        key = jnp.sum(onehot * rank, axis=0, keepdims=True)
        grp_f = best_group.astype(F32)
        grpr_ref[...] = grp_f
        keyr_ref[...] = key
        stack = jnp.concatenate([grp_f, key, jnp.zeros((LANES - 2, bt), F32)], axis=0)
        col = stack.T
        grpc_ref[...] = col[:, 0:1]
        keyc_ref[...] = col[:, 1:2]
        for gg in range(N_GROUPS):
            cnt_ref[gg] = jnp.sum(onehot[gg:gg + 1, :]).astype(jnp.int32)
            gates_ref[gg] = jnp.concatenate(
                [gates[gg * EXPERTS_PER_GROUP:(gg + 1) * EXPERTS_PER_GROUP],
                 jnp.zeros((gr - EXPERTS_PER_GROUP, bt), F32)], axis=0)
        yacc_ref[...] = jnp.zeros_like(yacc_ref)

    count = cnt_ref[g]
    g_f = g.astype(F32)

    @pl.when(first_j == 0)
    def _gather_group():
        gt = gates_ref[g]
        hi = gt.astype(BF16)
        xg_ref[d:d + gr, :] = hi
        xg_ref[d + gr:d + 2 * gr, :] = (gt - hi.astype(F32)).astype(BF16)
        key_mine = jnp.where(grpc_ref[...] == g_f, keyc_ref[...], -1.0)
        lane = lax.broadcasted_iota(jnp.int32, (bt, n), 1).astype(F32)
        for c in range(max_chunks):
            @pl.when(c * n < count)
            def _():
                selt = jnp.where(key_mine - float(c * n) == lane, 1.0, 0.0).astype(BF16)
                gx = jnp.dot(xg_ref[...], selt, preferred_element_type=F32)
                xs_ref[c] = gx[:d].astype(BF16)
                gw = gx[d:d + gr] + gx[d + gr:d + 2 * gr]
                for jj in range(EXPERTS_PER_GROUP):
                    gw_ref[c, jj] = jnp.broadcast_to(gw[jj:jj + 1, :], (SUBLANES, n))
                ych_ref[c] = jnp.zeros((d, n), F32)

    for c in range(max_chunks):
        @pl.when(c * n < count)
        def _():
            xs = xs_ref[c]
            half = D_EXPERT // 2
            y = ych_ref[c]
            for ee in range(MOE_EXPERTS_PER_STEP):
                gw = gw_ref[c, first_j + ee][0:1, :]
                gu = [(jnp.dot(wgu_ref[ee, s * half:(s + 1) * half, :], xs, preferred_element_type=F32),
                       jnp.dot(wgu_ref[ee, D_EXPERT + s * half:D_EXPERT + (s + 1) * half, :], xs,
                               preferred_element_type=F32)) for s in range(2)]
                for s in range(2):
                    hid = (_silu(gu[s][0]) * gu[s][1] * gw).astype(BF16)
                    y = y + jnp.dot(wd_ref[ee, :, s * half:(s + 1) * half], hid, preferred_element_type=F32)
            ych_ref[c] = y

    @pl.when(first_j + MOE_EXPERTS_PER_STEP == EXPERTS_PER_GROUP)
    def _scatter_group():
        key_mine = jnp.where(grpr_ref[...] == g_f, keyr_ref[...], -1.0)
        sub = lax.broadcasted_iota(jnp.int32, (n, bt), 0).astype(F32)
        for c in range(max_chunks):
            @pl.when(c * n < count)
            def _():
                sel = jnp.where(key_mine - float(c * n) == sub, 1.0, 0.0).astype(BF16)
                yacc_ref[...] += jnp.dot(ych_ref[c].astype(BF16), sel, preferred_element_type=F32)

    @pl.when(e == N_EXPERTS - MOE_EXPERTS_PER_STEP)
    def _finish():
        for c in range(bt // MOE_SLAB):
            cols = slice(c * MOE_SLAB, (c + 1) * MOE_SLAB)
            z = DN_ALPHA * xt_ref[:, cols] + yacc_ref[:, cols]
            y = _layer_norm(z.T, lw_ref[...], lb_ref[...])
            o_ref[cols, :] = y
            ob_ref[cols, :] = y.astype(BF16)


def _transpose_cast_kernel(*refs):
    *in_refs, o_ref = refs
    rows = 0
    for w_ref in in_refs:
        wt = w_ref[0, 0].astype(F32).T
        o_ref[0, rows:rows + wt.shape[0], :] = wt.astype(o_ref.dtype)
        rows += wt.shape[0]


def transpose_cast(layer, *ws):
    _, ne, r, c = ws[0].shape
    return pl.pallas_call(
        _transpose_cast_kernel,
        grid=(ne,),
        in_specs=[pl.BlockSpec((1, 1, r, c), lambda e: (layer, e, 0, 0)) for _ in ws],
        out_specs=pl.BlockSpec((1, len(ws) * c, r), lambda e: (e, 0, 0)),
        out_shape=jax.ShapeDtypeStruct((ne, len(ws) * c, r), BF16),
        compiler_params=_params("parallel"),
        name="transpose_cast",
    )(*ws)


def _moe_blocks(xt, rwt, rb, wgu_t, wd_t, lw, lb, out_f32, out_bf16, *, bt, first_block, num_blocks):
    d, t = xt.shape
    n = MOE_CHUNK
    max_chunks = pl.cdiv(bt, n)
    assert bt % MOE_SLAB == 0 and (first_block + num_blocks) * bt <= t
    tri = jnp.asarray(np.triu(np.ones((bt, bt), np.float32), k=1), dtype=BF16)
    return pl.pallas_call(
        _moe_kernel,
        grid=(num_blocks, N_EXPERTS // MOE_EXPERTS_PER_STEP),
        in_specs=[pl.BlockSpec((d, bt), lambda i, e: (0, first_block + i)),
                  pl.BlockSpec((N_EXPERTS, d), lambda i, e: (0, 0)),
                  pl.BlockSpec((N_EXPERTS, 1), lambda i, e: (0, 0)),
                  pl.BlockSpec((bt, bt), lambda i, e: (0, 0)),
                  pl.BlockSpec((MOE_EXPERTS_PER_STEP, 2 * D_EXPERT, d), lambda i, e: (e, 0, 0)),
                  pl.BlockSpec((MOE_EXPERTS_PER_STEP, d, D_EXPERT), lambda i, e: (e, 0, 0)),
                  pl.BlockSpec((1, d), lambda i, e: (0, 0)),
                  pl.BlockSpec((1, d), lambda i, e: (0, 0)),
                  pl.BlockSpec(memory_space=pl.ANY),
                  pl.BlockSpec(memory_space=pl.ANY)],
        out_specs=[pl.BlockSpec((bt, d), lambda i, e: (first_block + i, 0)),
                   pl.BlockSpec((bt, d), lambda i, e: (first_block + i, 0))],
        out_shape=[jax.ShapeDtypeStruct((t, d), F32),
                   jax.ShapeDtypeStruct((t, d), BF16)],
        input_output_aliases={8: 0, 9: 1},
        scratch_shapes=[pltpu.VMEM((d + 2 * MOE_GATE_ROWS, bt), BF16),
                        pltpu.VMEM((N_GROUPS, MOE_GATE_ROWS, bt), F32),
                        pltpu.VMEM((bt, 1), F32), pltpu.VMEM((bt, 1), F32),
                        pltpu.VMEM((1, bt), F32), pltpu.VMEM((1, bt), F32),
                        pltpu.SMEM((N_GROUPS,), jnp.int32),
                        pltpu.VMEM((max_chunks, d, n), BF16),
                        pltpu.VMEM((max_chunks, EXPERTS_PER_GROUP, SUBLANES, n), F32),
                        pltpu.VMEM((max_chunks, d, n), F32),
                        pltpu.VMEM((d, bt), F32)],
        compiler_params=pltpu.CompilerParams(dimension_semantics=("arbitrary", "arbitrary"),
                                             vmem_limit_bytes=MOE_VMEM_LIMIT),
        name="moe_residual_ln",
    )(xt, rwt, rb, tri, wgu_t, wd_t, lw.reshape(1, d), lb.reshape(1, d), out_f32, out_bf16)


def moe_residual_ln(xt, rwt, rb, wgu_t, wd_t, lw, lb, buf_f32, buf_bf16):
    d, t = xt.shape
    head_blocks = max((k for k in range(t // MOE_BLOCK + 1)
                       if (t - k * MOE_BLOCK) % MOE_TAIL_BLOCK == 0), default=0)
    tail_start = head_blocks * MOE_BLOCK
    assert (t - tail_start) % MOE_TAIL_BLOCK == 0
    out = (buf_f32, buf_bf16)
    if head_blocks:
        out = _moe_blocks(xt, rwt, rb, wgu_t, wd_t, lw, lb, *out,
                          bt=MOE_BLOCK, first_block=0, num_blocks=head_blocks)
    if tail_start < t:
        out = _moe_blocks(xt, rwt, rb, wgu_t, wd_t, lw, lb, *out, bt=MOE_TAIL_BLOCK,
                          first_block=tail_start // MOE_TAIL_BLOCK,
                          num_blocks=(t - tail_start) // MOE_TAIL_BLOCK)
    return out


def _lambda_init_for(layer):
    return 0.8 - 0.6 * math.exp(-0.3 * layer)


def kernel(x, a_w_in, a_conv, a_A_log, a_dt_bias, a_norm_w, a_w_out, b_w_q, b_w_kv, b_lambda, b_subln_w, b_w_out, rel_bias, router_w, router_b, moe_w_gate, moe_w_up, moe_w_down, ln_w, ln_b):
    bsz, t, d = x.shape
    assert bsz == 1 and d == D_MODEL and t % A_CHUNK == 0
    xf = x.reshape(t, d).astype(F32)
    xb = xf.astype(BF16)

    gate_tile = min(512, t)
    tri = jnp.asarray(np.kron(np.eye(gate_tile // A_CHUNK, dtype=np.float32),
                              np.tril(np.ones((A_CHUNK, A_CHUNK), np.float32))))
    rwt = router_w.T.astype(F32)
    rb = router_b.reshape(N_EXPERTS, 1).astype(F32)
    attn_tile = min(512, t)
    assert attn_tile % BIAS_BLOCK == 0
    bias_tables = attention_bias_tables(rel_bias.astype(F32))

    kb = None
    vt = None
    for l in range(DEPTH):
        if l < N_A_LAYERS:
            w_in = a_w_in[l]
            w_main = w_in[:, :A_CONV_CH + A_V].astype(BF16)
            w_ab = jnp.pad(w_in[:, A_CONV_CH + A_V:], ((0, 0), (0, LANES - 2 * A_HEADS))).astype(BF16)
            conv_pad = jnp.pad(a_conv[l].astype(F32), ((0, 0), (0, A_V)))
            par = jnp.zeros((SUBLANES, LANES), F32)
            par = par.at[0, :A_HEADS].set(a_A_log[l].astype(F32))
            par = par.at[1, :A_HEADS].set(a_dt_bias[l].astype(F32))
            proj = gdn_inproj(xb, w_main, conv_pad)
            fac, fact = gdn_gates(xb, w_ab, par, tri)
            fact3 = fact.reshape(2 * A_HEADS, t // A_CHUNK, A_CHUNK).transpose(1, 0, 2)
            u, wq, qk, kd, dec = gdn_intra(proj, fac, fact3)
            mixed = gdn_scan(u, wq, qk, kd, dec, proj, a_norm_w[l].astype(F32))
            w_out = a_w_out[l].astype(BF16)
        else:
            j = l - N_A_LAYERS
            if kb is None:
                kb = matmul_bf16(xb, b_w_kv[:, :B_QK].astype(BF16))
                vt = v_transposed_proj(xb, b_w_kv[:, B_QK:].T.astype(BF16), attn_tile)
            qb = matmul_bf16(xb, b_w_q[j].astype(BF16), scale=B_DH ** -0.5 * LOG2E)
            mixed = diff_attention(qb, kb, vt, bias_tables, b_lambda[j].astype(F32),
                                   b_subln_w[j].astype(F32), _lambda_init_for(l), tile=attn_tile)
            w_out = b_w_out[j].astype(BF16)
        xt = matmul_residual_ln_t(mixed, w_out, xf, ln_w[l, 0].astype(F32), ln_b[l, 0].astype(F32))
        wgu_t = transpose_cast(l, moe_w_gate, moe_w_up)
        wd_t = transpose_cast(l, moe_w_down)
        xf, xb = moe_residual_ln(xt, rwt, rb, wgu_t, wd_t,
                                 ln_w[l, 1].astype(F32), ln_b[l, 1].astype(F32), xf, xb)
    return xf.reshape(bsz, t, d).astype(x.dtype)
```
